```python
import math
import jax, jax.numpy as jnp
from jax import lax
import numpy as np

D_MODEL = 1024
BATCH = 4
SEQ = 4096
DEPTH = 2
DEC_BATCH = 8
DEC_SEQ = 64
PAST_LEN = 1024

CHUNK = 64
Q_BLOCK = 128
N_EVEN = (DEPTH + 1) // 2
N_ODD = DEPTH // 2
HEAD_DIM = 64
H_A = 4
DK_A = HEAD_DIM
DV_A = 2 * HEAD_DIM
H_B = 8
DK_B = HEAD_DIM
H_C = D_MODEL // HEAD_DIM
C_PAST_CHUNKS = 8
C_BAND_PAST = C_PAST_CHUNKS * CHUNK
REL_CLIP = 256
D_FF = 4 * D_MODEL
NORM_EPS = 1e-6
A_QK = H_A * 2 * DK_A
A_V = H_A * DV_A
B_QK = H_B * DK_B
D_IN_EVEN = 2 * A_QK + A_V + 3 * B_QK + H_B
D_MIX = A_V + B_QK
NEG_INF = -1e30

kernel_name = 'hybrid_streaming_encoder_step'


def rmsnorm(x, g):
    xf = x.astype(jnp.float32)
    y = xf * lax.rsqrt(jnp.mean(xf * xf, axis=-1, keepdims=True) + NORM_EPS)
    return (y * g.astype(jnp.float32)).astype(x.dtype)


def alibi_slopes():
    return 2.0 ** (-8.0 * jnp.arange(1, H_A + 1, dtype=jnp.float32) / H_A)


def diff_attention(q, k, v, q_pos, k_pos, lam):
    s = jnp.einsum('bqhmd,bkhmd->bmhqk', q, k).astype(jnp.float32) * (DK_A ** -0.5)
    dist = jnp.abs(q_pos[:, None] - k_pos[None, :]).astype(jnp.float32)
    bias = -alibi_slopes()[:, None, None] * dist
    visible = (k_pos[None, :] // CHUNK) <= (q_pos[:, None] // CHUNK)
    p = jax.nn.softmax(jnp.where(visible, s + bias, NEG_INF), axis=-1)
    w = p[:, 0] - lam * p[:, 1]
    return jnp.einsum('bhqk,bkhd->bqhd', w.astype(v.dtype), v)


def forgetting_attention(q, k, v, cum_q, cum_k, q_pos, k_pos):
    s = jnp.einsum('bqhd,bkhd->bhqk', q, k).astype(jnp.float32) * (DK_B ** -0.5)
    decay = jnp.swapaxes(cum_q, 1, 2)[..., :, None] - jnp.swapaxes(cum_k, 1, 2)[..., None, :]
    visible = k_pos[None, :] <= q_pos[:, None]
    p = jax.nn.softmax(jnp.where(visible, s + decay, NEG_INF), axis=-1)
    return jnp.einsum('bhqk,bkhd->bqhd', p.astype(v.dtype), v)


def chunk_band_attention(q, k, v, q_pos, k_pos, table):
    s = jnp.einsum('bqhd,bkhd->bhqk', q, k).astype(jnp.float32) * (HEAD_DIM ** -0.5)
    rel = jnp.clip(q_pos[:, None] - k_pos[None, :], -REL_CLIP, REL_CLIP) + REL_CLIP
    bias = table.astype(jnp.float32)[:, rel]
    qc = q_pos[:, None] // CHUNK
    kc = k_pos[None, :] // CHUNK
    visible = (k_pos[None, :] >= 0) & (kc <= qc) & (kc >= qc - C_PAST_CHUNKS)
    p = jax.nn.softmax(jnp.where(visible, s + bias, NEG_INF), axis=-1)
    return jnp.einsum('bhqk,bkhd->bqhd', p.astype(v.dtype), v)


def diff_lambda(lq1, lk1, lq2, lk2, lam_init):
    f = jnp.float32
    return (jnp.exp(jnp.sum(lq1.astype(f) * lk1.astype(f)))
            - jnp.exp(jnp.sum(lq2.astype(f) * lk2.astype(f))) + lam_init)


def even_project(h, w_in, b_f):
    B, T, _ = h.shape
    z = h @ w_in
    cuts = np.cumsum([A_QK, A_QK, A_V, B_QK, B_QK, B_QK]).tolist()
    qa, ka, va, qb, kb, vb, fz = jnp.split(z, cuts, axis=-1)
    qa = qa.reshape(B, T, H_A, 2, DK_A)
    ka = ka.reshape(B, T, H_A, 2 * DK_A)
    va = va.reshape(B, T, H_A, DV_A)
    qb = qb.reshape(B, T, H_B, DK_B)
    kb = kb.reshape(B, T, H_B, DK_B)
    vb = vb.reshape(B, T, H_B, DK_B)
    logf = jax.nn.log_sigmoid(fz.astype(jnp.float32) + b_f.astype(jnp.float32))
    return qa, ka, va, qb, kb, vb, logf


def even_merge(oa, ob, subln_g, lam_init, w_out):
    B, T = oa.shape[:2]
    oa = rmsnorm(oa, subln_g) * (1.0 - lam_init)
    o = jnp.concatenate([oa.reshape(B, T, A_V), ob.reshape(B, T, B_QK)], axis=-1)
    return o @ w_out


def even_mixer_prompt(h, w_in, b_f, lam, subln_g, lam_init, w_out):
    B, T, _ = h.shape
    qa, ka, va, qb, kb, vb, logf = even_project(h, w_in, b_f)
    cum = jnp.cumsum(logf, axis=1)
    pos = jnp.arange(T, dtype=jnp.int32)
    ka5 = ka.reshape(B, T, H_A, 2, DK_A)

    def block(i):
        start = i * Q_BLOCK
        q_pos = start + jnp.arange(Q_BLOCK, dtype=jnp.int32)
        oa = diff_attention(lax.dynamic_slice_in_dim(qa, start, Q_BLOCK, axis=1), ka5, va, q_pos, pos, lam)
        ob = forgetting_attention(lax.dynamic_slice_in_dim(qb, start, Q_BLOCK, axis=1), kb, vb,
                                  lax.dynamic_slice_in_dim(cum, start, Q_BLOCK, axis=1), cum, q_pos, pos)
        return oa, ob

    oa, ob = lax.map(block, jnp.arange(T // Q_BLOCK, dtype=jnp.int32))
    oa = jnp.moveaxis(oa, 0, 1).reshape(B, T, H_A, DV_A)
    ob = jnp.moveaxis(ob, 0, 1).reshape(B, T, H_B, DK_B)
    return even_merge(oa, ob, subln_g, lam_init, w_out), (ka, va, kb, vb, logf)


def even_mixer_sample(h, c_ka, c_va, c_kb, c_vb, c_logf, w_in, b_f, lam, subln_g, lam_init, w_out):
    B, T, _ = h.shape
    P = c_ka.shape[1]
    qa, ka, va, qb, kb, vb, logf = even_project(h, w_in, b_f)
    ka_all = jnp.concatenate([c_ka, ka], axis=1).reshape(B, P + T, H_A, 2, DK_A)
    va_all = jnp.concatenate([c_va, va], axis=1)
    kb_all = jnp.concatenate([c_kb, kb], axis=1)
    vb_all = jnp.concatenate([c_vb, vb], axis=1)
    cum = jnp.cumsum(jnp.concatenate([c_logf.astype(jnp.float32), logf], axis=1), axis=1)
    q_pos = P + jnp.arange(T, dtype=jnp.int32)
    k_pos = jnp.arange(P + T, dtype=jnp.int32)
    oa = diff_attention(qa, ka_all, va_all, q_pos, k_pos, lam)
    ob = forgetting_attention(qb, kb_all, vb_all, cum[:, P:], cum, q_pos, k_pos)
    return even_merge(oa, ob, subln_g, lam_init, w_out), (ka, va, kb, vb, logf)


def odd_project(h, w_in):
    B, T, _ = h.shape
    q, k, v = jnp.split(h @ w_in, 3, axis=-1)
    return (q.reshape(B, T, H_C, HEAD_DIM), k.reshape(B, T, H_C, HEAD_DIM), v.reshape(B, T, H_C, HEAD_DIM))


def odd_mixer_prompt(h, w_in, table, w_out):
    B, T, _ = h.shape
    q, k, v = odd_project(h, w_in)
    pad = ((0, 0), (C_BAND_PAST, 0), (0, 0), (0, 0))
    kp = jnp.pad(k, pad)
    vp = jnp.pad(v, pad)
    band = C_BAND_PAST + CHUNK

    def chunk_fn(c):
        start = c * CHUNK
        q_pos = start + jnp.arange(CHUNK, dtype=jnp.int32)
        k_pos = start - C_BAND_PAST + jnp.arange(band, dtype=jnp.int32)
        return chunk_band_attention(lax.dynamic_slice_in_dim(q, start, CHUNK, axis=1),
                                    lax.dynamic_slice_in_dim(kp, start, band, axis=1),
                                    lax.dynamic_slice_in_dim(vp, start, band, axis=1),
                                    q_pos, k_pos, table)

    o = lax.map(chunk_fn, jnp.arange(T // CHUNK, dtype=jnp.int32))
    o = jnp.moveaxis(o, 0, 1).reshape(B, T, D_MODEL)
    W = min(C_BAND_PAST, T)
    return o @ w_out, (k[:, T - W:], v[:, T - W:])


def odd_mixer_sample(h, c_k, c_v, past_len, w_in, table, w_out):
    B, T, _ = h.shape
    W = c_k.shape[1]
    q, k, v = odd_project(h, w_in)
    k_all = jnp.concatenate([c_k, k], axis=1)
    v_all = jnp.concatenate([c_v, v], axis=1)
    q_pos = past_len + jnp.arange(T, dtype=jnp.int32)
    k_pos = past_len - W + jnp.arange(W + T, dtype=jnp.int32)
    o = chunk_band_attention(q, k_all, v_all, q_pos, k_pos, table).reshape(B, T, D_MODEL)
    return o @ w_out, (k_all[:, T:], v_all[:, T:])


def squared_relu_mlp(h, w_up, w_down):
    u = jax.nn.relu(h @ w_up)
    return (u * u) @ w_down


def setup_inputs(seed: int = 0) -> dict:
    key = jax.random.key(seed)
    ks = jax.random.split(key, 32)

    def nrm(k, shape, scale=1.0):
        return scale * jax.random.normal(k, shape, jnp.float32)

    w_c = min(C_BAND_PAST, PAST_LEN)
    return {
        'x_prompt': nrm(ks[0], (BATCH, SEQ, D_MODEL)),
        'x_sample': nrm(ks[1], (DEC_BATCH, DEC_SEQ, D_MODEL)),
        'cache_a_k': nrm(ks[2], (N_EVEN, DEC_BATCH, PAST_LEN, H_A, 2 * DK_A)),
        'cache_a_v': nrm(ks[3], (N_EVEN, DEC_BATCH, PAST_LEN, H_A, DV_A)),
        'cache_b_k': nrm(ks[4], (N_EVEN, DEC_BATCH, PAST_LEN, H_B, DK_B)),
        'cache_b_v': nrm(ks[5], (N_EVEN, DEC_BATCH, PAST_LEN, H_B, DK_B)),
        'cache_b_logf': jax.nn.log_sigmoid(2.0 + nrm(ks[6], (N_EVEN, DEC_BATCH, PAST_LEN, H_B), 0.5)),
        'cache_c_k': nrm(ks[7], (N_ODD, DEC_BATCH, w_c, H_C, HEAD_DIM)),
        'cache_c_v': nrm(ks[8], (N_ODD, DEC_BATCH, w_c, H_C, HEAD_DIM)),
        'w_in_even': nrm(ks[9], (N_EVEN, D_MODEL, D_IN_EVEN), D_MODEL ** -0.5),
        'b_forget': 2.0 + nrm(ks[10], (N_EVEN, H_B), 0.1),
        'lam_q1': nrm(ks[11], (N_EVEN, DK_A), 0.1),
        'lam_k1': nrm(ks[12], (N_EVEN, DK_A), 0.1),
        'lam_q2': nrm(ks[13], (N_EVEN, DK_A), 0.1),
        'lam_k2': nrm(ks[14], (N_EVEN, DK_A), 0.1),
        'subln_g': 1.0 + nrm(ks[15], (N_EVEN, DV_A), 0.05),
        'w_out_even': nrm(ks[16], (N_EVEN, D_MIX, D_MODEL), D_MIX ** -0.5),
        'w_in_odd': nrm(ks[17], (N_ODD, D_MODEL, 3 * D_MODEL), D_MODEL ** -0.5),
        'rel_bias': nrm(ks[18], (N_ODD, H_C, 2 * REL_CLIP + 1), 0.5),
        'w_out_odd': nrm(ks[19], (N_ODD, D_MODEL, D_MODEL), D_MODEL ** -0.5),
        'g_pre_mix': 1.0 + nrm(ks[20], (DEPTH, D_MODEL), 0.05),
        'g_post_mix': 1.0 + nrm(ks[21], (DEPTH, D_MODEL), 0.05),
        'g_pre_ffn': 1.0 + nrm(ks[22], (DEPTH, D_MODEL), 0.05),
        'g_post_ffn': 1.0 + nrm(ks[23], (DEPTH, D_MODEL), 0.05),
        'w_ffn_up': nrm(ks[24], (DEPTH, D_MODEL, D_FF), D_MODEL ** -0.5),
        'w_ffn_down': nrm(ks[25], (DEPTH, D_FF, D_MODEL), D_FF ** -0.5),
    }


def reference(x_prompt, x_sample, cache_a_k, cache_a_v, cache_b_k, cache_b_v, cache_b_logf,
              cache_c_k, cache_c_v, w_in_even, b_forget, lam_q1, lam_k1, lam_q2, lam_k2,
              subln_g, w_out_even, w_in_odd, rel_bias, w_out_odd, g_pre_mix, g_post_mix,
              g_pre_ffn, g_post_ffn, w_ffn_up, w_ffn_down):
    past_len = cache_b_logf.shape[2]
    xp, xs = x_prompt, x_sample
    even_p, even_s, odd_p, odd_s = [], [], [], []
    for l in range(DEPTH):
        hp = rmsnorm(xp, g_pre_mix[l])
        hs = rmsnorm(xs, g_pre_mix[l])
        if l % 2 == 0:
            e = l // 2
            lam_init = 0.8 - 0.6 * math.exp(-0.3 * l)
            lam = diff_lambda(lam_q1[e], lam_k1[e], lam_q2[e], lam_k2[e], lam_init)
            mp, rp = even_mixer_prompt(hp, w_in_even[e], b_forget[e], lam, subln_g[e], lam_init, w_out_even[e])
            ms, rs = even_mixer_sample(hs, cache_a_k[e], cache_a_v[e], cache_b_k[e], cache_b_v[e], cache_b_logf[e],
                                       w_in_even[e], b_forget[e], lam, subln_g[e], lam_init, w_out_even[e])
            even_p.append(rp)
            even_s.append(rs)
        else:
            o = l // 2
            mp, rp = odd_mixer_prompt(hp, w_in_odd[o], rel_bias[o], w_out_odd[o])
            ms, rs = odd_mixer_sample(hs, cache_c_k[o], cache_c_v[o], past_len, w_in_odd[o], rel_bias[o], w_out_odd[o])
            odd_p.append(rp)
            odd_s.append(rs)
        xp = xp + rmsnorm(mp, g_post_mix[l])
        xs = xs + rmsnorm(ms, g_post_mix[l])
        xp = xp + rmsnorm(squared_relu_mlp(rmsnorm(xp, g_pre_ffn[l]), w_ffn_up[l], w_ffn_down[l]), g_post_ffn[l])
        xs = xs + rmsnorm(squared_relu_mlp(rmsnorm(xs, g_pre_ffn[l]), w_ffn_up[l], w_ffn_down[l]), g_post_ffn[l])

    new_pa_k = jnp.stack([r[0] for r in even_p])
    new_pa_v = jnp.stack([r[1] for r in even_p])
    new_pb_k = jnp.stack([r[2] for r in even_p])
    new_pb_v = jnp.stack([r[3] for r in even_p])
    new_pb_logf = jnp.stack([r[4] for r in even_p])
    new_pc_k = jnp.stack([r[0] for r in odd_p])
    new_pc_v = jnp.stack([r[1] for r in odd_p])
    new_sa_k = jnp.stack([r[0] for r in even_s])
    new_sa_v = jnp.stack([r[1] for r in even_s])
    new_sb_k = jnp.stack([r[2] for r in even_s])
    new_sb_v = jnp.stack([r[3] for r in even_s])
    new_sb_logf = jnp.stack([r[4] for r in even_s])
    new_sc_k = jnp.stack([r[0] for r in odd_s])
    new_sc_v = jnp.stack([r[1] for r in odd_s])
    return (xp, xs, new_pa_k, new_pa_v, new_pb_k, new_pb_v, new_pb_logf, new_pc_k, new_pc_v,
            new_sa_k, new_sa_v, new_sb_k, new_sb_v, new_sb_logf, new_sc_k, new_sc_v)
```

```python
import functools
import math

import numpy as np
import jax
import jax.numpy as jnp
from jax import lax
from jax.experimental import pallas as pl
from jax.experimental.pallas import tpu as pltpu

F32 = jnp.float32
BF16 = jnp.bfloat16

D_MODEL = 1024
HEAD_DIM = 64
CHUNK = 64
H_A = 4
H_B = 8
H_C = 16
A_W = 512
N_SEG = 6
BAND_PAST = 512
REL_CLIP = 256
D_FF = 4 * D_MODEL
NORM_EPS = 1e-6
NEG_INF = -1e30
QK_SCALE = HEAD_DIM ** -0.5

LANES = 128
ROW_TILE = 512
FF_CHUNK = 1024
VMEM_LIMIT = 56 * 1024 * 1024

_NT = (((1,), (1,)), ((), ()))


def _rms(x, g):
    ms = jnp.mean(x * x, axis=-1, keepdims=True)
    return x * lax.rsqrt(ms + NORM_EPS) * g


def _log_sigmoid(x):
    t = -x
    return -(jnp.maximum(t, 0.0) + jnp.log1p(jnp.exp(-jnp.abs(t))))


def _const_spec(shape):
    nd = len(shape)
    return pl.BlockSpec(shape, lambda *_: (0,) * nd, pipeline_mode=pl.Buffered(1))


def _params(n_axes):
    return pltpu.CompilerParams(dimension_semantics=("arbitrary",) * n_axes,
                                vmem_limit_bytes=VMEM_LIMIT)


def _proj_even_kernel(x_ref, g_ref, w_ref, wf_ref, wft_ref, bf_ref, bft_ref,
                      qa_ref, ka_ref, kab_ref, va_ref, vab_ref,
                      qb_ref, kb_ref, kbb_ref, vb_ref, vbb_ref, lf_ref, lft_ref):
    h = _rms(x_ref[...], g_ref[...]).astype(BF16)

    def seg(i):
        return jnp.dot(h, w_ref[:, i * A_W:(i + 1) * A_W], preferred_element_type=F32)

    qa_ref[...] = (seg(0) * QK_SCALE).astype(BF16)
    z = seg(1)
    ka_ref[...] = z
    kab_ref[...] = z.astype(BF16)
    z = seg(2)
    va_ref[...] = z
    vab_ref[...] = z.astype(BF16)
    qb_ref[...] = (seg(3) * QK_SCALE).astype(BF16)
    z = seg(4)
    kb_ref[...] = z
    kbb_ref[...] = z.astype(BF16)
    z = seg(5)
    vb_ref[...] = z
    vbb_ref[...] = z.astype(BF16)
    fz = jnp.dot(h, wf_ref[...], preferred_element_type=F32)
    lf_ref[...] = _log_sigmoid(fz + bf_ref[...])[:, :H_B]
    fzt = lax.dot_general(wft_ref[...], h, _NT, preferred_element_type=F32)
    lft_ref[0] = _log_sigmoid(fzt[:H_B] + bft_ref[...])


def _proj_even(x, g, w_main, wf, wft, bf_row, bf_col):
    m = x.shape[0]
    tm = min(ROW_TILE, m)
    n = m // tm
    row = lambda w: pl.BlockSpec((tm, w), lambda i: (i, 0))
    f32o = jax.ShapeDtypeStruct((m, A_W), F32)
    bfo = jax.ShapeDtypeStruct((m, A_W), BF16)
    out_shape = [bfo, f32o, bfo, f32o, bfo, bfo, f32o, bfo, f32o, bfo,
                 jax.ShapeDtypeStruct((m, H_B), F32),
                 jax.ShapeDtypeStruct((n, H_B, tm), F32)]
    out_specs = [row(A_W)] * 10 + [row(H_B), pl.BlockSpec((1, H_B, tm), lambda i: (i, 0, 0))]
    return pl.pallas_call(
        _proj_even_kernel,
        grid=(n,),
        in_specs=[row(D_MODEL), _const_spec((1, D_MODEL)), _const_spec(w_main.shape),
                  _const_spec(wf.shape), _const_spec(wft.shape),
                  _const_spec(bf_row.shape), _const_spec(bf_col.shape)],
        out_specs=out_specs,
        out_shape=out_shape,
        compiler_params=_params(1),
        name="proj_even",
    )(x, g, w_main, wf, wft, bf_row, bf_col)


def _cumsum_kernel(x_ref, o_ref):
    x = x_ref[0]
    t = x.shape[1]
    col = lax.broadcasted_iota(jnp.int32, x.shape, 1)
    s = 1
    while s < t:
        x = x + jnp.where(col >= s, pltpu.roll(x, s, 1), 0.0)
        s *= 2
    o_ref[0] = x


def _cumsum_time(x):
    b, h, t = x.shape
    spec = pl.BlockSpec((1, h, t), lambda i: (i, 0, 0))
    return pl.pallas_call(
        _cumsum_kernel, grid=(b,), in_specs=[spec], out_specs=spec,
        out_shape=jax.ShapeDtypeStruct(x.shape, F32),
        compiler_params=_params(1), name="cumsum_time",
    )(x)


def _stack_pair(q):
    q32 = q.astype(F32)
    lo = lax.broadcasted_iota(jnp.int32, q32.shape, 1) < HEAD_DIM
    return jnp.concatenate([jnp.where(lo, q32, 0.0), jnp.where(lo, 0.0, q32)], axis=0).astype(BF16)


def _online_update(a, t, vb, m_sc, l_sc, acc_sc):
    m_prev = m_sc[a]
    m_next = jnp.maximum(m_prev, jnp.max(t, axis=1, keepdims=True))
    p = jnp.exp(t - jnp.tile(m_next, (1, t.shape[1] // LANES)))
    alpha = jnp.exp(m_prev - m_next)
    l_sc[a] = alpha * l_sc[a] + jnp.sum(p, axis=1, keepdims=True)
    acc_sc[a] = alpha * acc_sc[a] + jnp.dot(p.astype(BF16), vb, preferred_element_type=F32)
    m_sc[a] = m_next


def _flash_kernel(*refs, mode, tq, tk, q_pos0, lam_init):
    if mode == "diff":
        slopes_ref, lamv_ref, g_ref, q_ref, k_ref, v_ref, o_ref, m_sc, l_sc, acc_sc = refs
    else:
        cum_ref, q_ref, k_ref, v_ref, o_ref, m_sc, l_sc, acc_sc = refs
    pair = pl.program_id(1)
    q0 = q_pos0 + pl.program_id(2) * tq
    n_int = q0 // tk

    q2 = _stack_pair(q_ref[0])
    m_sc[...] = jnp.full(m_sc.shape, NEG_INF, F32)
    l_sc[...] = jnp.zeros(l_sc.shape, F32)
    acc_sc[...] = jnp.zeros(acc_sc.shape, F32)

    row = lax.broadcasted_iota(jnp.int32, (tq, tk), 0)
    col = lax.broadcasted_iota(jnp.int32, (tq, tk), 1)
    if mode == "diff":
        neg_slope = -slopes_ref[pair]
        rel = (row - col).astype(F32)
        bias_int = neg_slope * rel
        same_chunk_or_past = (col >> 6) <= (row >> 6)
        bias_diag = jnp.where(same_chunk_or_past, neg_slope * jnp.abs(rel), NEG_INF)
    else:
        mask_diag = jnp.where(col <= row, 0.0, NEG_INF)

    def scores(k0):
        kb = k_ref[0, pl.ds(k0, tk), :]
        vb = v_ref[0, pl.ds(k0, tk), :]
        return lax.dot_general(q2, kb, _NT, preferred_element_type=F32), vb

    def cum_row(a, k0):
        return cum_ref[0, 2 * pair + a, :, pl.ds(k0, tk)]

    def interior(j, carry):
        k0 = pl.multiple_of(j * tk, tk)
        s, vb = scores(k0)
        for a in range(2):
            sa = s[a * tq:(a + 1) * tq]
            if mode == "diff":
                t = (sa + bias_int) + neg_slope * (q0 - k0).astype(F32)
            else:
                t = sa - cum_row(a, k0)
            _online_update(a, t, vb, m_sc, l_sc, acc_sc)
        return carry

    lax.fori_loop(0, n_int, interior, 0)

    k0 = pl.multiple_of(n_int * tk, tk)
    s, vb = scores(k0)
    for a in range(2):
        sa = s[a * tq:(a + 1) * tq]
        if mode == "diff":
            t = sa + bias_diag
        else:
            t = (sa - cum_row(a, k0)) + mask_diag
        _online_update(a, t, vb, m_sc, l_sc, acc_sc)

    o0 = acc_sc[0] / l_sc[0]
    o1 = acc_sc[1] / l_sc[1]
    if mode == "diff":
        lv = lamv_ref[...]
        lam = (jnp.exp(jnp.sum(lv[0:1] * lv[1:2], axis=1, keepdims=True))
               - jnp.exp(jnp.sum(lv[2:3] * lv[3:4], axis=1, keepdims=True)) + lam_init)
        oa = o0 - lam * o1
        o_ref[0] = (_rms(oa, g_ref[...]) * (1.0 - lam_init)).astype(BF16)
    else:
        lo = lax.broadcasted_iota(jnp.int32, o0.shape, 1) < HEAD_DIM
        o_ref[0] = jnp.where(lo, o0, o1).astype(BF16)


def _flash(mode, q, k, v, extras, *, tq, tk, q_pos0, lam_init=0.0):
    b, t_q, width = q.shape
    t_k = k.shape[1]
    pairs = width // LANES
    grid = (b, pairs, t_q // tq)
    qspec = pl.BlockSpec((1, tq, LANES), lambda bi, p, i: (bi, i, p))
    kvspec = pl.BlockSpec((1, t_k, LANES), lambda bi, p, i: (bi, 0, p))
    if mode == "diff":
        slopes, lamv, g = extras
        extra_specs = [pl.BlockSpec(memory_space=pltpu.SMEM),
                       pl.BlockSpec(lamv.shape, lambda bi, p, i: (0, 0)),
                       pl.BlockSpec(g.shape, lambda bi, p, i: (0, 0))]
    else:
        extras = (extras[0][:, :, None, :],)
        extra_specs = [pl.BlockSpec((1,) + extras[0].shape[1:], lambda bi, p, i: (bi, 0, 0, 0))]
    return pl.pallas_call(
        functools.partial(_flash_kernel, mode=mode, tq=tq, tk=tk, q_pos0=q_pos0, lam_init=lam_init),
        grid=grid,
        in_specs=extra_specs + [qspec, kvspec, kvspec],
        out_specs=qspec,
        out_shape=jax.ShapeDtypeStruct(q.shape, BF16),
        scratch_shapes=[pltpu.VMEM((2, tq, LANES), F32)] * 3,
        compiler_params=_params(3),
        name="flash_" + mode,
    )(*extras, q, k, v)


def _band_kernel(c_ref, q_ref, k_ref, v_ref, o_ref, *, tq, window, q_pos_base, windowed):
    qi = pl.program_id(2)
    q_pos0 = q_pos_base + qi * tq
    k_pos0 = q_pos0 - BAND_PAST
    w0 = pl.multiple_of(qi * tq, tq) if windowed else 0
    kb = k_ref[0, pl.ds(w0, window), :]
    vb = v_ref[0, pl.ds(w0, window), :]
    s = lax.dot_general(_stack_pair(q_ref[0]), kb, _NT, preferred_element_type=F32)

    row = lax.broadcasted_iota(jnp.int32, (tq, window), 0)
    col = lax.broadcasted_iota(jnp.int32, (tq, window), 1)
    k_pos = k_pos0 + col
    qc = lax.shift_right_arithmetic(q_pos0 + row, 6)
    kc = lax.shift_right_arithmetic(k_pos, 6)
    visible = (k_pos >= 0) & (kc <= qc) & (kc >= qc - BAND_PAST // CHUNK)

    outs = []
    for a in range(2):
        c = jnp.broadcast_to(c_ref[0, a:a + 1, :], (tq, c_ref.shape[2]))
        bias = pltpu.roll(c, 0, 1, stride=1, stride_axis=0)[:, :window]
        t = jnp.where(visible, s[a * tq:(a + 1) * tq] + bias, NEG_INF)
        p = jnp.exp(t - jnp.max(t, axis=1, keepdims=True))
        l = jnp.sum(p, axis=1, keepdims=True)
        outs.append(jnp.dot(p.astype(BF16), vb, preferred_element_type=F32) / l)
    lo = lax.broadcasted_iota(jnp.int32, outs[0].shape, 1) < HEAD_DIM
    o_ref[0] = jnp.where(lo, outs[0], outs[1]).astype(BF16)


def _band(q, k, v, cvec, *, tq, window, q_pos_base, windowed):
    b, t_q, width = q.shape
    t_k = k.shape[1]
    pairs = width // LANES
    qspec = pl.BlockSpec((1, tq, LANES), lambda bi, p, i: (bi, i, p))
    kvspec = pl.BlockSpec((1, t_k, LANES), lambda bi, p, i: (bi, 0, p))
    cspec = pl.BlockSpec((1, 2, cvec.shape[2]), lambda bi, p, i: (p, 0, 0))
    return pl.pallas_call(
        functools.partial(_band_kernel, tq=tq, window=window, q_pos_base=q_pos_base, windowed=windowed),
        grid=(b, pairs, t_q // tq),
        in_specs=[cspec, qspec, kvspec, kvspec],
        out_specs=qspec,
        out_shape=jax.ShapeDtypeStruct(q.shape, BF16),
        compiler_params=_params(3),
        name="band_attn",
    )(cvec, q, k, v)


def _proj_odd_kernel(x_ref, g_ref, w_ref, q_ref, k_ref, v_ref, kt_ref, vt_ref, *, tail_every):
    h = _rms(x_ref[...], g_ref[...]).astype(BF16)

    def seg(i):
        return jnp.dot(h, w_ref[:, i * D_MODEL:(i + 1) * D_MODEL], preferred_element_type=F32)

    q_ref[...] = (seg(0) * QK_SCALE).astype(BF16)
    kz = seg(1)
    k_ref[...] = kz.astype(BF16)
    vz = seg(2)
    v_ref[...] = vz.astype(BF16)

    @pl.when(pl.program_id(0) % tail_every == tail_every - 1)
    def _():
        kt_ref[...] = kz
        vt_ref[...] = vz


def _proj_odd(x, g, w, *, tail_every):
    m = x.shape[0]
    tm = min(ROW_TILE, m)
    n = m // tm
    row = pl.BlockSpec((tm, D_MODEL), lambda i: (i, 0))
    tail = pl.BlockSpec((tm, D_MODEL), lambda i: (i // tail_every, 0))
    bfo = jax.ShapeDtypeStruct((m, D_MODEL), BF16)
    tailo = jax.ShapeDtypeStruct((n // tail_every * tm, D_MODEL), F32)
    return pl.pallas_call(
        functools.partial(_proj_odd_kernel, tail_every=tail_every),
        grid=(n,),
        in_specs=[row, _const_spec((1, D_MODEL)), _const_spec(w.shape)],
        out_specs=[row, row, row, tail, tail],
        out_shape=[bfo, bfo, bfo, tailo, tailo],
        compiler_params=_params(1),
        name="proj_odd",
    )(x, g, w)


def _post_kernel(*refs, n_o):
    o_refs = refs[:n_o]
    x_ref, wo_ref, gpm_ref, gpre_ref, wup_ref, wdn_ref, gpf_ref, out_ref = refs[n_o:]
    o = o_refs[0][...] if n_o == 1 else jnp.concatenate([r[...] for r in o_refs], axis=1)
    mixed = jnp.dot(o, wo_ref[...], preferred_element_type=F32)
    x1 = x_ref[...] + _rms(mixed, gpm_ref[...])
    h = _rms(x1, gpre_ref[...]).astype(BF16)
    acc = jnp.zeros(x1.shape, F32)
    for c in range(D_FF // FF_CHUNK):
        u = jnp.dot(h, wup_ref[:, c * FF_CHUNK:(c + 1) * FF_CHUNK], preferred_element_type=F32)
        u = jnp.maximum(u, 0.0)
        acc = acc + jnp.dot((u * u).astype(BF16), wdn_ref[c * FF_CHUNK:(c + 1) * FF_CHUNK, :],
                            preferred_element_type=F32)
    out_ref[...] = x1 + _rms(acc, gpf_ref[...])


def _post(o_list, x, wo, gpm, gpre, wup, wdn, gpf):
    m = x.shape[0]
    tm = min(ROW_TILE, m)
    row = lambda w: pl.BlockSpec((tm, w), lambda i: (i, 0))
    gspec = _const_spec((1, D_MODEL))
    return pl.pallas_call(
        functools.partial(_post_kernel, n_o=len(o_list)),
        grid=(m // tm,),
        in_specs=[row(o.shape[1]) for o in o_list]
                 + [row(D_MODEL), _const_spec(wo.shape), gspec, gspec,
                    _const_spec(wup.shape), _const_spec(wdn.shape), gspec],
        out_specs=row(D_MODEL),
        out_shape=jax.ShapeDtypeStruct((m, D_MODEL), F32),
        compiler_params=_params(1),
        name="post_mix_mlp",
    )(*o_list, x, wo, gpm, gpre, wup, wdn, gpf)


def _band_bias_vectors(table):
    n = 2 * BAND_PAST
    m = np.arange(n)
    d = np.where(m <= n - REL_CLIP, -m, n - m)
    idx = np.clip(d + BAND_PAST, -REL_CLIP, REL_CLIP) + REL_CLIP
    return jnp.take(table.astype(F32), jnp.asarray(idx, jnp.int32), axis=1).reshape(H_C // 2, 2, n)


def _pad_time(x, front, back):
    return jnp.pad(x, ((0, 0), (front, back), (0, 0)))


def kernel(x_prompt, x_sample, cache_a_k, cache_a_v, cache_b_k, cache_b_v, cache_b_logf, cache_c_k, cache_c_v,
           w_in_even, b_forget, lam_q1, lam_k1, lam_q2, lam_k2, subln_g, w_out_even, w_in_odd, rel_bias,
           w_out_odd, g_pre_mix, g_post_mix, g_pre_ffn, g_post_ffn, w_ffn_up, w_ffn_down):
    b_p, t_p, _ = x_prompt.shape
    b_s, t_s, _ = x_sample.shape
    past = cache_b_logf.shape[2]
    m_p, m_s = b_p * t_p, b_s * t_s
    xp = x_prompt.reshape(m_p, D_MODEL)
    xs = x_sample.reshape(m_s, D_MODEL)
    gvec = lambda a, l: a[l].reshape(1, D_MODEL)

    lam_init = 0.8 - 0.6 * math.exp(-0.3 * 0)
    w_in = w_in_even[0]
    w_main = w_in[:, :N_SEG * A_W].astype(BF16)
    w_f = w_in[:, N_SEG * A_W:]
    wf = jnp.pad(w_f, ((0, 0), (0, LANES - H_B))).astype(BF16)
    wft = jnp.pad(w_f.T, ((0, 16 - H_B), (0, 0))).astype(BF16)
    bf_row = jnp.pad(b_forget[0], (0, LANES - H_B)).reshape(1, LANES)
    bf_col = b_forget[0].reshape(H_B, 1)
    slopes = 2.0 ** (-8.0 * jnp.arange(1, H_A + 1, dtype=F32) / H_A)
    lamv = jnp.stack([lam_q1[0], lam_k1[0], lam_q2[0], lam_k2[0]]).astype(F32)
    sub_g = subln_g[0].reshape(1, 2 * HEAD_DIM)
    wo0 = w_out_even[0].astype(BF16)
    wup0, wdn0 = w_ffn_up[0].astype(BF16), w_ffn_down[0].astype(BF16)

    def even_layer(x, bsz, t, caches):
        (qa, ka, kab, va, vab, qb, kb, kbb, vb, vbb, lf, lft) = _proj_even(
            x, gvec(g_pre_mix, 0), w_main, wf, wft, bf_row, bf_col)
        m = x.shape[0]
        tm = lft.shape[2]
        r3 = lambda a: a.reshape(bsz, t, A_W)
        lft = lft.transpose(1, 0, 2).reshape(H_B, bsz, t).transpose(1, 0, 2)
        if caches is None:
            kab3, vab3, kbb3, vbb3 = r3(kab), r3(vab), r3(kbb), r3(vbb)
            cum = _cumsum_time(lft)
            tq = tk = 512
            q_pos0 = 0
        else:
            c_ka, c_va, c_kb, c_vb, c_lf = caches
            p_len = c_ka.shape[1]
            t_all = p_len + t
            t_pad = -(-t_all // LANES) * LANES
            cat = lambda c, new: _pad_time(
                jnp.concatenate([c.reshape(bsz, p_len, A_W).astype(BF16), r3(new)], axis=1), 0, t_pad - t_all)
            kab3, vab3, kbb3, vbb3 = cat(c_ka, kab), cat(c_va, vab), cat(c_kb, kbb), cat(c_vb, vbb)
            lf_all = jnp.concatenate([c_lf.astype(F32).transpose(0, 2, 1), lft], axis=2)
            cum = _cumsum_time(jnp.pad(lf_all, ((0, 0), (0, 0), (0, t_pad - t_all))))
            tq, tk = t, LANES
            q_pos0 = p_len
        oa = _flash("diff", r3(qa), kab3, vab3, (slopes, lamv, sub_g), tq=tq, tk=tk, q_pos0=q_pos0,
                    lam_init=lam_init)
        ob = _flash("fox", r3(qb), kbb3, vbb3, (cum,), tq=tq, tk=tk, q_pos0=q_pos0)
        x_out = _post([oa.reshape(m, A_W), ob.reshape(m, A_W)], x, wo0, gvec(g_post_mix, 0),
                      gvec(g_pre_ffn, 0), wup0, wdn0, gvec(g_post_ffn, 0))
        new = (ka.reshape(1, bsz, t, H_A, 2 * HEAD_DIM), va.reshape(1, bsz, t, H_A, 2 * HEAD_DIM),
               kb.reshape(1, bsz, t, H_B, HEAD_DIM), vb.reshape(1, bsz, t, H_B, HEAD_DIM),
               lf.reshape(1, bsz, t, H_B))
        return x_out, new

    xp, new_p_even = even_layer(xp, b_p, t_p, None)
    xs, new_s_even = even_layer(xs, b_s, t_s, (cache_a_k[0], cache_a_v[0], cache_b_k[0], cache_b_v[0],
                                              cache_b_logf[0]))

    w_odd = w_in_odd[0].astype(BF16)
    cvec = _band_bias_vectors(rel_bias[0])
    wo1 = w_out_odd[0].astype(BF16)
    wup1, wdn1 = w_ffn_up[1].astype(BF16), w_ffn_down[1].astype(BF16)

    def odd_layer(x, bsz, t, caches):
        m = x.shape[0]
        tm = min(ROW_TILE, m)
        tail_every = (t // tm) if caches is None else 1
        q, k, v, k_tail, v_tail = _proj_odd(x, gvec(g_pre_mix, 1), w_odd, tail_every=tail_every)
        r3 = lambda a: a.reshape(bsz, t, D_MODEL)
        if caches is None:
            kp, vp = _pad_time(r3(k), BAND_PAST, 0), _pad_time(r3(v), BAND_PAST, 0)
            tq = 256
            o = _band(r3(q), kp, vp, cvec, tq=tq, window=BAND_PAST + tq, q_pos_base=0, windowed=True)
            w_keep = min(BAND_PAST, t)
            new = (k_tail.reshape(1, bsz, w_keep, H_C, HEAD_DIM), v_tail.reshape(1, bsz, w_keep, H_C, HEAD_DIM))
        else:
            c_k, c_v = caches
            w_len = c_k.shape[1]
            window = -(-(w_len + t) // LANES) * LANES
            cat = lambda c, new_: _pad_time(
                jnp.concatenate([c.reshape(bsz, w_len, D_MODEL).astype(BF16), r3(new_)], axis=1),
                0, window - w_len - t)
            o = _band(r3(q), cat(c_k, k), cat(c_v, v), cvec, tq=t, window=window, q_pos_base=past,
                      windowed=False)
            upd = lambda c, tail: jnp.concatenate(
                [c.reshape(bsz, w_len, D_MODEL), tail.reshape(bsz, t, D_MODEL)], axis=1)[:, t:].reshape(
                    1, bsz, w_len, H_C, HEAD_DIM)
            new = (upd(c_k, k_tail), upd(c_v, v_tail))
        x_out = _post([o.reshape(m, D_MODEL)], x, wo1, gvec(g_post_mix, 1), gvec(g_pre_ffn, 1),
                      wup1, wdn1, gvec(g_post_ffn, 1))
        return x_out, new

    xp, new_p_odd = odd_layer(xp, b_p, t_p, None)
    xs, new_s_odd = odd_layer(xs, b_s, t_s, (cache_c_k[0], cache_c_v[0]))

    return (xp.reshape(b_p, t_p, D_MODEL), xs.reshape(b_s, t_s, D_MODEL),
            *new_p_even, *new_p_odd, *new_s_even, *new_s_odd)
```

```python
import functools
import math

import numpy as np
import jax
import jax.numpy as jnp
from jax import lax
from jax.experimental import pallas as pl
from jax.experimental.pallas import tpu as pltpu

F32 = jnp.float32
BF16 = jnp.bfloat16

D_MODEL = 1024
HEAD_DIM = 64
CHUNK = 64
H_A = 4
H_B = 8
H_C = 16
A_W = 512
N_SEG = 6
BAND_PAST = 512
REL_CLIP = 256
D_FF = 4 * D_MODEL
NORM_EPS = 1e-6
NEG_INF = -1e30
QK_SCALE = HEAD_DIM ** -0.5

LANES = 128
ROW_TILE = 512
FF_CHUNK = 1024
VMEM_LIMIT = 56 * 1024 * 1024

_NT = (((1,), (1,)), ((), ()))


def _rms(x, g):
    ms = jnp.mean(x * x, axis=-1, keepdims=True)
    return x * lax.rsqrt(ms + NORM_EPS) * g


def _log_sigmoid(x):
    t = -x
    return -(jnp.maximum(t, 0.0) + jnp.log1p(jnp.exp(-jnp.abs(t))))


def _const_spec(shape):
    nd = len(shape)
    return pl.BlockSpec(shape, lambda *_: (0,) * nd, pipeline_mode=pl.Buffered(1))


def _params(n_axes):
    return pltpu.CompilerParams(dimension_semantics=("arbitrary",) * n_axes,
                                vmem_limit_bytes=VMEM_LIMIT)


def _proj_even_kernel(x_ref, g_ref, w_ref, wf_ref, wft_ref, bf_ref, bft_ref,
                      qa_ref, ka_ref, kab_ref, va_ref, vab_ref,
                      qb_ref, kb_ref, kbb_ref, vb_ref, vbb_ref, lf_ref, lft_ref):
    h = _rms(x_ref[...], g_ref[...]).astype(BF16)

    def seg(i):
        return jnp.dot(h, w_ref[:, i * A_W:(i + 1) * A_W], preferred_element_type=F32)

    qa_ref[...] = (seg(0) * QK_SCALE).astype(BF16)
    z = seg(1)
    ka_ref[...] = z
    kab_ref[...] = z.astype(BF16)
    z = seg(2)
    va_ref[...] = z
    vab_ref[...] = z.astype(BF16)
    qb_ref[...] = (seg(3) * QK_SCALE).astype(BF16)
    z = seg(4)
    kb_ref[...] = z
    kbb_ref[...] = z.astype(BF16)
    z = seg(5)
    vb_ref[...] = z
    vbb_ref[...] = z.astype(BF16)
    fz = jnp.dot(h, wf_ref[...], preferred_element_type=F32)
    lf_ref[...] = _log_sigmoid(fz + bf_ref[...])[:, :H_B]
    fzt = lax.dot_general(wft_ref[...], h, _NT, preferred_element_type=F32)
    lft_ref[0] = _log_sigmoid(fzt[:H_B] + bft_ref[...])


def _proj_even(x, g, w_main, wf, wft, bf_row, bf_col):
    m = x.shape[0]
    tm = min(ROW_TILE, m)
    n = m // tm
    row = lambda w: pl.BlockSpec((tm, w), lambda i: (i, 0))
    f32o = jax.ShapeDtypeStruct((m, A_W), F32)
    bfo = jax.ShapeDtypeStruct((m, A_W), BF16)
    out_shape = [bfo, f32o, bfo, f32o, bfo, bfo, f32o, bfo, f32o, bfo,
                 jax.ShapeDtypeStruct((m, H_B), F32),
                 jax.ShapeDtypeStruct((n, H_B, tm), F32)]
    out_specs = [row(A_W)] * 10 + [row(H_B), pl.BlockSpec((1, H_B, tm), lambda i: (i, 0, 0))]
    return pl.pallas_call(
        _proj_even_kernel,
        grid=(n,),
        in_specs=[row(D_MODEL), _const_spec((1, D_MODEL)), _const_spec(w_main.shape),
                  _const_spec(wf.shape), _const_spec(wft.shape),
                  _const_spec(bf_row.shape), _const_spec(bf_col.shape)],
        out_specs=out_specs,
        out_shape=out_shape,
        compiler_params=_params(1),
        name="proj_even",
    )(x, g, w_main, wf, wft, bf_row, bf_col)


def _cumsum_kernel(x_ref, o_ref):
    x = x_ref[0]
    t = x.shape[1]
    col = lax.broadcasted_iota(jnp.int32, x.shape, 1)
    s = 1
    while s < t:
        x = x + jnp.where(col >= s, pltpu.roll(x, s, 1), 0.0)
        s *= 2
    o_ref[0] = x


def _cumsum_time(x):
    b, h, t = x.shape
    spec = pl.BlockSpec((1, h, t), lambda i: (i, 0, 0))
    return pl.pallas_call(
        _cumsum_kernel, grid=(b,), in_specs=[spec], out_specs=spec,
        out_shape=jax.ShapeDtypeStruct(x.shape, F32),
        compiler_params=_params(1), name="cumsum_time",
    )(x)


def _stack_pair(q):
    q32 = q.astype(F32)
    lo = lax.broadcasted_iota(jnp.int32, q32.shape, 1) < HEAD_DIM
    return jnp.concatenate([jnp.where(lo, q32, 0.0), jnp.where(lo, 0.0, q32)], axis=0).astype(BF16)


def _online_update(a, u, c, vb, m_sc, l_sc, acc_sc):
    m_prev = m_sc[a]
    m_next = jnp.maximum(m_prev, jnp.max(u, axis=1, keepdims=True) + c)
    p = jnp.exp(u - jnp.tile(m_next - c, (1, u.shape[1] // LANES)))
    alpha = jnp.exp(m_prev - m_next)
    l_sc[a] = alpha * l_sc[a] + jnp.sum(p, axis=1, keepdims=True)
    acc_sc[a] = alpha * acc_sc[a] + jnp.dot(p.astype(BF16), vb, preferred_element_type=F32)
    m_sc[a] = m_next


def _flash_kernel(*refs, mode, tq, tk, q_pos0, diag_off, has_interior, lam_init):
    if mode == "diff":
        slopes_ref, lamv_ref, g_ref, q_ref, k_ref, v_ref, o_ref, m_sc, l_sc, acc_sc, bint_sc, bdiag_sc = refs
    else:
        cum_ref, q_ref, k_ref, v_ref, o_ref, m_sc, l_sc, acc_sc, bdiag_sc = refs
    pair = pl.program_id(0)
    q0 = q_pos0 + pl.program_id(2) * tq
    n_int = q0 // tk

    @pl.when((pl.program_id(1) == 0) & (pl.program_id(2) == 0))
    def _():
        row = lax.broadcasted_iota(jnp.int32, (tq, tk), 0)
        col = lax.broadcasted_iota(jnp.int32, (tq, tk), 1)
        if mode == "diff":
            neg_slope = -slopes_ref[pair]
            if has_interior:
                bint_sc[...] = neg_slope * (row - col).astype(F32)
            visible = (col >> 6) <= ((row + diag_off) >> 6)
            dist = jnp.abs(row + diag_off - col).astype(F32)
            bdiag_sc[...] = jnp.where(visible, neg_slope * dist, NEG_INF)
        else:
            bdiag_sc[...] = jnp.where(col <= row + diag_off, 0.0, NEG_INF)

    q2 = _stack_pair(q_ref[0])
    m_sc[...] = jnp.full(m_sc.shape, NEG_INF, F32)
    l_sc[...] = jnp.zeros(l_sc.shape, F32)
    acc_sc[...] = jnp.zeros(acc_sc.shape, F32)

    def scores(k0):
        kb = k_ref[0, pl.ds(k0, tk), :]
        vb = v_ref[0, pl.ds(k0, tk), :]
        return lax.dot_general(q2, kb, _NT, preferred_element_type=F32), vb

    def cum_row(a, k0):
        return cum_ref[0, 2 * pair + a, :, pl.ds(k0, tk)]

    def interior(j, carry):
        k0 = pl.multiple_of(j * tk, tk)
        s, vb = scores(k0)
        for a in range(2):
            sa = s[a * tq:(a + 1) * tq]
            if mode == "diff":
                _online_update(a, sa + bint_sc[...], -slopes_ref[pair] * (q0 - k0).astype(F32),
                               vb, m_sc, l_sc, acc_sc)
            else:
                _online_update(a, sa - cum_row(a, k0), 0.0, vb, m_sc, l_sc, acc_sc)
        return carry

    if has_interior:
        lax.fori_loop(0, n_int, interior, 0)

    k0 = pl.multiple_of(n_int * tk, tk)
    s, vb = scores(k0)
    for a in range(2):
        sa = s[a * tq:(a + 1) * tq]
        if mode == "diff":
            u = sa + bdiag_sc[...]
        else:
            u = (sa - cum_row(a, k0)) + bdiag_sc[...]
        _online_update(a, u, 0.0, vb, m_sc, l_sc, acc_sc)

    o0 = acc_sc[0] / l_sc[0]
    o1 = acc_sc[1] / l_sc[1]
    if mode == "diff":
        lv = lamv_ref[...]
        lam = (jnp.exp(jnp.sum(lv[0:1] * lv[1:2], axis=1, keepdims=True))
               - jnp.exp(jnp.sum(lv[2:3] * lv[3:4], axis=1, keepdims=True)) + lam_init)
        oa = o0 - lam * o1
        o_ref[0] = (_rms(oa, g_ref[...]) * (1.0 - lam_init)).astype(BF16)
    else:
        lo = lax.broadcasted_iota(jnp.int32, o0.shape, 1) < HEAD_DIM
        o_ref[0] = jnp.where(lo, o0, o1).astype(BF16)


def _flash(mode, q, k, v, extras, *, tq, tk, q_pos0, lam_init=0.0):
    b, t_q, width = q.shape
    t_k = k.shape[1]
    pairs = width // LANES
    n_q = t_q // tq
    has_interior = t_k > tk
    diag_off = q_pos0 % tk
    assert all((q_pos0 + i * tq) % tk == diag_off for i in range(n_q)) and diag_off + tq <= tk
    assert t_k % tk == 0 and (q_pos0 + (n_q - 1) * tq) // tk < t_k // tk
    grid = (pairs, b, n_q)
    qspec = pl.BlockSpec((1, tq, LANES), lambda p, bi, i: (bi, i, p))
    kvspec = pl.BlockSpec((1, t_k, LANES), lambda p, bi, i: (bi, 0, p))
    bias_scratch = [pltpu.VMEM((tq, tk), F32)]
    if mode == "diff":
        slopes, lamv, g = extras
        extra_specs = [pl.BlockSpec(memory_space=pltpu.SMEM),
                       pl.BlockSpec(lamv.shape, lambda p, bi, i: (0, 0)),
                       pl.BlockSpec(g.shape, lambda p, bi, i: (0, 0))]
        bias_scratch = [pltpu.VMEM((tq, tk) if has_interior else (8, LANES), F32)] + bias_scratch
    else:
        extras = (extras[0][:, :, None, :],)
        extra_specs = [pl.BlockSpec((1,) + extras[0].shape[1:], lambda p, bi, i: (bi, 0, 0, 0))]
    return pl.pallas_call(
        functools.partial(_flash_kernel, mode=mode, tq=tq, tk=tk, q_pos0=q_pos0, diag_off=diag_off,
                          has_interior=has_interior, lam_init=lam_init),
        grid=grid,
        in_specs=extra_specs + [qspec, kvspec, kvspec],
        out_specs=qspec,
        out_shape=jax.ShapeDtypeStruct(q.shape, BF16),
        scratch_shapes=[pltpu.VMEM((2, tq, LANES), F32)] * 3 + bias_scratch,
        compiler_params=_params(3),
        name="flash_" + mode,
    )(*extras, q, k, v)


def _band_kernel(c_ref, q_ref, k_ref, v_ref, o_ref, bias_sc, *, tq, window, q_pos_base, windowed, n_edge):
    qi = pl.program_id(2)

    @pl.when((pl.program_id(1) == 0) & (qi == 0))
    def _():
        row = lax.broadcasted_iota(jnp.int32, (tq, window), 0)
        col = lax.broadcasted_iota(jnp.int32, (tq, window), 1)
        visible = ((col >> 6) >= (row >> 6)) & ((col >> 6) <= (row >> 6) + BAND_PAST // CHUNK)
        for a in range(2):
            c = jnp.broadcast_to(c_ref[0, a:a + 1, :], (tq, c_ref.shape[2]))
            bias = pltpu.roll(c, 0, 1, stride=1, stride_axis=0)[:, :window]
            bias_sc[a] = jnp.where(visible, bias, NEG_INF)

    def attend(edge):
        w0 = pl.multiple_of(qi * tq, tq) if windowed else 0
        kb = k_ref[0, pl.ds(w0, window), :]
        vb = v_ref[0, pl.ds(w0, window), :]
        s = lax.dot_general(_stack_pair(q_ref[0]), kb, _NT, preferred_element_type=F32)
        outs = []
        for a in range(2):
            t = s[a * tq:(a + 1) * tq] + bias_sc[a]
            if edge:
                k_pos = q_pos_base + qi * tq - BAND_PAST + lax.broadcasted_iota(jnp.int32, (1, window), 1)
                t = jnp.where(k_pos >= 0, t, NEG_INF)
            p = jnp.exp(t - jnp.max(t, axis=1, keepdims=True))
            l = jnp.sum(p, axis=1, keepdims=True)
            outs.append(jnp.dot(p.astype(BF16), vb, preferred_element_type=F32) / l)
        lo = lax.broadcasted_iota(jnp.int32, outs[0].shape, 1) < HEAD_DIM
        o_ref[0] = jnp.where(lo, outs[0], outs[1]).astype(BF16)

    if n_edge:
        pl.when(qi < n_edge)(lambda: attend(True))
        pl.when(qi >= n_edge)(lambda: attend(False))
    else:
        attend(False)


def _band(q, k, v, cvec, *, tq, window, q_pos_base, windowed):
    b, t_q, width = q.shape
    t_k = k.shape[1]
    pairs = width // LANES
    n_edge = max(0, -(-(BAND_PAST - q_pos_base) // tq))
    qspec = pl.BlockSpec((1, tq, LANES), lambda p, bi, i: (bi, i, p))
    kvspec = pl.BlockSpec((1, t_k, LANES), lambda p, bi, i: (bi, 0, p))
    cspec = pl.BlockSpec((1, 2, cvec.shape[2]), lambda p, bi, i: (p, 0, 0))
    return pl.pallas_call(
        functools.partial(_band_kernel, tq=tq, window=window, q_pos_base=q_pos_base, windowed=windowed,
                          n_edge=n_edge),
        grid=(pairs, b, t_q // tq),
        in_specs=[cspec, qspec, kvspec, kvspec],
        out_specs=qspec,
        out_shape=jax.ShapeDtypeStruct(q.shape, BF16),
        scratch_shapes=[pltpu.VMEM((2, tq, window), F32)],
        compiler_params=_params(3),
        name="band_attn",
    )(cvec, q, k, v)


def _proj_odd_kernel(x_ref, g_ref, w_ref, q_ref, k_ref, v_ref, kt_ref, vt_ref, *, tail_every):
    h = _rms(x_ref[...], g_ref[...]).astype(BF16)

    def seg(i):
        return jnp.dot(h, w_ref[:, i * D_MODEL:(i + 1) * D_MODEL], preferred_element_type=F32)

    q_ref[...] = (seg(0) * QK_SCALE).astype(BF16)
    kz = seg(1)
    k_ref[...] = kz.astype(BF16)
    vz = seg(2)
    v_ref[...] = vz.astype(BF16)

    @pl.when(pl.program_id(0) % tail_every == tail_every - 1)
    def _():
        kt_ref[...] = kz
        vt_ref[...] = vz


def _proj_odd(x, g, w, *, tail_every):
    m = x.shape[0]
    tm = min(ROW_TILE, m)
    n = m // tm
    row = pl.BlockSpec((tm, D_MODEL), lambda i: (i, 0))
    tail = pl.BlockSpec((tm, D_MODEL), lambda i: (i // tail_every, 0))
    bfo = jax.ShapeDtypeStruct((m, D_MODEL), BF16)
    tailo = jax.ShapeDtypeStruct((n // tail_every * tm, D_MODEL), F32)
    return pl.pallas_call(
        functools.partial(_proj_odd_kernel, tail_every=tail_every),
        grid=(n,),
        in_specs=[row, _const_spec((1, D_MODEL)), _const_spec(w.shape)],
        out_specs=[row, row, row, tail, tail],
        out_shape=[bfo, bfo, bfo, tailo, tailo],
        compiler_params=_params(1),
        name="proj_odd",
    )(x, g, w)


def _post_kernel(*refs, n_o):
    o_refs = refs[:n_o]
    x_ref, wo_ref, gpm_ref, gpre_ref, wup_ref, wdn_ref, gpf_ref, out_ref = refs[n_o:]
    o = o_refs[0][...] if n_o == 1 else jnp.concatenate([r[...] for r in o_refs], axis=1)
    mixed = jnp.dot(o, wo_ref[...], preferred_element_type=F32)
    x1 = x_ref[...] + _rms(mixed, gpm_ref[...])
    h = _rms(x1, gpre_ref[...]).astype(BF16)
    acc = jnp.zeros(x1.shape, F32)
    for c in range(D_FF // FF_CHUNK):
        u = jnp.dot(h, wup_ref[:, c * FF_CHUNK:(c + 1) * FF_CHUNK], preferred_element_type=F32)
        u = jnp.maximum(u, 0.0)
        acc = acc + jnp.dot((u * u).astype(BF16), wdn_ref[c * FF_CHUNK:(c + 1) * FF_CHUNK, :],
                            preferred_element_type=F32)
    out_ref[...] = x1 + _rms(acc, gpf_ref[...])


def _post(o_list, x, wo, gpm, gpre, wup, wdn, gpf):
    m = x.shape[0]
    tm = min(ROW_TILE, m)
    row = lambda w: pl.BlockSpec((tm, w), lambda i: (i, 0))
    gspec = _const_spec((1, D_MODEL))
    return pl.pallas_call(
        functools.partial(_post_kernel, n_o=len(o_list)),
        grid=(m // tm,),
        in_specs=[row(o.shape[1]) for o in o_list]
                 + [row(D_MODEL), _const_spec(wo.shape), gspec, gspec,
                    _const_spec(wup.shape), _const_spec(wdn.shape), gspec],
        out_specs=row(D_MODEL),
        out_shape=jax.ShapeDtypeStruct((m, D_MODEL), F32),
        compiler_params=_params(1),
        name="post_mix_mlp",
    )(*o_list, x, wo, gpm, gpre, wup, wdn, gpf)


def _band_bias_vectors(table):
    n = 2 * BAND_PAST
    m = np.arange(n)
    d = np.where(m <= n - REL_CLIP, -m, n - m)
    idx = np.clip(d + BAND_PAST, -REL_CLIP, REL_CLIP) + REL_CLIP
    return jnp.take(table.astype(F32), jnp.asarray(idx, jnp.int32), axis=1).reshape(H_C // 2, 2, n)


def _pad_time(x, front, back):
    return jnp.pad(x, ((0, 0), (front, back), (0, 0)))


def kernel(x_prompt, x_sample, cache_a_k, cache_a_v, cache_b_k, cache_b_v, cache_b_logf, cache_c_k, cache_c_v,
           w_in_even, b_forget, lam_q1, lam_k1, lam_q2, lam_k2, subln_g, w_out_even, w_in_odd, rel_bias,
           w_out_odd, g_pre_mix, g_post_mix, g_pre_ffn, g_post_ffn, w_ffn_up, w_ffn_down):
    b_p, t_p, _ = x_prompt.shape
    b_s, t_s, _ = x_sample.shape
    past = cache_b_logf.shape[2]
    m_p, m_s = b_p * t_p, b_s * t_s
    xp = x_prompt.reshape(m_p, D_MODEL)
    xs = x_sample.reshape(m_s, D_MODEL)
    gvec = lambda a, l: a[l].reshape(1, D_MODEL)

    lam_init = 0.8 - 0.6 * math.exp(-0.3 * 0)
    w_in = w_in_even[0]
    w_main = w_in[:, :N_SEG * A_W].astype(BF16)
    w_f = w_in[:, N_SEG * A_W:]
    wf = jnp.pad(w_f, ((0, 0), (0, LANES - H_B))).astype(BF16)
    wft = jnp.pad(w_f.T, ((0, 16 - H_B), (0, 0))).astype(BF16)
    bf_row = jnp.pad(b_forget[0], (0, LANES - H_B)).reshape(1, LANES)
    bf_col = b_forget[0].reshape(H_B, 1)
    slopes = 2.0 ** (-8.0 * jnp.arange(1, H_A + 1, dtype=F32) / H_A)
    lamv = jnp.stack([lam_q1[0], lam_k1[0], lam_q2[0], lam_k2[0]]).astype(F32)
    sub_g = subln_g[0].reshape(1, 2 * HEAD_DIM)
    wo0 = w_out_even[0].astype(BF16)
    wup0, wdn0 = w_ffn_up[0].astype(BF16), w_ffn_down[0].astype(BF16)

    def even_layer(x, bsz, t, caches):
        (qa, ka, kab, va, vab, qb, kb, kbb, vb, vbb, lf, lft) = _proj_even(
            x, gvec(g_pre_mix, 0), w_main, wf, wft, bf_row, bf_col)
        m = x.shape[0]
        tm = lft.shape[2]
        r3 = lambda a: a.reshape(bsz, t, A_W)
        lft = lft.transpose(1, 0, 2).reshape(H_B, bsz, t).transpose(1, 0, 2)
        if caches is None:
            kab3, vab3, kbb3, vbb3 = r3(kab), r3(vab), r3(kbb), r3(vbb)
            cum = _cumsum_time(lft)
            tq = tk = 512
            q_pos0 = 0
        else:
            c_ka, c_va, c_kb, c_vb, c_lf = caches
            p_len = c_ka.shape[1]
            t_all = p_len + t
            t_pad = -(-t_all // LANES) * LANES
            cat = lambda c, new: _pad_time(
                jnp.concatenate([c.reshape(bsz, p_len, A_W).astype(BF16), r3(new)], axis=1), 0, t_pad - t_all)
            kab3, vab3, kbb3, vbb3 = cat(c_ka, kab), cat(c_va, vab), cat(c_kb, kbb), cat(c_vb, vbb)
            lf_all = jnp.concatenate([c_lf.astype(F32).transpose(0, 2, 1), lft], axis=2)
            cum = _cumsum_time(jnp.pad(lf_all, ((0, 0), (0, 0), (0, t_pad - t_all))))
            tq, tk = t, t_pad
            q_pos0 = p_len
        oa = _flash("diff", r3(qa), kab3, vab3, (slopes, lamv, sub_g), tq=tq, tk=tk, q_pos0=q_pos0,
                    lam_init=lam_init)
        ob = _flash("fox", r3(qb), kbb3, vbb3, (cum,), tq=tq, tk=tk, q_pos0=q_pos0)
        x_out = _post([oa.reshape(m, A_W), ob.reshape(m, A_W)], x, wo0, gvec(g_post_mix, 0),
                      gvec(g_pre_ffn, 0), wup0, wdn0, gvec(g_post_ffn, 0))
        new = (ka.reshape(1, bsz, t, H_A, 2 * HEAD_DIM), va.reshape(1, bsz, t, H_A, 2 * HEAD_DIM),
               kb.reshape(1, bsz, t, H_B, HEAD_DIM), vb.reshape(1, bsz, t, H_B, HEAD_DIM),
               lf.reshape(1, bsz, t, H_B))
        return x_out, new

    xp, new_p_even = even_layer(xp, b_p, t_p, None)
    xs, new_s_even = even_layer(xs, b_s, t_s, (cache_a_k[0], cache_a_v[0], cache_b_k[0], cache_b_v[0],
                                              cache_b_logf[0]))

    w_odd = w_in_odd[0].astype(BF16)
    cvec = _band_bias_vectors(rel_bias[0])
    wo1 = w_out_odd[0].astype(BF16)
    wup1, wdn1 = w_ffn_up[1].astype(BF16), w_ffn_down[1].astype(BF16)

    def odd_layer(x, bsz, t, caches):
        m = x.shape[0]
        tm = min(ROW_TILE, m)
        tail_every = (t // tm) if caches is None else 1
        q, k, v, k_tail, v_tail = _proj_odd(x, gvec(g_pre_mix, 1), w_odd, tail_every=tail_every)
        r3 = lambda a: a.reshape(bsz, t, D_MODEL)
        if caches is None:
            kp, vp = _pad_time(r3(k), BAND_PAST, 0), _pad_time(r3(v), BAND_PAST, 0)
            tq = 256
            o = _band(r3(q), kp, vp, cvec, tq=tq, window=BAND_PAST + tq, q_pos_base=0, windowed=True)
            w_keep = min(BAND_PAST, t)
            new = (k_tail.reshape(1, bsz, w_keep, H_C, HEAD_DIM), v_tail.reshape(1, bsz, w_keep, H_C, HEAD_DIM))
        else:
            c_k, c_v = caches
            w_len = c_k.shape[1]
            window = -(-(w_len + t) // LANES) * LANES
            cat = lambda c, new_: _pad_time(
                jnp.concatenate([c.reshape(bsz, w_len, D_MODEL).astype(BF16), r3(new_)], axis=1),
                0, window - w_len - t)
            o = _band(r3(q), cat(c_k, k), cat(c_v, v), cvec, tq=t, window=window, q_pos_base=past,
                      windowed=False)
            upd = lambda c, tail: jnp.concatenate(
                [c.reshape(bsz, w_len, D_MODEL), tail.reshape(bsz, t, D_MODEL)], axis=1)[:, t:].reshape(
                    1, bsz, w_len, H_C, HEAD_DIM)
            new = (upd(c_k, k_tail), upd(c_v, v_tail))
        x_out = _post([o.reshape(m, D_MODEL)], x, wo1, gvec(g_post_mix, 1), gvec(g_pre_ffn, 1),
                      wup1, wdn1, gvec(g_post_ffn, 1))
        return x_out, new

    xp, new_p_odd = odd_layer(xp, b_p, t_p, None)
    xs, new_s_odd = odd_layer(xs, b_s, t_s, (cache_c_k[0], cache_c_v[0]))

    return (xp.reshape(b_p, t_p, D_MODEL), xs.reshape(b_s, t_s, D_MODEL),
            *new_p_even, *new_p_odd, *new_s_even, *new_s_odd)
```

```python
import functools
import math

import numpy as np
import jax
import jax.numpy as jnp
from jax import lax
from jax.experimental import pallas as pl
from jax.experimental.pallas import tpu as pltpu

F32 = jnp.float32
BF16 = jnp.bfloat16

D_MODEL = 1024
HEAD_DIM = 64
CHUNK = 64
H_A = 4
H_B = 8
H_C = 16
A_W = 512
N_SEG = 6
BAND_PAST = 512
REL_CLIP = 256
D_FF = 4 * D_MODEL
NORM_EPS = 1e-6
NEG_INF = -1e30
QK_SCALE = HEAD_DIM ** -0.5

LANES = 128
ROW_TILE = 512
FF_CHUNK = 1024
VMEM_LIMIT = 56 * 1024 * 1024

_NT = (((1,), (1,)), ((), ()))


def _rms(x, g):
    ms = jnp.mean(x * x, axis=-1, keepdims=True)
    return x * lax.rsqrt(ms + NORM_EPS) * g


def _log_sigmoid(x):
    t = -x
    return -(jnp.maximum(t, 0.0) + jnp.log1p(jnp.exp(-jnp.abs(t))))


def _const_spec(shape):
    nd = len(shape)
    return pl.BlockSpec(shape, lambda *_: (0,) * nd, pipeline_mode=pl.Buffered(1))


def _params(n_axes):
    return pltpu.CompilerParams(dimension_semantics=("arbitrary",) * n_axes,
                                vmem_limit_bytes=VMEM_LIMIT)


def _proj_even_kernel(x_ref, g_ref, w_ref, wvt_ref, wf_ref, wft_ref, bf_ref, bft_ref,
                      qa_ref, ka_ref, kab_ref, va_ref, vab_ref,
                      qb_ref, kb_ref, kbb_ref, vb_ref, vbb_ref, lf_ref, lft_ref, *, v_feature_major):
    h = _rms(x_ref[...], g_ref[...]).astype(BF16)

    def seg(i):
        return jnp.dot(h, w_ref[:, i * A_W:(i + 1) * A_W], preferred_element_type=F32)

    def seg_t(i):
        return lax.dot_general(wvt_ref[i], h, _NT, preferred_element_type=F32)

    qa_ref[...] = (seg(0) * QK_SCALE).astype(BF16)
    z = seg(1)
    ka_ref[...] = z
    kab_ref[...] = z.astype(BF16)
    z = seg(2)
    va_ref[...] = z
    if v_feature_major:
        vab_ref[0] = seg_t(0).astype(BF16)
    else:
        vab_ref[...] = z.astype(BF16)
    qb_ref[...] = (seg(3) * QK_SCALE).astype(BF16)
    z = seg(4)
    kb_ref[...] = z
    kbb_ref[...] = z.astype(BF16)
    z = seg(5)
    vb_ref[...] = z
    if v_feature_major:
        vbb_ref[0] = seg_t(1).astype(BF16)
    else:
        vbb_ref[...] = z.astype(BF16)
    fz = jnp.dot(h, wf_ref[...], preferred_element_type=F32)
    lf_ref[...] = _log_sigmoid(fz + bf_ref[...])[:, :H_B]
    fzt = lax.dot_general(wft_ref[...], h, _NT, preferred_element_type=F32)
    lft_ref[0] = _log_sigmoid(fzt[:H_B] + bft_ref[...])


def _proj_even(x, g, w_main, w_vt, wf, wft, bf_row, bf_col, *, tiles_per_batch=None):
    m = x.shape[0]
    tm = min(ROW_TILE, m)
    n = m // tm
    row = lambda w: pl.BlockSpec((tm, w), lambda i: (i, 0))
    f32o = jax.ShapeDtypeStruct((m, A_W), F32)
    bfo = jax.ShapeDtypeStruct((m, A_W), BF16)
    if tiles_per_batch:
        tpb = tiles_per_batch
        vo = jax.ShapeDtypeStruct((n // tpb, A_W, tpb * tm), BF16)
        vspec = pl.BlockSpec((1, A_W, tm), lambda i: (i // tpb, 0, i % tpb))
    else:
        vo, vspec = bfo, row(A_W)
    out_shape = [bfo, f32o, bfo, f32o, vo, bfo, f32o, bfo, f32o, vo,
                 jax.ShapeDtypeStruct((m, H_B), F32),
                 jax.ShapeDtypeStruct((n, H_B, tm), F32)]
    out_specs = ([row(A_W)] * 4 + [vspec] + [row(A_W)] * 4 + [vspec]
                 + [row(H_B), pl.BlockSpec((1, H_B, tm), lambda i: (i, 0, 0))])
    return pl.pallas_call(
        functools.partial(_proj_even_kernel, v_feature_major=bool(tiles_per_batch)),
        grid=(n,),
        in_specs=[row(D_MODEL), _const_spec((1, D_MODEL)), _const_spec(w_main.shape),
                  _const_spec(w_vt.shape), _const_spec(wf.shape), _const_spec(wft.shape),
                  _const_spec(bf_row.shape), _const_spec(bf_col.shape)],
        out_specs=out_specs,
        out_shape=out_shape,
        compiler_params=_params(1),
        name="proj_even",
    )(x, g, w_main, w_vt, wf, wft, bf_row, bf_col)


def _cumsum_kernel(x_ref, o_ref):
    x = x_ref[0]
    t = x.shape[1]
    col = lax.broadcasted_iota(jnp.int32, x.shape, 1)
    s = 1
    while s < t:
        x = x + jnp.where(col >= s, pltpu.roll(x, s, 1), 0.0)
        s *= 2
    o_ref[0] = x


def _cumsum_time(x):
    b, h, t = x.shape
    spec = pl.BlockSpec((1, h, t), lambda i: (i, 0, 0))
    return pl.pallas_call(
        _cumsum_kernel, grid=(b,), in_specs=[spec], out_specs=spec,
        out_shape=jax.ShapeDtypeStruct(x.shape, F32),
        compiler_params=_params(1), name="cumsum_time",
    )(x)


def _stack_pair(q):
    q32 = q.astype(F32)
    lo = lax.broadcasted_iota(jnp.int32, q32.shape, 1) < HEAD_DIM
    return jnp.concatenate([jnp.where(lo, q32, 0.0), jnp.where(lo, 0.0, q32)], axis=0).astype(BF16)


def _online_update(a, u, c, vb, m_sc, l_sc, acc_sc):
    m_prev = m_sc[a]
    m_next = jnp.maximum(m_prev, jnp.max(u, axis=1, keepdims=True) + c)
    p = jnp.exp(u - jnp.tile(m_next - c, (1, u.shape[1] // LANES)))
    alpha = jnp.exp(m_prev - m_next)
    l_sc[a] = alpha * l_sc[a] + jnp.sum(p, axis=1, keepdims=True)
    acc_sc[a] = alpha * acc_sc[a] + jnp.dot(p.astype(BF16), vb, preferred_element_type=F32)
    m_sc[a] = m_next


def _flash_kernel(*refs, mode, tq, tk, q_pos0, diag_off, has_interior, lam_init):
    if mode == "diff":
        slopes_ref, lamv_ref, g_ref, q_ref, k_ref, v_ref, o_ref, m_sc, l_sc, acc_sc, bint_sc, bdiag_sc = refs
    else:
        cum_ref, q_ref, k_ref, v_ref, o_ref, m_sc, l_sc, acc_sc, bdiag_sc = refs
    pair = pl.program_id(0)
    q0 = q_pos0 + pl.program_id(2) * tq
    n_int = q0 // tk

    @pl.when((pl.program_id(1) == 0) & (pl.program_id(2) == 0))
    def _():
        row = lax.broadcasted_iota(jnp.int32, (tq, tk), 0)
        col = lax.broadcasted_iota(jnp.int32, (tq, tk), 1)
        if mode == "diff":
            neg_slope = -slopes_ref[pair]
            if has_interior:
                bint_sc[...] = neg_slope * (row - col).astype(F32)
            visible = (col >> 6) <= ((row + diag_off) >> 6)
            dist = jnp.abs(row + diag_off - col).astype(F32)
            bdiag_sc[...] = jnp.where(visible, neg_slope * dist, NEG_INF)
        else:
            bdiag_sc[...] = jnp.where(col <= row + diag_off, 0.0, NEG_INF)

    q2 = _stack_pair(q_ref[0])
    m_sc[...] = jnp.full(m_sc.shape, NEG_INF, F32)
    l_sc[...] = jnp.zeros(l_sc.shape, F32)
    acc_sc[...] = jnp.zeros(acc_sc.shape, F32)

    def scores(k0):
        kb = k_ref[0, pl.ds(k0, tk), :]
        vb = v_ref[0, pl.ds(k0, tk), :]
        return lax.dot_general(q2, kb, _NT, preferred_element_type=F32), vb

    def cum_row(a, k0):
        return cum_ref[0, 2 * pair + a, :, pl.ds(k0, tk)]

    def interior(j, carry):
        k0 = pl.multiple_of(j * tk, tk)
        s, vb = scores(k0)
        for a in range(2):
            sa = s[a * tq:(a + 1) * tq]
            if mode == "diff":
                _online_update(a, sa + bint_sc[...], -slopes_ref[pair] * (q0 - k0).astype(F32),
                               vb, m_sc, l_sc, acc_sc)
            else:
                _online_update(a, sa - cum_row(a, k0), 0.0, vb, m_sc, l_sc, acc_sc)
        return carry

    if has_interior:
        lax.fori_loop(0, n_int, interior, 0)

    k0 = pl.multiple_of(n_int * tk, tk)
    s, vb = scores(k0)
    for a in range(2):
        sa = s[a * tq:(a + 1) * tq]
        if mode == "diff":
            u = sa + bdiag_sc[...]
        else:
            u = (sa - cum_row(a, k0)) + bdiag_sc[...]
        _online_update(a, u, 0.0, vb, m_sc, l_sc, acc_sc)

    o0 = acc_sc[0] / l_sc[0]
    o1 = acc_sc[1] / l_sc[1]
    if mode == "diff":
        lv = lamv_ref[...]
        lam = (jnp.exp(jnp.sum(lv[0:1] * lv[1:2], axis=1, keepdims=True))
               - jnp.exp(jnp.sum(lv[2:3] * lv[3:4], axis=1, keepdims=True)) + lam_init)
        oa = o0 - lam * o1
        o_ref[0] = (_rms(oa, g_ref[...]) * (1.0 - lam_init)).astype(BF16)
    else:
        lo = lax.broadcasted_iota(jnp.int32, o0.shape, 1) < HEAD_DIM
        o_ref[0] = jnp.where(lo, o0, o1).astype(BF16)


def _flash(mode, q, k, v, extras, *, tq, tk, q_pos0, lam_init=0.0):
    b, t_q, width = q.shape
    t_k = k.shape[1]
    pairs = width // LANES
    n_q = t_q // tq
    has_interior = t_k > tk
    diag_off = q_pos0 % tk
    assert all((q_pos0 + i * tq) % tk == diag_off for i in range(n_q)) and diag_off + tq <= tk
    assert t_k % tk == 0 and (q_pos0 + (n_q - 1) * tq) // tk < t_k // tk
    grid = (pairs, b, n_q)
    qspec = pl.BlockSpec((1, tq, LANES), lambda p, bi, i: (bi, i, p))
    kvspec = pl.BlockSpec((1, t_k, LANES), lambda p, bi, i: (bi, 0, p))
    bias_scratch = [pltpu.VMEM((tq, tk), F32)]
    if mode == "diff":
        slopes, lamv, g = extras
        extra_specs = [pl.BlockSpec(memory_space=pltpu.SMEM),
                       pl.BlockSpec(lamv.shape, lambda p, bi, i: (0, 0)),
                       pl.BlockSpec(g.shape, lambda p, bi, i: (0, 0))]
        bias_scratch = [pltpu.VMEM((tq, tk) if has_interior else (8, LANES), F32)] + bias_scratch
    else:
        extras = (extras[0][:, :, None, :],)
        extra_specs = [pl.BlockSpec((1,) + extras[0].shape[1:], lambda p, bi, i: (bi, 0, 0, 0))]
    return pl.pallas_call(
        functools.partial(_flash_kernel, mode=mode, tq=tq, tk=tk, q_pos0=q_pos0, diag_off=diag_off,
                          has_interior=has_interior, lam_init=lam_init),
        grid=grid,
        in_specs=extra_specs + [qspec, kvspec, kvspec],
        out_specs=qspec,
        out_shape=jax.ShapeDtypeStruct(q.shape, BF16),
        scratch_shapes=[pltpu.VMEM((2, tq, LANES), F32)] * 3 + bias_scratch,
        compiler_params=_params(3),
        name="flash_" + mode,
    )(*extras, q, k, v)


ONES_ROWS = 16


def _with_ones(vt):
    return jnp.concatenate([vt, jnp.ones((ONES_ROWS, vt.shape[1]), BF16)], axis=0)


def _online_update_t(a, u, c, vta, m_sc, acc_sc):
    m_prev = m_sc[a]
    m_next = jnp.maximum(m_prev, jnp.max(u, axis=0, keepdims=True) + c)
    p = jnp.exp(u - (m_next - c))
    alpha = jnp.exp(m_prev - m_next)
    acc_sc[a] = alpha * acc_sc[a] + jnp.dot(vta, p.astype(BF16), preferred_element_type=F32)
    m_sc[a] = m_next


def _flash_t_kernel(*refs, mode, tq, lam_init):
    tk = tq
    if mode == "diff":
        (slopes_ref, lamv_ref, g_ref, q_ref, k_ref, vt_ref, o_ref,
         m_sc, acc_sc, q2_sc, s0_sc, s1_sc, bint_sc, bdiag_sc) = refs
    else:
        cum_ref, q_ref, k_ref, vt_ref, o_ref, m_sc, acc_sc, q2_sc, s0_sc, s1_sc, bdiag_sc, ck_sc = refs
    pair = pl.program_id(0)
    qi = pl.program_id(2)

    @pl.when((pl.program_id(1) == 0) & (qi == 0))
    def _():
        row = lax.broadcasted_iota(jnp.int32, (tk, tq), 0)
        col = lax.broadcasted_iota(jnp.int32, (tk, tq), 1)
        if mode == "diff":
            neg_slope = -slopes_ref[pair]
            bint_sc[...] = neg_slope * (col - row).astype(F32)
            visible = (row >> 6) <= (col >> 6)
            bdiag_sc[...] = jnp.where(visible, neg_slope * jnp.abs(col - row).astype(F32), NEG_INF)
        else:
            bdiag_sc[...] = jnp.where(row <= col, 0.0, NEG_INF)

    if mode == "fox":
        @pl.when(qi == 0)
        def _():
            for a in range(2):
                for c0 in range(0, ck_sc.shape[1], tk):
                    r = cum_ref[0, 2 * pair + a, :, c0:c0 + tk]
                    ck_sc[a, c0:c0 + tk, :] = jnp.broadcast_to(r, (LANES, tk)).T

    q2_sc[...] = _stack_pair(q_ref[0])
    m_sc[...] = jnp.full(m_sc.shape, NEG_INF, F32)
    acc_sc[...] = jnp.zeros(acc_sc.shape, F32)

    def scores(k0, s_ref):
        s_ref[...] = lax.dot_general(k_ref[0, pl.ds(k0, tk), :], q2_sc[...], _NT,
                                     preferred_element_type=F32)

    def consume(k0, s_ref, diag):
        vta = _with_ones(vt_ref[0, :, pl.ds(k0, tk)])
        for a in range(2):
            sa = s_ref[:, a * tq:(a + 1) * tq]
            if mode == "diff":
                if diag:
                    u, c = sa + bdiag_sc[...], 0.0
                else:
                    u, c = sa + bint_sc[...], -slopes_ref[pair] * (qi * tq - k0).astype(F32)
            else:
                u = sa - jnp.tile(ck_sc[a, pl.ds(k0, tk), :], (1, tq // LANES))
                if diag:
                    u = u + bdiag_sc[...]
                c = 0.0
            _online_update_t(a, u, c, vta, m_sc, acc_sc)

    blk = lambda j: pl.multiple_of(j * tk, tk)
    scores(0, s0_sc)

    def two_blocks(i, carry):
        scores(blk(2 * i + 1), s1_sc)
        consume(blk(2 * i), s0_sc, False)
        scores(blk(2 * i + 2), s0_sc)
        consume(blk(2 * i + 1), s1_sc, False)
        return carry

    lax.fori_loop(0, qi // 2, two_blocks, 0)

    @pl.when(qi % 2 == 1)
    def _():
        scores(blk(qi), s1_sc)
        consume(blk(qi - 1), s0_sc, False)
        consume(blk(qi), s1_sc, True)

    @pl.when(qi % 2 == 0)
    def _():
        consume(blk(qi), s0_sc, True)

    d = 2 * HEAD_DIM
    o0 = acc_sc[0, :d, :] / acc_sc[0, d:d + 1, :]
    o1 = acc_sc[1, :d, :] / acc_sc[1, d:d + 1, :]
    if mode == "diff":
        lv = lamv_ref[...]
        lam = (jnp.exp(jnp.sum(lv[0:1] * lv[1:2], axis=1, keepdims=True))
               - jnp.exp(jnp.sum(lv[2:3] * lv[3:4], axis=1, keepdims=True)) + lam_init)
        oa = o0 - lam * o1
        ms = jnp.mean(oa * oa, axis=0, keepdims=True)
        y = (oa * lax.rsqrt(ms + NORM_EPS)).T * g_ref[...]
        o_ref[0] = (y * (1.0 - lam_init)).astype(BF16)
    else:
        lo = lax.broadcasted_iota(jnp.int32, o0.shape, 0) < HEAD_DIM
        o_ref[0] = jnp.where(lo, o0, o1).T.astype(BF16)


def _flash_t(mode, q, k, vt, extras, *, tq, lam_init=0.0):
    b, t, width = q.shape
    pairs = width // LANES
    qspec = pl.BlockSpec((1, tq, LANES), lambda p, bi, i: (bi, i, p))
    kspec = pl.BlockSpec((1, t, LANES), lambda p, bi, i: (bi, 0, p))
    vspec = pl.BlockSpec((1, LANES, t), lambda p, bi, i: (bi, p, 0))
    scratch = [pltpu.VMEM((2, 1, tq), F32), pltpu.VMEM((2, LANES + ONES_ROWS, tq), F32),
               pltpu.VMEM((2 * tq, LANES), BF16), pltpu.VMEM((tq, 2 * tq), F32), pltpu.VMEM((tq, 2 * tq), F32)]
    if mode == "diff":
        slopes, lamv, g = extras
        extra_specs = [pl.BlockSpec(memory_space=pltpu.SMEM),
                       pl.BlockSpec(lamv.shape, lambda p, bi, i: (0, 0)),
                       pl.BlockSpec(g.shape, lambda p, bi, i: (0, 0))]
        scratch += [pltpu.VMEM((tq, tq), F32)] * 2
    else:
        extras = (extras[0][:, :, None, :],)
        extra_specs = [pl.BlockSpec((1,) + extras[0].shape[1:], lambda p, bi, i: (bi, 0, 0, 0))]
        scratch += [pltpu.VMEM((tq, tq), F32), pltpu.VMEM((2, t, LANES), F32)]
    return pl.pallas_call(
        functools.partial(_flash_t_kernel, mode=mode, tq=tq, lam_init=lam_init),
        grid=(pairs, b, t // tq),
        in_specs=extra_specs + [qspec, kspec, vspec],
        out_specs=qspec,
        out_shape=jax.ShapeDtypeStruct(q.shape, BF16),
        scratch_shapes=scratch,
        compiler_params=_params(3),
        name="flash_t_" + mode,
    )(*extras, q, k, vt)


def _band_kernel(c_ref, q_ref, k_ref, v_ref, o_ref, bias_sc, *, tq, window, q_pos_base, windowed, n_edge):
    qi = pl.program_id(2)

    @pl.when((pl.program_id(1) == 0) & (qi == 0))
    def _():
        row = lax.broadcasted_iota(jnp.int32, (tq, window), 0)
        col = lax.broadcasted_iota(jnp.int32, (tq, window), 1)
        visible = ((col >> 6) >= (row >> 6)) & ((col >> 6) <= (row >> 6) + BAND_PAST // CHUNK)
        for a in range(2):
            c = jnp.broadcast_to(c_ref[0, a:a + 1, :], (tq, c_ref.shape[2]))
            bias = pltpu.roll(c, 0, 1, stride=1, stride_axis=0)[:, :window]
            bias_sc[a] = jnp.where(visible, bias, NEG_INF)

    def attend(edge):
        w0 = pl.multiple_of(qi * tq, tq) if windowed else 0
        kb = k_ref[0, pl.ds(w0, window), :]
        vb = v_ref[0, pl.ds(w0, window), :]
        s = lax.dot_general(_stack_pair(q_ref[0]), kb, _NT, preferred_element_type=F32)
        outs = []
        for a in range(2):
            t = s[a * tq:(a + 1) * tq] + bias_sc[a]
            if edge:
                k_pos = q_pos_base + qi * tq - BAND_PAST + lax.broadcasted_iota(jnp.int32, (1, window), 1)
                t = jnp.where(k_pos >= 0, t, NEG_INF)
            p = jnp.exp(t - jnp.max(t, axis=1, keepdims=True))
            l = jnp.sum(p, axis=1, keepdims=True)
            outs.append(jnp.dot(p.astype(BF16), vb, preferred_element_type=F32) / l)
        lo = lax.broadcasted_iota(jnp.int32, outs[0].shape, 1) < HEAD_DIM
        o_ref[0] = jnp.where(lo, outs[0], outs[1]).astype(BF16)

    if n_edge:
        pl.when(qi < n_edge)(lambda: attend(True))
        pl.when(qi >= n_edge)(lambda: attend(False))
    else:
        attend(False)


def _band(q, k, v, cvec, *, tq, window, q_pos_base, windowed):
    b, t_q, width = q.shape
    t_k = k.shape[1]
    pairs = width // LANES
    n_edge = max(0, -(-(BAND_PAST - q_pos_base) // tq))
    qspec = pl.BlockSpec((1, tq, LANES), lambda p, bi, i: (bi, i, p))
    kvspec = pl.BlockSpec((1, t_k, LANES), lambda p, bi, i: (bi, 0, p))
    cspec = pl.BlockSpec((1, 2, cvec.shape[2]), lambda p, bi, i: (p, 0, 0))
    return pl.pallas_call(
        functools.partial(_band_kernel, tq=tq, window=window, q_pos_base=q_pos_base, windowed=windowed,
                          n_edge=n_edge),
        grid=(pairs, b, t_q // tq),
        in_specs=[cspec, qspec, kvspec, kvspec],
        out_specs=qspec,
        out_shape=jax.ShapeDtypeStruct(q.shape, BF16),
        scratch_shapes=[pltpu.VMEM((2, tq, window), F32)],
        compiler_params=_params(3),
        name="band_attn",
    )(cvec, q, k, v)


def _proj_odd_kernel(x_ref, g_ref, w_ref, q_ref, k_ref, v_ref, kt_ref, vt_ref, *, tail_every):
    h = _rms(x_ref[...], g_ref[...]).astype(BF16)

    def seg(i):
        return jnp.dot(h, w_ref[:, i * D_MODEL:(i + 1) * D_MODEL], preferred_element_type=F32)

    q_ref[...] = (seg(0) * QK_SCALE).astype(BF16)
    kz = seg(1)
    k_ref[...] = kz.astype(BF16)
    vz = seg(2)
    v_ref[...] = vz.astype(BF16)

    @pl.when(pl.program_id(0) % tail_every == tail_every - 1)
    def _():
        kt_ref[...] = kz
        vt_ref[...] = vz


def _proj_odd(x, g, w, *, tail_every):
    m = x.shape[0]
    tm = min(ROW_TILE, m)
    n = m // tm
    row = pl.BlockSpec((tm, D_MODEL), lambda i: (i, 0))
    tail = pl.BlockSpec((tm, D_MODEL), lambda i: (i // tail_every, 0))
    bfo = jax.ShapeDtypeStruct((m, D_MODEL), BF16)
    tailo = jax.ShapeDtypeStruct((n // tail_every * tm, D_MODEL), F32)
    return pl.pallas_call(
        functools.partial(_proj_odd_kernel, tail_every=tail_every),
        grid=(n,),
        in_specs=[row, _const_spec((1, D_MODEL)), _const_spec(w.shape)],
        out_specs=[row, row, row, tail, tail],
        out_shape=[bfo, bfo, bfo, tailo, tailo],
        compiler_params=_params(1),
        name="proj_odd",
    )(x, g, w)


def _post_kernel(*refs, n_o):
    o_refs = refs[:n_o]
    x_ref, wo_ref, gpm_ref, gpre_ref, wup_ref, wdn_ref, gpf_ref, out_ref = refs[n_o:]
    o = o_refs[0][...] if n_o == 1 else jnp.concatenate([r[...] for r in o_refs], axis=1)
    mixed = jnp.dot(o, wo_ref[...], preferred_element_type=F32)
    x1 = x_ref[...] + _rms(mixed, gpm_ref[...])
    h = _rms(x1, gpre_ref[...]).astype(BF16)
    acc = jnp.zeros(x1.shape, F32)
    for c in range(D_FF // FF_CHUNK):
        u = jnp.dot(h, wup_ref[:, c * FF_CHUNK:(c + 1) * FF_CHUNK], preferred_element_type=F32)
        u = jnp.maximum(u, 0.0)
        acc = acc + jnp.dot((u * u).astype(BF16), wdn_ref[c * FF_CHUNK:(c + 1) * FF_CHUNK, :],
                            preferred_element_type=F32)
    out_ref[...] = x1 + _rms(acc, gpf_ref[...])


def _post(o_list, x, wo, gpm, gpre, wup, wdn, gpf):
    m = x.shape[0]
    tm = min(ROW_TILE, m)
    row = lambda w: pl.BlockSpec((tm, w), lambda i: (i, 0))
    gspec = _const_spec((1, D_MODEL))
    return pl.pallas_call(
        functools.partial(_post_kernel, n_o=len(o_list)),
        grid=(m // tm,),
        in_specs=[row(o.shape[1]) for o in o_list]
                 + [row(D_MODEL), _const_spec(wo.shape), gspec, gspec,
                    _const_spec(wup.shape), _const_spec(wdn.shape), gspec],
        out_specs=row(D_MODEL),
        out_shape=jax.ShapeDtypeStruct((m, D_MODEL), F32),
        compiler_params=_params(1),
        name="post_mix_mlp",
    )(*o_list, x, wo, gpm, gpre, wup, wdn, gpf)


def _band_bias_vectors(table):
    n = 2 * BAND_PAST
    m = np.arange(n)
    d = np.where(m <= n - REL_CLIP, -m, n - m)
    idx = np.clip(d + BAND_PAST, -REL_CLIP, REL_CLIP) + REL_CLIP
    return jnp.take(table.astype(F32), jnp.asarray(idx, jnp.int32), axis=1).reshape(H_C // 2, 2, n)


def _pad_time(x, front, back):
    return jnp.pad(x, ((0, 0), (front, back), (0, 0)))


def kernel(x_prompt, x_sample, cache_a_k, cache_a_v, cache_b_k, cache_b_v, cache_b_logf, cache_c_k, cache_c_v,
           w_in_even, b_forget, lam_q1, lam_k1, lam_q2, lam_k2, subln_g, w_out_even, w_in_odd, rel_bias,
           w_out_odd, g_pre_mix, g_post_mix, g_pre_ffn, g_post_ffn, w_ffn_up, w_ffn_down):
    b_p, t_p, _ = x_prompt.shape
    b_s, t_s, _ = x_sample.shape
    past = cache_b_logf.shape[2]
    m_p, m_s = b_p * t_p, b_s * t_s
    xp = x_prompt.reshape(m_p, D_MODEL)
    xs = x_sample.reshape(m_s, D_MODEL)
    gvec = lambda a, l: a[l].reshape(1, D_MODEL)

    lam_init = 0.8 - 0.6 * math.exp(-0.3 * 0)
    w_in = w_in_even[0]
    w_main = w_in[:, :N_SEG * A_W].astype(BF16)
    w_f = w_in[:, N_SEG * A_W:]
    wf = jnp.pad(w_f, ((0, 0), (0, LANES - H_B))).astype(BF16)
    wft = jnp.pad(w_f.T, ((0, 16 - H_B), (0, 0))).astype(BF16)
    bf_row = jnp.pad(b_forget[0], (0, LANES - H_B)).reshape(1, LANES)
    bf_col = b_forget[0].reshape(H_B, 1)
    slopes = 2.0 ** (-8.0 * jnp.arange(1, H_A + 1, dtype=F32) / H_A)
    lamv = jnp.stack([lam_q1[0], lam_k1[0], lam_q2[0], lam_k2[0]]).astype(F32)
    sub_g = subln_g[0].reshape(1, 2 * HEAD_DIM)
    wo0 = w_out_even[0].astype(BF16)
    wup0, wdn0 = w_ffn_up[0].astype(BF16), w_ffn_down[0].astype(BF16)

    w_vt = jnp.stack([w_in[:, 2 * A_W:3 * A_W].T, w_in[:, 5 * A_W:6 * A_W].T]).astype(BF16)

    def even_layer(x, bsz, t, caches):
        m = x.shape[0]
        tm = min(ROW_TILE, m)
        (qa, ka, kab, va, vab, qb, kb, kbb, vb, vbb, lf, lft) = _proj_even(
            x, gvec(g_pre_mix, 0), w_main, w_vt, wf, wft, bf_row, bf_col,
            tiles_per_batch=(t // tm) if caches is None else None)
        r3 = lambda a: a.reshape(bsz, t, A_W)
        lft = lft.transpose(1, 0, 2).reshape(H_B, bsz, t).transpose(1, 0, 2)
        if caches is None:
            oa = _flash_t("diff", r3(qa), r3(kab), vab, (slopes, lamv, sub_g), tq=512, lam_init=lam_init)
            ob = _flash_t("fox", r3(qb), r3(kbb), vbb, (_cumsum_time(lft),), tq=512)
        else:
            c_ka, c_va, c_kb, c_vb, c_lf = caches
            p_len = c_ka.shape[1]
            t_all = p_len + t
            t_pad = -(-t_all // LANES) * LANES
            cat = lambda c, new: _pad_time(
                jnp.concatenate([c.reshape(bsz, p_len, A_W).astype(BF16), r3(new)], axis=1), 0, t_pad - t_all)
            kab3, vab3, kbb3, vbb3 = cat(c_ka, kab), cat(c_va, vab), cat(c_kb, kbb), cat(c_vb, vbb)
            lf_all = jnp.concatenate([c_lf.astype(F32).transpose(0, 2, 1), lft], axis=2)
            cum = _cumsum_time(jnp.pad(lf_all, ((0, 0), (0, 0), (0, t_pad - t_all))))
            oa = _flash("diff", r3(qa), kab3, vab3, (slopes, lamv, sub_g), tq=t, tk=t_pad, q_pos0=p_len,
                        lam_init=lam_init)
            ob = _flash("fox", r3(qb), kbb3, vbb3, (cum,), tq=t, tk=t_pad, q_pos0=p_len)
        x_out = _post([oa.reshape(m, A_W), ob.reshape(m, A_W)], x, wo0, gvec(g_post_mix, 0),
                      gvec(g_pre_ffn, 0), wup0, wdn0, gvec(g_post_ffn, 0))
        new = (ka.reshape(1, bsz, t, H_A, 2 * HEAD_DIM), va.reshape(1, bsz, t, H_A, 2 * HEAD_DIM),
               kb.reshape(1, bsz, t, H_B, HEAD_DIM), vb.reshape(1, bsz, t, H_B, HEAD_DIM),
               lf.reshape(1, bsz, t, H_B))
        return x_out, new

    xp, new_p_even = even_layer(xp, b_p, t_p, None)
    xs, new_s_even = even_layer(xs, b_s, t_s, (cache_a_k[0], cache_a_v[0], cache_b_k[0], cache_b_v[0],
                                              cache_b_logf[0]))

    w_odd = w_in_odd[0].astype(BF16)
    cvec = _band_bias_vectors(rel_bias[0])
    wo1 = w_out_odd[0].astype(BF16)
    wup1, wdn1 = w_ffn_up[1].astype(BF16), w_ffn_down[1].astype(BF16)

    def odd_layer(x, bsz, t, caches):
        m = x.shape[0]
        tm = min(ROW_TILE, m)
        tail_every = (t // tm) if caches is None else 1
        q, k, v, k_tail, v_tail = _proj_odd(x, gvec(g_pre_mix, 1), w_odd, tail_every=tail_every)
        r3 = lambda a: a.reshape(bsz, t, D_MODEL)
        if caches is None:
            kp, vp = _pad_time(r3(k), BAND_PAST, 0), _pad_time(r3(v), BAND_PAST, 0)
            tq = 256
            o = _band(r3(q), kp, vp, cvec, tq=tq, window=BAND_PAST + tq, q_pos_base=0, windowed=True)
            w_keep = min(BAND_PAST, t)
            new = (k_tail.reshape(1, bsz, w_keep, H_C, HEAD_DIM), v_tail.reshape(1, bsz, w_keep, H_C, HEAD_DIM))
        else:
            c_k, c_v = caches
            w_len = c_k.shape[1]
            window = -(-(w_len + t) // LANES) * LANES
            cat = lambda c, new_: _pad_time(
                jnp.concatenate([c.reshape(bsz, w_len, D_MODEL).astype(BF16), r3(new_)], axis=1),
                0, window - w_len - t)
            o = _band(r3(q), cat(c_k, k), cat(c_v, v), cvec, tq=t, window=window, q_pos_base=past,
                      windowed=False)
            upd = lambda c, tail: jnp.concatenate(
                [c.reshape(bsz, w_len, D_MODEL), tail.reshape(bsz, t, D_MODEL)], axis=1)[:, t:].reshape(
                    1, bsz, w_len, H_C, HEAD_DIM)
            new = (upd(c_k, k_tail), upd(c_v, v_tail))
        x_out = _post([o.reshape(m, D_MODEL)], x, wo1, gvec(g_post_mix, 1), gvec(g_pre_ffn, 1),
                      wup1, wdn1, gvec(g_post_ffn, 1))
        return x_out, new

    xp, new_p_odd = odd_layer(xp, b_p, t_p, None)
    xs, new_s_odd = odd_layer(xs, b_s, t_s, (cache_c_k[0], cache_c_v[0]))

    return (xp.reshape(b_p, t_p, D_MODEL), xs.reshape(b_s, t_s, D_MODEL),
            *new_p_even, *new_p_odd, *new_s_even, *new_s_odd)
```

```python
import functools
import math

import numpy as np
import jax
import jax.numpy as jnp
from jax import lax
from jax.experimental import pallas as pl
from jax.experimental.pallas import tpu as pltpu

F32 = jnp.float32
BF16 = jnp.bfloat16

D_MODEL = 1024
HEAD_DIM = 64
CHUNK = 64
H_A = 4
H_B = 8
H_C = 16
A_W = 512
N_SEG = 6
BAND_PAST = 512
REL_CLIP = 256
D_FF = 4 * D_MODEL
NORM_EPS = 1e-6
NEG_INF = -1e30
QK_SCALE = HEAD_DIM ** -0.5

LANES = 128
ROW_TILE = 512
FF_CHUNK = 1024
VMEM_LIMIT = 56 * 1024 * 1024

_NT = (((1,), (1,)), ((), ()))


def _rms(x, g):
    ms = jnp.mean(x * x, axis=-1, keepdims=True)
    return x * lax.rsqrt(ms + NORM_EPS) * g


def _log_sigmoid(x):
    t = -x
    return -(jnp.maximum(t, 0.0) + jnp.log1p(jnp.exp(-jnp.abs(t))))


def _const_spec(shape):
    nd = len(shape)
    return pl.BlockSpec(shape, lambda *_: (0,) * nd, pipeline_mode=pl.Buffered(1))


def _params(n_axes):
    return pltpu.CompilerParams(dimension_semantics=("arbitrary",) * n_axes,
                                vmem_limit_bytes=VMEM_LIMIT)


def _proj_even_kernel(x_ref, g_ref, w_ref, wvt_ref, wf_ref, wft_ref, bf_ref, bft_ref,
                      qa_ref, ka_ref, kab_ref, va_ref, vab_ref,
                      qb_ref, kb_ref, kbb_ref, vb_ref, vbb_ref, lf_ref, lft_ref, *, v_feature_major):
    h = _rms(x_ref[...], g_ref[...]).astype(BF16)

    def seg(i):
        return jnp.dot(h, w_ref[:, i * A_W:(i + 1) * A_W], preferred_element_type=F32)

    def seg_t(i):
        return lax.dot_general(wvt_ref[i], h, _NT, preferred_element_type=F32)

    qa_ref[...] = (seg(0) * QK_SCALE).astype(BF16)
    z = seg(1)
    ka_ref[...] = z
    kab_ref[...] = z.astype(BF16)
    z = seg(2)
    va_ref[...] = z
    if v_feature_major:
        vab_ref[0] = seg_t(0).astype(BF16)
    else:
        vab_ref[...] = z.astype(BF16)
    qb_ref[...] = (seg(3) * QK_SCALE).astype(BF16)
    z = seg(4)
    kb_ref[...] = z
    kbb_ref[...] = z.astype(BF16)
    z = seg(5)
    vb_ref[...] = z
    if v_feature_major:
        vbb_ref[0] = seg_t(1).astype(BF16)
    else:
        vbb_ref[...] = z.astype(BF16)
    fz = jnp.dot(h, wf_ref[...], preferred_element_type=F32)
    lf_ref[...] = _log_sigmoid(fz + bf_ref[...])[:, :H_B]
    fzt = lax.dot_general(wft_ref[...], h, _NT, preferred_element_type=F32)
    lft_ref[0] = _log_sigmoid(fzt[:H_B] + bft_ref[...])


def _proj_even(x, g, w_main, w_vt, wf, wft, bf_row, bf_col, *, tiles_per_batch=None):
    m = x.shape[0]
    tm = min(ROW_TILE, m)
    n = m // tm
    row = lambda w: pl.BlockSpec((tm, w), lambda i: (i, 0))
    f32o = jax.ShapeDtypeStruct((m, A_W), F32)
    bfo = jax.ShapeDtypeStruct((m, A_W), BF16)
    if tiles_per_batch:
        tpb = tiles_per_batch
        vo = jax.ShapeDtypeStruct((n // tpb, A_W, tpb * tm), BF16)
        vspec = pl.BlockSpec((1, A_W, tm), lambda i: (i // tpb, 0, i % tpb))
    else:
        vo, vspec = bfo, row(A_W)
    out_shape = [bfo, f32o, bfo, f32o, vo, bfo, f32o, bfo, f32o, vo,
                 jax.ShapeDtypeStruct((m, H_B), F32),
                 jax.ShapeDtypeStruct((n, H_B, tm), F32)]
    out_specs = ([row(A_W)] * 4 + [vspec] + [row(A_W)] * 4 + [vspec]
                 + [row(H_B), pl.BlockSpec((1, H_B, tm), lambda i: (i, 0, 0))])
    return pl.pallas_call(
        functools.partial(_proj_even_kernel, v_feature_major=bool(tiles_per_batch)),
        grid=(n,),
        in_specs=[row(D_MODEL), _const_spec((1, D_MODEL)), _const_spec(w_main.shape),
                  _const_spec(w_vt.shape), _const_spec(wf.shape), _const_spec(wft.shape),
                  _const_spec(bf_row.shape), _const_spec(bf_col.shape)],
        out_specs=out_specs,
        out_shape=out_shape,
        compiler_params=_params(1),
        name="proj_even",
    )(x, g, w_main, w_vt, wf, wft, bf_row, bf_col)


def _cumsum_kernel(x_ref, o_ref):
    x = x_ref[0]
    t = x.shape[1]
    col = lax.broadcasted_iota(jnp.int32, x.shape, 1)
    s = 1
    while s < t:
        x = x + jnp.where(col >= s, pltpu.roll(x, s, 1), 0.0)
        s *= 2
    o_ref[0] = x


def _cumsum_time(x):
    b, h, t = x.shape
    spec = pl.BlockSpec((1, h, t), lambda i: (i, 0, 0))
    return pl.pallas_call(
        _cumsum_kernel, grid=(b,), in_specs=[spec], out_specs=spec,
        out_shape=jax.ShapeDtypeStruct(x.shape, F32),
        compiler_params=_params(1), name="cumsum_time",
    )(x)


def _stack_pair(q):
    q32 = q.astype(F32)
    lo = lax.broadcasted_iota(jnp.int32, q32.shape, 1) < HEAD_DIM
    return jnp.concatenate([jnp.where(lo, q32, 0.0), jnp.where(lo, 0.0, q32)], axis=0).astype(BF16)


def _online_update(a, u, c, vb, m_sc, l_sc, acc_sc):
    m_prev = m_sc[a]
    m_next = jnp.maximum(m_prev, jnp.max(u, axis=1, keepdims=True) + c)
    p = jnp.exp(u - jnp.tile(m_next - c, (1, u.shape[1] // LANES)))
    alpha = jnp.exp(m_prev - m_next)
    l_sc[a] = alpha * l_sc[a] + jnp.sum(p, axis=1, keepdims=True)
    acc_sc[a] = alpha * acc_sc[a] + jnp.dot(p.astype(BF16), vb, preferred_element_type=F32)
    m_sc[a] = m_next


def _flash_kernel(*refs, mode, tq, tk, q_pos0, diag_off, has_interior, lam_init):
    if mode == "diff":
        slopes_ref, lamv_ref, g_ref, q_ref, k_ref, v_ref, o_ref, m_sc, l_sc, acc_sc, bint_sc, bdiag_sc = refs
    else:
        cum_ref, q_ref, k_ref, v_ref, o_ref, m_sc, l_sc, acc_sc, bdiag_sc = refs
    pair = pl.program_id(0)
    q0 = q_pos0 + pl.program_id(2) * tq
    n_int = q0 // tk

    @pl.when((pl.program_id(1) == 0) & (pl.program_id(2) == 0))
    def _():
        row = lax.broadcasted_iota(jnp.int32, (tq, tk), 0)
        col = lax.broadcasted_iota(jnp.int32, (tq, tk), 1)
        if mode == "diff":
            neg_slope = -slopes_ref[pair]
            if has_interior:
                bint_sc[...] = neg_slope * (row - col).astype(F32)
            visible = (col >> 6) <= ((row + diag_off) >> 6)
            dist = jnp.abs(row + diag_off - col).astype(F32)
            bdiag_sc[...] = jnp.where(visible, neg_slope * dist, NEG_INF)
        else:
            bdiag_sc[...] = jnp.where(col <= row + diag_off, 0.0, NEG_INF)

    q2 = _stack_pair(q_ref[0])
    m_sc[...] = jnp.full(m_sc.shape, NEG_INF, F32)
    l_sc[...] = jnp.zeros(l_sc.shape, F32)
    acc_sc[...] = jnp.zeros(acc_sc.shape, F32)

    def scores(k0):
        kb = k_ref[0, pl.ds(k0, tk), :]
        vb = v_ref[0, pl.ds(k0, tk), :]
        return lax.dot_general(q2, kb, _NT, preferred_element_type=F32), vb

    def cum_row(a, k0):
        return cum_ref[0, 2 * pair + a, :, pl.ds(k0, tk)]

    def interior(j, carry):
        k0 = pl.multiple_of(j * tk, tk)
        s, vb = scores(k0)
        for a in range(2):
            sa = s[a * tq:(a + 1) * tq]
            if mode == "diff":
                _online_update(a, sa + bint_sc[...], -slopes_ref[pair] * (q0 - k0).astype(F32),
                               vb, m_sc, l_sc, acc_sc)
            else:
                _online_update(a, sa - cum_row(a, k0), 0.0, vb, m_sc, l_sc, acc_sc)
        return carry

    if has_interior:
        lax.fori_loop(0, n_int, interior, 0)

    k0 = pl.multiple_of(n_int * tk, tk)
    s, vb = scores(k0)
    for a in range(2):
        sa = s[a * tq:(a + 1) * tq]
        if mode == "diff":
            u = sa + bdiag_sc[...]
        else:
            u = (sa - cum_row(a, k0)) + bdiag_sc[...]
        _online_update(a, u, 0.0, vb, m_sc, l_sc, acc_sc)

    o0 = acc_sc[0] / l_sc[0]
    o1 = acc_sc[1] / l_sc[1]
    if mode == "diff":
        lv = lamv_ref[...]
        lam = (jnp.exp(jnp.sum(lv[0:1] * lv[1:2], axis=1, keepdims=True))
               - jnp.exp(jnp.sum(lv[2:3] * lv[3:4], axis=1, keepdims=True)) + lam_init)
        oa = o0 - lam * o1
        o_ref[0] = (_rms(oa, g_ref[...]) * (1.0 - lam_init)).astype(BF16)
    else:
        lo = lax.broadcasted_iota(jnp.int32, o0.shape, 1) < HEAD_DIM
        o_ref[0] = jnp.where(lo, o0, o1).astype(BF16)


def _flash(mode, q, k, v, extras, *, tq, tk, q_pos0, lam_init=0.0):
    b, t_q, width = q.shape
    t_k = k.shape[1]
    pairs = width // LANES
    n_q = t_q // tq
    has_interior = t_k > tk
    diag_off = q_pos0 % tk
    assert all((q_pos0 + i * tq) % tk == diag_off for i in range(n_q)) and diag_off + tq <= tk
    assert t_k % tk == 0 and (q_pos0 + (n_q - 1) * tq) // tk < t_k // tk
    grid = (pairs, b, n_q)
    qspec = pl.BlockSpec((1, tq, LANES), lambda p, bi, i: (bi, i, p))
    kvspec = pl.BlockSpec((1, t_k, LANES), lambda p, bi, i: (bi, 0, p))
    bias_scratch = [pltpu.VMEM((tq, tk), F32)]
    if mode == "diff":
        slopes, lamv, g = extras
        extra_specs = [pl.BlockSpec(memory_space=pltpu.SMEM),
                       pl.BlockSpec(lamv.shape, lambda p, bi, i: (0, 0)),
                       pl.BlockSpec(g.shape, lambda p, bi, i: (0, 0))]
        bias_scratch = [pltpu.VMEM((tq, tk) if has_interior else (8, LANES), F32)] + bias_scratch
    else:
        extras = (extras[0][:, :, None, :],)
        extra_specs = [pl.BlockSpec((1,) + extras[0].shape[1:], lambda p, bi, i: (bi, 0, 0, 0))]
    return pl.pallas_call(
        functools.partial(_flash_kernel, mode=mode, tq=tq, tk=tk, q_pos0=q_pos0, diag_off=diag_off,
                          has_interior=has_interior, lam_init=lam_init),
        grid=grid,
        in_specs=extra_specs + [qspec, kvspec, kvspec],
        out_specs=qspec,
        out_shape=jax.ShapeDtypeStruct(q.shape, BF16),
        scratch_shapes=[pltpu.VMEM((2, tq, LANES), F32)] * 3 + bias_scratch,
        compiler_params=_params(3),
        name="flash_" + mode,
    )(*extras, q, k, v)


ONES_ROWS = 16


def _with_ones(vt):
    return jnp.concatenate([vt, jnp.ones((ONES_ROWS, vt.shape[1]), BF16)], axis=0)


AUG_TERMS = 3


def _lane_terms(lane, first, terms):
    out = jnp.zeros(terms[0].shape, F32)
    for t, v in enumerate(terms):
        out = jnp.where(lane == first + t, v, out)
    return out


def _flash_t_kernel(*refs, mode, tq, lam_init):
    tk = tq
    if mode == "diff":
        (slopes_ref, lamv_ref, g_ref, q_ref, k_ref, vt_ref, o_ref,
         m_sc, acc_sc, q2_sc, s0_sc, s1_sc, mu_sc, kaug_sc, bdiag_sc) = refs
    else:
        cum_ref, q_ref, k_ref, vt_ref, o_ref, m_sc, acc_sc, q2_sc, s0_sc, s1_sc, mu_sc, kaug_sc, bdiag_sc = refs
    pair = pl.program_id(0)
    qi = pl.program_id(2)
    t_all = kaug_sc.shape[0]
    s_bufs = (s0_sc, s1_sc)

    @pl.when((pl.program_id(1) == 0) & (qi == 0))
    def _():
        row = lax.broadcasted_iota(jnp.int32, (tk, tq), 0)
        col = lax.broadcasted_iota(jnp.int32, (tk, tq), 1)
        lane_q = lax.broadcasted_iota(jnp.int32, (2 * tq, LANES), 1)
        first_q = jnp.where(lax.broadcasted_iota(jnp.int32, (2 * tq, LANES), 0) < tq, 0, AUG_TERMS)
        if mode == "diff":
            slope = slopes_ref[pair]
            visible = (row >> 6) <= (col >> 6)
            bdiag_sc[...] = jnp.where(visible, -2.0 * slope * jnp.maximum(row - col, 0).astype(F32), NEG_INF)
            q2_sc[:, LANES:] = jnp.where(lane_q < AUG_TERMS, 1.0, 0.0).astype(BF16)
            for c0 in range(0, t_all, tk):
                j = c0 + lax.broadcasted_iota(jnp.int32, (tk, LANES), 0)
                lane = lax.broadcasted_iota(jnp.int32, (tk, LANES), 1)
                terms = [((j >> 8) << 8).astype(F32), (((j >> 4) & 15) << 4).astype(F32), (j & 15).astype(F32)]
                kaug_sc[c0:c0 + tk, :] = (slope * _lane_terms(lane, 0, terms)).astype(BF16)
        else:
            bdiag_sc[...] = jnp.where(row <= col, 0.0, NEG_INF)
            mine = (lane_q >= first_q) & (lane_q < first_q + AUG_TERMS)
            q2_sc[:, LANES:] = jnp.where(mine, -1.0, 0.0).astype(BF16)

    if mode == "fox":
        @pl.when(qi == 0)
        def _():
            for c0 in range(0, t_all, tk):
                lane = lax.broadcasted_iota(jnp.int32, (tk, LANES), 1)
                feat = jnp.zeros((tk, LANES), F32)
                for a in range(2):
                    r = cum_ref[0, 2 * pair + a, :, c0:c0 + tk]
                    ck = jnp.broadcast_to(r, (LANES, tk)).T
                    hi = ck.astype(BF16).astype(F32)
                    mid = (ck - hi).astype(BF16).astype(F32)
                    lo = ((ck - hi) - mid).astype(BF16).astype(F32)
                    feat = feat + _lane_terms(lane, AUG_TERMS * a, [hi, mid, lo])
                kaug_sc[c0:c0 + tk, :] = feat.astype(BF16)

    q2_sc[:, :LANES] = _stack_pair(q_ref[0])
    m_sc[...] = jnp.full(m_sc.shape, NEG_INF, F32)
    acc_sc[...] = jnp.zeros(acc_sc.shape, F32)

    def stage_a(k0, slot, diag):
        kk = jnp.concatenate([k_ref[0, pl.ds(k0, tk), :], kaug_sc[pl.ds(k0, tk), :]], axis=1)
        st = lax.dot_general(kk, q2_sc[...], _NT, preferred_element_type=F32)
        for a in range(2):
            sa = st[:, a * tq:(a + 1) * tq]
            if diag:
                sa = sa + bdiag_sc[...]
            s_bufs[slot][:, a * tq:(a + 1) * tq] = sa
            mu_sc[slot, :, a * tq:(a + 1) * tq] = jnp.max(sa, axis=0, keepdims=True)

    def stage_b(k0, slot):
        vta = _with_ones(vt_ref[0, :, pl.ds(k0, tk)])
        for a in range(2):
            m_prev = m_sc[a]
            m_next = jnp.maximum(m_prev, mu_sc[slot, :, a * tq:(a + 1) * tq])
            p = jnp.exp(s_bufs[slot][:, a * tq:(a + 1) * tq] - m_next)
            alpha = jnp.exp(m_prev - m_next)
            acc_sc[a] = alpha * acc_sc[a] + jnp.dot(vta, p.astype(BF16), preferred_element_type=F32)
            m_sc[a] = m_next

    blk = lambda j: pl.multiple_of(j * tk, tk)
    pl.when(qi == 0)(lambda: stage_a(0, 0, True))
    pl.when(qi > 0)(lambda: stage_a(0, 0, False))

    def two_blocks(i, carry):
        stage_a(blk(2 * i + 1), 1, False)
        stage_b(blk(2 * i), 0)
        stage_a(blk(2 * i + 2), 0, False)
        stage_b(blk(2 * i + 1), 1)
        return carry

    lax.fori_loop(0, (qi - 1) // 2, two_blocks, 0)

    @pl.when(qi == 0)
    def _():
        stage_b(0, 0)

    @pl.when(qi % 2 == 1)
    def _():
        stage_a(blk(qi), 1, True)
        stage_b(blk(qi - 1), 0)
        stage_b(blk(qi), 1)

    @pl.when((qi % 2 == 0) & (qi > 0))
    def _():
        stage_a(blk(qi - 1), 1, False)
        stage_b(blk(qi - 2), 0)
        stage_a(blk(qi), 0, True)
        stage_b(blk(qi - 1), 1)
        stage_b(blk(qi), 0)

    d = 2 * HEAD_DIM
    o0 = acc_sc[0, :d, :] / acc_sc[0, d:d + 1, :]
    o1 = acc_sc[1, :d, :] / acc_sc[1, d:d + 1, :]
    if mode == "diff":
        lv = lamv_ref[...]
        lam = (jnp.exp(jnp.sum(lv[0:1] * lv[1:2], axis=1, keepdims=True))
               - jnp.exp(jnp.sum(lv[2:3] * lv[3:4], axis=1, keepdims=True)) + lam_init)
        oa = o0 - lam * o1
        ms = jnp.mean(oa * oa, axis=0, keepdims=True)
        y = (oa * lax.rsqrt(ms + NORM_EPS)).T * g_ref[...]
        o_ref[0] = (y * (1.0 - lam_init)).astype(BF16)
    else:
        lo = lax.broadcasted_iota(jnp.int32, o0.shape, 0) < HEAD_DIM
        o_ref[0] = jnp.where(lo, o0, o1).T.astype(BF16)


def _flash_t(mode, q, k, vt, extras, *, tq, lam_init=0.0):
    b, t, width = q.shape
    pairs = width // LANES
    qspec = pl.BlockSpec((1, tq, LANES), lambda p, bi, i: (bi, i, p))
    kspec = pl.BlockSpec((1, t, LANES), lambda p, bi, i: (bi, 0, p))
    vspec = pl.BlockSpec((1, LANES, t), lambda p, bi, i: (bi, p, 0))
    scratch = [pltpu.VMEM((2, 1, tq), F32),
               pltpu.VMEM((2, LANES + ONES_ROWS, tq), F32),
               pltpu.VMEM((2 * tq, 2 * LANES), BF16),
               pltpu.VMEM((tq, 2 * tq), F32), pltpu.VMEM((tq, 2 * tq), F32),
               pltpu.VMEM((2, 1, 2 * tq), F32),
               pltpu.VMEM((t, LANES), BF16),
               pltpu.VMEM((tq, tq), F32)]
    if mode == "diff":
        slopes, lamv, g = extras
        extra_specs = [pl.BlockSpec(memory_space=pltpu.SMEM),
                       pl.BlockSpec(lamv.shape, lambda p, bi, i: (0, 0)),
                       pl.BlockSpec(g.shape, lambda p, bi, i: (0, 0))]
    else:
        extras = (extras[0][:, :, None, :],)
        extra_specs = [pl.BlockSpec((1,) + extras[0].shape[1:], lambda p, bi, i: (bi, 0, 0, 0))]
    return pl.pallas_call(
        functools.partial(_flash_t_kernel, mode=mode, tq=tq, lam_init=lam_init),
        grid=(pairs, b, t // tq),
        in_specs=extra_specs + [qspec, kspec, vspec],
        out_specs=qspec,
        out_shape=jax.ShapeDtypeStruct(q.shape, BF16),
        scratch_shapes=scratch,
        compiler_params=_params(3),
        name="flash_t_" + mode,
    )(*extras, q, k, vt)


def _band_kernel(c_ref, q_ref, k_ref, v_ref, o_ref, bias_sc, *, tq, window, q_pos_base, windowed, n_edge):
    qi = pl.program_id(2)

    @pl.when((pl.program_id(1) == 0) & (qi == 0))
    def _():
        row = lax.broadcasted_iota(jnp.int32, (tq, window), 0)
        col = lax.broadcasted_iota(jnp.int32, (tq, window), 1)
        visible = ((col >> 6) >= (row >> 6)) & ((col >> 6) <= (row >> 6) + BAND_PAST // CHUNK)
        for a in range(2):
            c = jnp.broadcast_to(c_ref[0, a:a + 1, :], (tq, c_ref.shape[2]))
            bias = pltpu.roll(c, 0, 1, stride=1, stride_axis=0)[:, :window]
            bias_sc[a] = jnp.where(visible, bias, NEG_INF)

    def attend(edge):
        w0 = pl.multiple_of(qi * tq, tq) if windowed else 0
        kb = k_ref[0, pl.ds(w0, window), :]
        vb = v_ref[0, pl.ds(w0, window), :]
        s = lax.dot_general(_stack_pair(q_ref[0]), kb, _NT, preferred_element_type=F32)
        outs = []
        for a in range(2):
            t = s[a * tq:(a + 1) * tq] + bias_sc[a]
            if edge:
                k_pos = q_pos_base + qi * tq - BAND_PAST + lax.broadcasted_iota(jnp.int32, (1, window), 1)
                t = jnp.where(k_pos >= 0, t, NEG_INF)
            p = jnp.exp(t - jnp.max(t, axis=1, keepdims=True))
            l = jnp.sum(p, axis=1, keepdims=True)
            outs.append(jnp.dot(p.astype(BF16), vb, preferred_element_type=F32) / l)
        lo = lax.broadcasted_iota(jnp.int32, outs[0].shape, 1) < HEAD_DIM
        o_ref[0] = jnp.where(lo, outs[0], outs[1]).astype(BF16)

    if n_edge:
        pl.when(qi < n_edge)(lambda: attend(True))
        pl.when(qi >= n_edge)(lambda: attend(False))
    else:
        attend(False)


def _band(q, k, v, cvec, *, tq, window, q_pos_base, windowed):
    b, t_q, width = q.shape
    t_k = k.shape[1]
    pairs = width // LANES
    n_edge = max(0, -(-(BAND_PAST - q_pos_base) // tq))
    qspec = pl.BlockSpec((1, tq, LANES), lambda p, bi, i: (bi, i, p))
    kvspec = pl.BlockSpec((1, t_k, LANES), lambda p, bi, i: (bi, 0, p))
    cspec = pl.BlockSpec((1, 2, cvec.shape[2]), lambda p, bi, i: (p, 0, 0))
    return pl.pallas_call(
        functools.partial(_band_kernel, tq=tq, window=window, q_pos_base=q_pos_base, windowed=windowed,
                          n_edge=n_edge),
        grid=(pairs, b, t_q // tq),
        in_specs=[cspec, qspec, kvspec, kvspec],
        out_specs=qspec,
        out_shape=jax.ShapeDtypeStruct(q.shape, BF16),
        scratch_shapes=[pltpu.VMEM((2, tq, window), F32)],
        compiler_params=_params(3),
        name="band_attn",
    )(cvec, q, k, v)


def _band_t_kernel(c_ref, q_ref, k_ref, vt_ref, o_ref, s0_sc, s1_sc, mu_sc, bias_sc, *, tq, t):
    window = BAND_PAST + tq
    n_q = t // tq
    s_bufs = (s0_sc, s1_sc)

    @pl.when(pl.program_id(1) == 0)
    def _():
        row = lax.broadcasted_iota(jnp.int32, (tq, window), 0)
        col = lax.broadcasted_iota(jnp.int32, (tq, window), 1)
        visible = ((col >> 6) >= (row >> 6)) & ((col >> 6) <= (row >> 6) + BAND_PAST // CHUNK)
        for a in range(2):
            c = jnp.broadcast_to(c_ref[0, a:a + 1, :], (tq, c_ref.shape[2]))
            bias = pltpu.roll(c, 0, 1, stride=1, stride_axis=0)[:, :window]
            bias_sc[a] = jnp.where(visible, bias, NEG_INF).T

    lane_q = lax.broadcasted_iota(jnp.int32, (2 * tq, LANES), 1)
    q_extra = jnp.where(lane_q == 0, 1.0, 0.0).astype(BF16)

    def stage_a(qb, slot):
        w0 = pl.multiple_of(qb * tq, tq)
        q2 = jnp.concatenate([_stack_pair(q_ref[0, pl.ds(w0, tq), :]), q_extra], axis=1)
        pad_row = (w0 + lax.broadcasted_iota(jnp.int32, (window, LANES), 0)) < BAND_PAST
        lane_k = lax.broadcasted_iota(jnp.int32, (window, LANES), 1)
        k_extra = jnp.where(pad_row & (lane_k == 0), NEG_INF, 0.0).astype(BF16)
        kk = jnp.concatenate([k_ref[0, pl.ds(w0, window), :], k_extra], axis=1)
        st = lax.dot_general(kk, q2, _NT, preferred_element_type=F32)
        for a in range(2):
            u = st[:, a * tq:(a + 1) * tq] + bias_sc[a]
            s_bufs[slot][:, a * tq:(a + 1) * tq] = u
            mu_sc[slot, :, a * tq:(a + 1) * tq] = jnp.max(u, axis=0, keepdims=True)

    def stage_b(qb, slot):
        w0 = pl.multiple_of(qb * tq, tq)
        vta = _with_ones(vt_ref[0, :, pl.ds(w0, window)])
        outs = []
        for a in range(2):
            p = jnp.exp(s_bufs[slot][:, a * tq:(a + 1) * tq] - mu_sc[slot, :, a * tq:(a + 1) * tq])
            r = jnp.dot(vta, p.astype(BF16), preferred_element_type=F32)
            outs.append(r[:LANES] / r[LANES:LANES + 1])
        lo = lax.broadcasted_iota(jnp.int32, outs[0].shape, 0) < HEAD_DIM
        o_ref[0, pl.ds(w0, tq), :] = jnp.where(lo, outs[0], outs[1]).T.astype(BF16)

    stage_a(0, 0)

    def two_blocks(i, carry):
        stage_a(2 * i + 1, 1)
        stage_b(2 * i, 0)
        stage_a(2 * i + 2, 0)
        stage_b(2 * i + 1, 1)
        return carry

    lax.fori_loop(0, n_q // 2 - 1, two_blocks, 0)
    stage_a(n_q - 1, 1)
    stage_b(n_q - 2, 0)
    stage_b(n_q - 1, 1)


def _band_t(q, k_pad, vt_pad, cvec, *, tq):
    b, t, width = q.shape
    pairs = width // LANES
    window = BAND_PAST + tq
    assert (t // tq) % 2 == 0 and t // tq >= 2
    qspec = pl.BlockSpec((1, t, LANES), lambda p, bi: (bi, 0, p))
    kspec = pl.BlockSpec((1, k_pad.shape[1], LANES), lambda p, bi: (bi, 0, p))
    vspec = pl.BlockSpec((1, LANES, vt_pad.shape[2]), lambda p, bi: (bi, p, 0))
    cspec = pl.BlockSpec((1, 2, cvec.shape[2]), lambda p, bi: (p, 0, 0))
    return pl.pallas_call(
        functools.partial(_band_t_kernel, tq=tq, t=t),
        grid=(pairs, b),
        in_specs=[cspec, qspec, kspec, vspec],
        out_specs=qspec,
        out_shape=jax.ShapeDtypeStruct(q.shape, BF16),
        scratch_shapes=[pltpu.VMEM((window, 2 * tq), F32), pltpu.VMEM((window, 2 * tq), F32),
                        pltpu.VMEM((2, 1, 2 * tq), F32), pltpu.VMEM((2, window, tq), F32)],
        compiler_params=_params(2),
        name="band_t_attn",
    )(cvec, q, k_pad, vt_pad)


def _proj_odd_kernel(x_ref, g_ref, w_ref, wvt_ref, q_ref, k_ref, v_ref, kt_ref, vt_ref, *,
                     tail_every, v_feature_major):
    h = _rms(x_ref[...], g_ref[...]).astype(BF16)

    def seg(i):
        return jnp.dot(h, w_ref[:, i * D_MODEL:(i + 1) * D_MODEL], preferred_element_type=F32)

    q_ref[...] = (seg(0) * QK_SCALE).astype(BF16)
    kz = seg(1)
    k_ref[...] = kz.astype(BF16)
    vz = seg(2)
    if v_feature_major:
        v_ref[0] = lax.dot_general(wvt_ref[...], h, _NT, preferred_element_type=F32).astype(BF16)
    else:
        v_ref[...] = vz.astype(BF16)

    @pl.when(pl.program_id(0) % tail_every == tail_every - 1)
    def _():
        kt_ref[...] = kz
        vt_ref[...] = vz


def _proj_odd(x, g, w, w_vt, *, tail_every, v_feature_major):
    m = x.shape[0]
    tm = min(ROW_TILE, m)
    n = m // tm
    row = pl.BlockSpec((tm, D_MODEL), lambda i: (i, 0))
    tail = pl.BlockSpec((tm, D_MODEL), lambda i: (i // tail_every, 0))
    bfo = jax.ShapeDtypeStruct((m, D_MODEL), BF16)
    tailo = jax.ShapeDtypeStruct((n // tail_every * tm, D_MODEL), F32)
    if v_feature_major:
        vo = jax.ShapeDtypeStruct((n // tail_every, D_MODEL, tail_every * tm), BF16)
        vspec = pl.BlockSpec((1, D_MODEL, tm), lambda i: (i // tail_every, 0, i % tail_every))
    else:
        vo, vspec = bfo, row
    return pl.pallas_call(
        functools.partial(_proj_odd_kernel, tail_every=tail_every, v_feature_major=v_feature_major),
        grid=(n,),
        in_specs=[row, _const_spec((1, D_MODEL)), _const_spec(w.shape), _const_spec(w_vt.shape)],
        out_specs=[row, row, vspec, tail, tail],
        out_shape=[bfo, bfo, vo, tailo, tailo],
        compiler_params=_params(1),
        name="proj_odd",
    )(x, g, w, w_vt)


def _post_kernel(*refs, n_o):
    o_refs = refs[:n_o]
    x_ref, wo_ref, gpm_ref, gpre_ref, wup_ref, wdn_ref, gpf_ref, out_ref = refs[n_o:]
    o = o_refs[0][...] if n_o == 1 else jnp.concatenate([r[...] for r in o_refs], axis=1)
    mixed = jnp.dot(o, wo_ref[...], preferred_element_type=F32)
    x1 = x_ref[...] + _rms(mixed, gpm_ref[...])
    h = _rms(x1, gpre_ref[...]).astype(BF16)
    acc = jnp.zeros(x1.shape, F32)
    for c in range(D_FF // FF_CHUNK):
        u = jnp.dot(h, wup_ref[:, c * FF_CHUNK:(c + 1) * FF_CHUNK], preferred_element_type=F32)
        u = jnp.maximum(u, 0.0)
        acc = acc + jnp.dot((u * u).astype(BF16), wdn_ref[c * FF_CHUNK:(c + 1) * FF_CHUNK, :],
                            preferred_element_type=F32)
    out_ref[...] = x1 + _rms(acc, gpf_ref[...])


def _post(o_list, x, wo, gpm, gpre, wup, wdn, gpf):
    m = x.shape[0]
    tm = min(ROW_TILE, m)
    row = lambda w: pl.BlockSpec((tm, w), lambda i: (i, 0))
    gspec = _const_spec((1, D_MODEL))
    return pl.pallas_call(
        functools.partial(_post_kernel, n_o=len(o_list)),
        grid=(m // tm,),
        in_specs=[row(o.shape[1]) for o in o_list]
                 + [row(D_MODEL), _const_spec(wo.shape), gspec, gspec,
                    _const_spec(wup.shape), _const_spec(wdn.shape), gspec],
        out_specs=row(D_MODEL),
        out_shape=jax.ShapeDtypeStruct((m, D_MODEL), F32),
        compiler_params=_params(1),
        name="post_mix_mlp",
    )(*o_list, x, wo, gpm, gpre, wup, wdn, gpf)


def _band_bias_vectors(table):
    n = 2 * BAND_PAST
    m = np.arange(n)
    d = np.where(m <= n - REL_CLIP, -m, n - m)
    idx = np.clip(d + BAND_PAST, -REL_CLIP, REL_CLIP) + REL_CLIP
    return jnp.take(table.astype(F32), jnp.asarray(idx, jnp.int32), axis=1).reshape(H_C // 2, 2, n)


def _pad_time(x, front, back):
    return jnp.pad(x, ((0, 0), (front, back), (0, 0)))


def kernel(x_prompt, x_sample, cache_a_k, cache_a_v, cache_b_k, cache_b_v, cache_b_logf, cache_c_k, cache_c_v,
           w_in_even, b_forget, lam_q1, lam_k1, lam_q2, lam_k2, subln_g, w_out_even, w_in_odd, rel_bias,
           w_out_odd, g_pre_mix, g_post_mix, g_pre_ffn, g_post_ffn, w_ffn_up, w_ffn_down):
    b_p, t_p, _ = x_prompt.shape
    b_s, t_s, _ = x_sample.shape
    past = cache_b_logf.shape[2]
    m_p, m_s = b_p * t_p, b_s * t_s
    xp = x_prompt.reshape(m_p, D_MODEL)
    xs = x_sample.reshape(m_s, D_MODEL)
    gvec = lambda a, l: a[l].reshape(1, D_MODEL)

    lam_init = 0.8 - 0.6 * math.exp(-0.3 * 0)
    w_in = w_in_even[0]
    w_main = w_in[:, :N_SEG * A_W].astype(BF16)
    w_f = w_in[:, N_SEG * A_W:]
    wf = jnp.pad(w_f, ((0, 0), (0, LANES - H_B))).astype(BF16)
    wft = jnp.pad(w_f.T, ((0, 16 - H_B), (0, 0))).astype(BF16)
    bf_row = jnp.pad(b_forget[0], (0, LANES - H_B)).reshape(1, LANES)
    bf_col = b_forget[0].reshape(H_B, 1)
    slopes = 2.0 ** (-8.0 * jnp.arange(1, H_A + 1, dtype=F32) / H_A)
    lamv = jnp.stack([lam_q1[0], lam_k1[0], lam_q2[0], lam_k2[0]]).astype(F32)
    sub_g = subln_g[0].reshape(1, 2 * HEAD_DIM)
    wo0 = w_out_even[0].astype(BF16)
    wup0, wdn0 = w_ffn_up[0].astype(BF16), w_ffn_down[0].astype(BF16)

    w_vt = jnp.stack([w_in[:, 2 * A_W:3 * A_W].T, w_in[:, 5 * A_W:6 * A_W].T]).astype(BF16)

    def even_layer(x, bsz, t, caches):
        m = x.shape[0]
        tm = min(ROW_TILE, m)
        (qa, ka, kab, va, vab, qb, kb, kbb, vb, vbb, lf, lft) = _proj_even(
            x, gvec(g_pre_mix, 0), w_main, w_vt, wf, wft, bf_row, bf_col,
            tiles_per_batch=(t // tm) if caches is None else None)
        r3 = lambda a: a.reshape(bsz, t, A_W)
        lft = lft.transpose(1, 0, 2).reshape(H_B, bsz, t).transpose(1, 0, 2)
        if caches is None:
            oa = _flash_t("diff", r3(qa), r3(kab), vab, (slopes, lamv, sub_g), tq=512, lam_init=lam_init)
            ob = _flash_t("fox", r3(qb), r3(kbb), vbb, (_cumsum_time(lft),), tq=512)
        else:
            c_ka, c_va, c_kb, c_vb, c_lf = caches
            p_len = c_ka.shape[1]
            t_all = p_len + t
            t_pad = -(-t_all // LANES) * LANES
            cat = lambda c, new: _pad_time(
                jnp.concatenate([c.reshape(bsz, p_len, A_W).astype(BF16), r3(new)], axis=1), 0, t_pad - t_all)
            kab3, vab3, kbb3, vbb3 = cat(c_ka, kab), cat(c_va, vab), cat(c_kb, kbb), cat(c_vb, vbb)
            lf_all = jnp.concatenate([c_lf.astype(F32).transpose(0, 2, 1), lft], axis=2)
            cum = _cumsum_time(jnp.pad(lf_all, ((0, 0), (0, 0), (0, t_pad - t_all))))
            oa = _flash("diff", r3(qa), kab3, vab3, (slopes, lamv, sub_g), tq=t, tk=t_pad, q_pos0=p_len,
                        lam_init=lam_init)
            ob = _flash("fox", r3(qb), kbb3, vbb3, (cum,), tq=t, tk=t_pad, q_pos0=p_len)
        x_out = _post([oa.reshape(m, A_W), ob.reshape(m, A_W)], x, wo0, gvec(g_post_mix, 0),
                      gvec(g_pre_ffn, 0), wup0, wdn0, gvec(g_post_ffn, 0))
        new = (ka.reshape(1, bsz, t, H_A, 2 * HEAD_DIM), va.reshape(1, bsz, t, H_A, 2 * HEAD_DIM),
               kb.reshape(1, bsz, t, H_B, HEAD_DIM), vb.reshape(1, bsz, t, H_B, HEAD_DIM),
               lf.reshape(1, bsz, t, H_B))
        return x_out, new

    xp, new_p_even = even_layer(xp, b_p, t_p, None)
    xs, new_s_even = even_layer(xs, b_s, t_s, (cache_a_k[0], cache_a_v[0], cache_b_k[0], cache_b_v[0],
                                              cache_b_logf[0]))

    w_odd = w_in_odd[0].astype(BF16)
    w_odd_vt = w_in_odd[0][:, 2 * D_MODEL:].T.astype(BF16)
    cvec = _band_bias_vectors(rel_bias[0])
    wo1 = w_out_odd[0].astype(BF16)
    wup1, wdn1 = w_ffn_up[1].astype(BF16), w_ffn_down[1].astype(BF16)

    def odd_layer(x, bsz, t, caches):
        m = x.shape[0]
        tm = min(ROW_TILE, m)
        tail_every = (t // tm) if caches is None else 1
        q, k, v, k_tail, v_tail = _proj_odd(x, gvec(g_pre_mix, 1), w_odd, w_odd_vt, tail_every=tail_every,
                                            v_feature_major=caches is None)
        r3 = lambda a: a.reshape(bsz, t, D_MODEL)
        if caches is None:
            kp = _pad_time(r3(k), BAND_PAST, 0)
            vtp = jnp.pad(v, ((0, 0), (0, 0), (BAND_PAST, 0)))
            o = _band_t(r3(q), kp, vtp, cvec, tq=256)
            w_keep = min(BAND_PAST, t)
            new = (k_tail.reshape(1, bsz, w_keep, H_C, HEAD_DIM), v_tail.reshape(1, bsz, w_keep, H_C, HEAD_DIM))
        else:
            c_k, c_v = caches
            w_len = c_k.shape[1]
            window = -(-(w_len + t) // LANES) * LANES
            cat = lambda c, new_: _pad_time(
                jnp.concatenate([c.reshape(bsz, w_len, D_MODEL).astype(BF16), r3(new_)], axis=1),
                0, window - w_len - t)
            o = _band(r3(q), cat(c_k, k), cat(c_v, v), cvec, tq=t, window=window, q_pos_base=past,
                      windowed=False)
            upd = lambda c, tail: jnp.concatenate(
                [c.reshape(bsz, w_len, D_MODEL), tail.reshape(bsz, t, D_MODEL)], axis=1)[:, t:].reshape(
                    1, bsz, w_len, H_C, HEAD_DIM)
            new = (upd(c_k, k_tail), upd(c_v, v_tail))
        x_out = _post([o.reshape(m, D_MODEL)], x, wo1, gvec(g_post_mix, 1), gvec(g_pre_ffn, 1),
                      wup1, wdn1, gvec(g_post_ffn, 1))
        return x_out, new

    xp, new_p_odd = odd_layer(xp, b_p, t_p, None)
    xs, new_s_odd = odd_layer(xs, b_s, t_s, (cache_c_k[0], cache_c_v[0]))

    return (xp.reshape(b_p, t_p, D_MODEL), xs.reshape(b_s, t_s, D_MODEL),
            *new_p_even, *new_p_odd, *new_s_even, *new_s_odd)
```

```python
import functools
import math

import numpy as np
import jax
import jax.numpy as jnp
from jax import lax
from jax.experimental import pallas as pl
from jax.experimental.pallas import tpu as pltpu

F32 = jnp.float32
BF16 = jnp.bfloat16

D_MODEL = 1024
HEAD_DIM = 64
CHUNK = 64
H_A = 4
H_B = 8
H_C = 16
A_W = 512
N_SEG = 6
BAND_PAST = 512
REL_CLIP = 256
D_FF = 4 * D_MODEL
NORM_EPS = 1e-6
NEG_INF = -1e30
QK_SCALE = HEAD_DIM ** -0.5

LANES = 128
ROW_TILE = 512
FF_CHUNK = 1024
VMEM_LIMIT = 56 * 1024 * 1024

_NT = (((1,), (1,)), ((), ()))


def _rms(x, g):
    ms = jnp.mean(x * x, axis=-1, keepdims=True)
    return x * lax.rsqrt(ms + NORM_EPS) * g


def _log_sigmoid(x):
    t = -x
    return -(jnp.maximum(t, 0.0) + jnp.log1p(jnp.exp(-jnp.abs(t))))


def _const_spec(shape):
    nd = len(shape)
    return pl.BlockSpec(shape, lambda *_: (0,) * nd, pipeline_mode=pl.Buffered(1))


def _params(n_axes):
    return pltpu.CompilerParams(dimension_semantics=("arbitrary",) * n_axes,
                                vmem_limit_bytes=VMEM_LIMIT)


def _proj_even_kernel(x_ref, g_ref, w_ref, wt_ref, wft_ref, bft_ref,
                      qa_ref, ka_ref, kab_ref, va_ref, vab_ref,
                      qb_ref, kb_ref, kbb_ref, vb_ref, vbb_ref, lft_ref, *, feature_major):
    h = _rms(x_ref[...], g_ref[...]).astype(BF16)
    tm = h.shape[0]

    def seg(i):
        return jnp.dot(h, w_ref[:, i * A_W:(i + 1) * A_W], preferred_element_type=F32)

    def seg_t(i):
        return lax.dot_general(wt_ref[i], h, _NT, preferred_element_type=F32)

    def store_token_head_rows(ref, z):
        for hd in range(H_A):
            ref[pl.ds(hd, tm, stride=H_A), :] = z[:, hd * LANES:(hd + 1) * LANES]

    qa_ref[...] = (seg(0) * QK_SCALE).astype(BF16)
    z = seg(1)
    store_token_head_rows(ka_ref, z)
    kab_ref[...] = z.astype(BF16)
    z = seg(2)
    store_token_head_rows(va_ref, z)
    if feature_major:
        vab_ref[0] = seg_t(0).astype(BF16)
    else:
        vab_ref[...] = z.astype(BF16)
    qb_ref[...] = (seg(3) * QK_SCALE).astype(BF16)
    z = seg(4)
    kbb_ref[...] = z.astype(BF16)
    if feature_major:
        kb_ref[0] = seg_t(1)
        zt = seg_t(2)
        vb_ref[0] = zt
        vbb_ref[0] = zt.astype(BF16)
    else:
        kb_ref[...] = z
        z = seg(5)
        vb_ref[...] = z
        vbb_ref[...] = z.astype(BF16)
    fzt = lax.dot_general(wft_ref[...], h, _NT, preferred_element_type=F32)
    lft_ref[0] = _log_sigmoid(fzt[:H_B] + bft_ref[...])


def _proj_even(x, g, w_main, w_t, wft, bf_col, *, tiles_per_batch=None):
    m = x.shape[0]
    tm = min(ROW_TILE, m)
    n = m // tm
    row = lambda w: pl.BlockSpec((tm, w), lambda i: (i, 0))
    bfo = jax.ShapeDtypeStruct((m, A_W), BF16)
    tok_head = jax.ShapeDtypeStruct((m * H_A, LANES), F32)
    tok_head_spec = pl.BlockSpec((tm * H_A, LANES), lambda i: (i, 0))
    if tiles_per_batch:
        tpb = tiles_per_batch
        fm = lambda dt: jax.ShapeDtypeStruct((n // tpb, A_W, tpb * tm), dt)
        fm_spec = pl.BlockSpec((1, A_W, tm), lambda i: (i // tpb, 0, i % tpb))
        outs = [(bfo, row(A_W)), (tok_head, tok_head_spec), (bfo, row(A_W)), (tok_head, tok_head_spec),
                (fm(BF16), fm_spec), (bfo, row(A_W)), (fm(F32), fm_spec), (bfo, row(A_W)),
                (fm(F32), fm_spec), (fm(BF16), fm_spec)]
    else:
        f32o = jax.ShapeDtypeStruct((m, A_W), F32)
        outs = [(bfo, row(A_W)), (tok_head, tok_head_spec), (bfo, row(A_W)), (tok_head, tok_head_spec),
                (bfo, row(A_W)), (bfo, row(A_W)), (f32o, row(A_W)), (bfo, row(A_W)),
                (f32o, row(A_W)), (bfo, row(A_W))]
    outs.append((jax.ShapeDtypeStruct((n, H_B, tm), F32), pl.BlockSpec((1, H_B, tm), lambda i: (i, 0, 0))))
    return pl.pallas_call(
        functools.partial(_proj_even_kernel, feature_major=bool(tiles_per_batch)),
        grid=(n,),
        in_specs=[row(D_MODEL), _const_spec((1, D_MODEL)), _const_spec(w_main.shape),
                  _const_spec(w_t.shape), _const_spec(wft.shape), _const_spec(bf_col.shape)],
        out_specs=[s for _, s in outs],
        out_shape=[o for o, _ in outs],
        compiler_params=_params(1),
        name="proj_even",
    )(x, g, w_main, w_t, wft, bf_col)


def _cumsum_kernel(x_ref, o_ref):
    x = x_ref[0]
    t = x.shape[1]
    col = lax.broadcasted_iota(jnp.int32, x.shape, 1)
    s = 1
    while s < t:
        x = x + jnp.where(col >= s, pltpu.roll(x, s, 1), 0.0)
        s *= 2
    o_ref[0] = x


def _cumsum_time(x):
    b, h, t = x.shape
    spec = pl.BlockSpec((1, h, t), lambda i: (i, 0, 0))
    return pl.pallas_call(
        _cumsum_kernel, grid=(b,), in_specs=[spec], out_specs=spec,
        out_shape=jax.ShapeDtypeStruct(x.shape, F32),
        compiler_params=_params(1), name="cumsum_time",
    )(x)


def _stack_pair(q):
    q32 = q.astype(F32)
    lo = lax.broadcasted_iota(jnp.int32, q32.shape, 1) < HEAD_DIM
    return jnp.concatenate([jnp.where(lo, q32, 0.0), jnp.where(lo, 0.0, q32)], axis=0).astype(BF16)


def _online_update(a, u, c, vb, m_sc, l_sc, acc_sc):
    m_prev = m_sc[a]
    m_next = jnp.maximum(m_prev, jnp.max(u, axis=1, keepdims=True) + c)
    p = jnp.exp(u - jnp.tile(m_next - c, (1, u.shape[1] // LANES)))
    alpha = jnp.exp(m_prev - m_next)
    l_sc[a] = alpha * l_sc[a] + jnp.sum(p, axis=1, keepdims=True)
    acc_sc[a] = alpha * acc_sc[a] + jnp.dot(p.astype(BF16), vb, preferred_element_type=F32)
    m_sc[a] = m_next


def _flash_kernel(*refs, mode, tq, tk, q_pos0, diag_off, has_interior, lam_init):
    if mode == "diff":
        slopes_ref, lamv_ref, g_ref, q_ref, k_ref, v_ref, o_ref, m_sc, l_sc, acc_sc, bint_sc, bdiag_sc = refs
    else:
        cum_ref, q_ref, k_ref, v_ref, o_ref, m_sc, l_sc, acc_sc, bdiag_sc = refs
    pair = pl.program_id(0)
    q0 = q_pos0 + pl.program_id(2) * tq
    n_int = q0 // tk

    @pl.when((pl.program_id(1) == 0) & (pl.program_id(2) == 0))
    def _():
        row = lax.broadcasted_iota(jnp.int32, (tq, tk), 0)
        col = lax.broadcasted_iota(jnp.int32, (tq, tk), 1)
        if mode == "diff":
            neg_slope = -slopes_ref[pair]
            if has_interior:
                bint_sc[...] = neg_slope * (row - col).astype(F32)
            visible = (col >> 6) <= ((row + diag_off) >> 6)
            dist = jnp.abs(row + diag_off - col).astype(F32)
            bdiag_sc[...] = jnp.where(visible, neg_slope * dist, NEG_INF)
        else:
            bdiag_sc[...] = jnp.where(col <= row + diag_off, 0.0, NEG_INF)

    q2 = _stack_pair(q_ref[0])
    m_sc[...] = jnp.full(m_sc.shape, NEG_INF, F32)
    l_sc[...] = jnp.zeros(l_sc.shape, F32)
    acc_sc[...] = jnp.zeros(acc_sc.shape, F32)

    def scores(k0):
        kb = k_ref[0, pl.ds(k0, tk), :]
        vb = v_ref[0, pl.ds(k0, tk), :]
        return lax.dot_general(q2, kb, _NT, preferred_element_type=F32), vb

    def cum_row(a, k0):
        return cum_ref[0, 2 * pair + a, :, pl.ds(k0, tk)]

    def interior(j, carry):
        k0 = pl.multiple_of(j * tk, tk)
        s, vb = scores(k0)
        for a in range(2):
            sa = s[a * tq:(a + 1) * tq]
            if mode == "diff":
                _online_update(a, sa + bint_sc[...], -slopes_ref[pair] * (q0 - k0).astype(F32),
                               vb, m_sc, l_sc, acc_sc)
            else:
                _online_update(a, sa - cum_row(a, k0), 0.0, vb, m_sc, l_sc, acc_sc)
        return carry

    if has_interior:
        lax.fori_loop(0, n_int, interior, 0)

    k0 = pl.multiple_of(n_int * tk, tk)
    s, vb = scores(k0)
    for a in range(2):
        sa = s[a * tq:(a + 1) * tq]
        if mode == "diff":
            u = sa + bdiag_sc[...]
        else:
            u = (sa - cum_row(a, k0)) + bdiag_sc[...]
        _online_update(a, u, 0.0, vb, m_sc, l_sc, acc_sc)

    o0 = acc_sc[0] / l_sc[0]
    o1 = acc_sc[1] / l_sc[1]
    if mode == "diff":
        lv = lamv_ref[...]
        lam = (jnp.exp(jnp.sum(lv[0:1] * lv[1:2], axis=1, keepdims=True))
               - jnp.exp(jnp.sum(lv[2:3] * lv[3:4], axis=1, keepdims=True)) + lam_init)
        oa = o0 - lam * o1
        o_ref[0] = (_rms(oa, g_ref[...]) * (1.0 - lam_init)).astype(BF16)
    else:
        lo = lax.broadcasted_iota(jnp.int32, o0.shape, 1) < HEAD_DIM
        o_ref[0] = jnp.where(lo, o0, o1).astype(BF16)


def _flash(mode, q, k, v, extras, *, tq, tk, q_pos0, lam_init=0.0):
    b, t_q, width = q.shape
    t_k = k.shape[1]
    pairs = width // LANES
    n_q = t_q // tq
    has_interior = t_k > tk
    diag_off = q_pos0 % tk
    assert all((q_pos0 + i * tq) % tk == diag_off for i in range(n_q)) and diag_off + tq <= tk
    assert t_k % tk == 0 and (q_pos0 + (n_q - 1) * tq) // tk < t_k // tk
    grid = (pairs, b, n_q)
    qspec = pl.BlockSpec((1, tq, LANES), lambda p, bi, i: (bi, i, p))
    kvspec = pl.BlockSpec((1, t_k, LANES), lambda p, bi, i: (bi, 0, p))
    bias_scratch = [pltpu.VMEM((tq, tk), F32)]
    if mode == "diff":
        slopes, lamv, g = extras
        extra_specs = [pl.BlockSpec(memory_space=pltpu.SMEM),
                       pl.BlockSpec(lamv.shape, lambda p, bi, i: (0, 0)),
                       pl.BlockSpec(g.shape, lambda p, bi, i: (0, 0))]
        bias_scratch = [pltpu.VMEM((tq, tk) if has_interior else (8, LANES), F32)] + bias_scratch
    else:
        extras = (extras[0][:, :, None, :],)
        extra_specs = [pl.BlockSpec((1,) + extras[0].shape[1:], lambda p, bi, i: (bi, 0, 0, 0))]
    return pl.pallas_call(
        functools.partial(_flash_kernel, mode=mode, tq=tq, tk=tk, q_pos0=q_pos0, diag_off=diag_off,
                          has_interior=has_interior, lam_init=lam_init),
        grid=grid,
        in_specs=extra_specs + [qspec, kvspec, kvspec],
        out_specs=qspec,
        out_shape=jax.ShapeDtypeStruct(q.shape, BF16),
        scratch_shapes=[pltpu.VMEM((2, tq, LANES), F32)] * 3 + bias_scratch,
        compiler_params=_params(3),
        name="flash_" + mode,
    )(*extras, q, k, v)


ONES_ROWS = 16


def _with_ones(vt):
    return jnp.concatenate([vt, jnp.ones((ONES_ROWS, vt.shape[1]), BF16)], axis=0)


AUG_TERMS = 3


def _lane_terms(lane, first, terms):
    out = jnp.zeros(terms[0].shape, F32)
    for t, v in enumerate(terms):
        out = jnp.where(lane == first + t, v, out)
    return out


def _flash_t_kernel(*refs, mode, tq, lam_init):
    tk = tq
    if mode == "diff":
        (slopes_ref, lamv_ref, g_ref, q_ref, k_ref, vt_ref, o_ref,
         m_sc, acc_sc, q2_sc, s0_sc, s1_sc, mu_sc, kaug_sc, bdiag_sc) = refs
    else:
        cum_ref, q_ref, k_ref, vt_ref, o_ref, m_sc, acc_sc, q2_sc, s0_sc, s1_sc, mu_sc, kaug_sc, bdiag_sc = refs
    pair = pl.program_id(0)
    qi = pl.program_id(2)
    t_all = kaug_sc.shape[0]
    s_bufs = (s0_sc, s1_sc)

    @pl.when((pl.program_id(1) == 0) & (qi == 0))
    def _():
        row = lax.broadcasted_iota(jnp.int32, (tk, tq), 0)
        col = lax.broadcasted_iota(jnp.int32, (tk, tq), 1)
        lane_q = lax.broadcasted_iota(jnp.int32, (2 * tq, LANES), 1)
        first_q = jnp.where(lax.broadcasted_iota(jnp.int32, (2 * tq, LANES), 0) < tq, 0, AUG_TERMS)
        if mode == "diff":
            slope = slopes_ref[pair]
            visible = (row >> 6) <= (col >> 6)
            bdiag_sc[...] = jnp.where(visible, -2.0 * slope * jnp.maximum(row - col, 0).astype(F32), NEG_INF)
            q2_sc[:, LANES:] = jnp.where(lane_q < AUG_TERMS, 1.0, 0.0).astype(BF16)
            for c0 in range(0, t_all, tk):
                j = c0 + lax.broadcasted_iota(jnp.int32, (tk, LANES), 0)
                lane = lax.broadcasted_iota(jnp.int32, (tk, LANES), 1)
                terms = [((j >> 8) << 8).astype(F32), (((j >> 4) & 15) << 4).astype(F32), (j & 15).astype(F32)]
                kaug_sc[c0:c0 + tk, :] = (slope * _lane_terms(lane, 0, terms)).astype(BF16)
        else:
            bdiag_sc[...] = jnp.where(row <= col, 0.0, NEG_INF)
            mine = (lane_q >= first_q) & (lane_q < first_q + AUG_TERMS)
            q2_sc[:, LANES:] = jnp.where(mine, -1.0, 0.0).astype(BF16)

    if mode == "fox":
        @pl.when(qi == 0)
        def _():
            for c0 in range(0, t_all, tk):
                lane = lax.broadcasted_iota(jnp.int32, (tk, LANES), 1)
                feat = jnp.zeros((tk, LANES), F32)
                for a in range(2):
                    r = cum_ref[0, 2 * pair + a, :, c0:c0 + tk]
                    ck = jnp.broadcast_to(r, (LANES, tk)).T
                    hi = ck.astype(BF16).astype(F32)
                    mid = (ck - hi).astype(BF16).astype(F32)
                    lo = ((ck - hi) - mid).astype(BF16).astype(F32)
                    feat = feat + _lane_terms(lane, AUG_TERMS * a, [hi, mid, lo])
                kaug_sc[c0:c0 + tk, :] = feat.astype(BF16)

    q2_sc[:, :LANES] = _stack_pair(q_ref[0])
    m_sc[...] = jnp.full(m_sc.shape, NEG_INF, F32)
    acc_sc[...] = jnp.zeros(acc_sc.shape, F32)

    def stage_a(k0, slot, diag):
        kk = jnp.concatenate([k_ref[0, pl.ds(k0, tk), :], kaug_sc[pl.ds(k0, tk), :]], axis=1)
        st = lax.dot_general(kk, q2_sc[...], _NT, preferred_element_type=F32)
        for a in range(2):
            sa = st[:, a * tq:(a + 1) * tq]
            if diag:
                sa = sa + bdiag_sc[...]
            s_bufs[slot][:, a * tq:(a + 1) * tq] = sa
            mu_sc[slot, :, a * tq:(a + 1) * tq] = jnp.max(sa, axis=0, keepdims=True)

    def stage_b(k0, slot):
        vta = _with_ones(vt_ref[0, :, pl.ds(k0, tk)])
        for a in range(2):
            m_prev = m_sc[a]
            m_next = jnp.maximum(m_prev, mu_sc[slot, :, a * tq:(a + 1) * tq])
            p = jnp.exp(s_bufs[slot][:, a * tq:(a + 1) * tq] - m_next)
            alpha = jnp.exp(m_prev - m_next)
            acc_sc[a] = alpha * acc_sc[a] + jnp.dot(vta, p.astype(BF16), preferred_element_type=F32)
            m_sc[a] = m_next

    blk = lambda j: pl.multiple_of(j * tk, tk)
    pl.when(qi == 0)(lambda: stage_a(0, 0, True))
    pl.when(qi > 0)(lambda: stage_a(0, 0, False))

    def two_blocks(i, carry):
        stage_a(blk(2 * i + 1), 1, False)
        stage_b(blk(2 * i), 0)
        stage_a(blk(2 * i + 2), 0, False)
        stage_b(blk(2 * i + 1), 1)
        return carry

    lax.fori_loop(0, (qi - 1) // 2, two_blocks, 0)

    @pl.when(qi == 0)
    def _():
        stage_b(0, 0)

    @pl.when(qi % 2 == 1)
    def _():
        stage_a(blk(qi), 1, True)
        stage_b(blk(qi - 1), 0)
        stage_b(blk(qi), 1)

    @pl.when((qi % 2 == 0) & (qi > 0))
    def _():
        stage_a(blk(qi - 1), 1, False)
        stage_b(blk(qi - 2), 0)
        stage_a(blk(qi), 0, True)
        stage_b(blk(qi - 1), 1)
        stage_b(blk(qi), 0)

    d = 2 * HEAD_DIM
    o0 = acc_sc[0, :d, :] / acc_sc[0, d:d + 1, :]
    o1 = acc_sc[1, :d, :] / acc_sc[1, d:d + 1, :]
    if mode == "diff":
        lv = lamv_ref[...]
        lam = (jnp.exp(jnp.sum(lv[0:1] * lv[1:2], axis=1, keepdims=True))
               - jnp.exp(jnp.sum(lv[2:3] * lv[3:4], axis=1, keepdims=True)) + lam_init)
        oa = o0 - lam * o1
        ms = jnp.mean(oa * oa, axis=0, keepdims=True)
        y = (oa * lax.rsqrt(ms + NORM_EPS)).T * g_ref[...]
        o_ref[0] = (y * (1.0 - lam_init)).astype(BF16)
    else:
        lo = lax.broadcasted_iota(jnp.int32, o0.shape, 0) < HEAD_DIM
        o_ref[0] = jnp.where(lo, o0, o1).T.astype(BF16)


def _flash_t(mode, q, k, vt, extras, *, tq, lam_init=0.0):
    b, t, width = q.shape
    pairs = width // LANES
    qspec = pl.BlockSpec((1, tq, LANES), lambda p, bi, i: (bi, i, p))
    kspec = pl.BlockSpec((1, t, LANES), lambda p, bi, i: (bi, 0, p))
    vspec = pl.BlockSpec((1, LANES, t), lambda p, bi, i: (bi, p, 0))
    scratch = [pltpu.VMEM((2, 1, tq), F32),
               pltpu.VMEM((2, LANES + ONES_ROWS, tq), F32),
               pltpu.VMEM((2 * tq, 2 * LANES), BF16),
               pltpu.VMEM((tq, 2 * tq), F32), pltpu.VMEM((tq, 2 * tq), F32),
               pltpu.VMEM((2, 1, 2 * tq), F32),
               pltpu.VMEM((t, LANES), BF16),
               pltpu.VMEM((tq, tq), F32)]
    if mode == "diff":
        slopes, lamv, g = extras
        extra_specs = [pl.BlockSpec(memory_space=pltpu.SMEM),
                       pl.BlockSpec(lamv.shape, lambda p, bi, i: (0, 0)),
                       pl.BlockSpec(g.shape, lambda p, bi, i: (0, 0))]
    else:
        extras = (extras[0][:, :, None, :],)
        extra_specs = [pl.BlockSpec((1,) + extras[0].shape[1:], lambda p, bi, i: (bi, 0, 0, 0))]
    return pl.pallas_call(
        functools.partial(_flash_t_kernel, mode=mode, tq=tq, lam_init=lam_init),
        grid=(pairs, b, t // tq),
        in_specs=extra_specs + [qspec, kspec, vspec],
        out_specs=qspec,
        out_shape=jax.ShapeDtypeStruct(q.shape, BF16),
        scratch_shapes=scratch,
        compiler_params=_params(3),
        name="flash_t_" + mode,
    )(*extras, q, k, vt)


def _band_kernel(c_ref, q_ref, k_ref, v_ref, o_ref, bias_sc, *, tq, window, q_pos_base, windowed, n_edge):
    qi = pl.program_id(2)

    @pl.when((pl.program_id(1) == 0) & (qi == 0))
    def _():
        row = lax.broadcasted_iota(jnp.int32, (tq, window), 0)
        col = lax.broadcasted_iota(jnp.int32, (tq, window), 1)
        visible = ((col >> 6) >= (row >> 6)) & ((col >> 6) <= (row >> 6) + BAND_PAST // CHUNK)
        for a in range(2):
            c = jnp.broadcast_to(c_ref[0, a:a + 1, :], (tq, c_ref.shape[2]))
            bias = pltpu.roll(c, 0, 1, stride=1, stride_axis=0)[:, :window]
            bias_sc[a] = jnp.where(visible, bias, NEG_INF)

    def attend(edge):
        w0 = pl.multiple_of(qi * tq, tq) if windowed else 0
        kb = k_ref[0, pl.ds(w0, window), :]
        vb = v_ref[0, pl.ds(w0, window), :]
        s = lax.dot_general(_stack_pair(q_ref[0]), kb, _NT, preferred_element_type=F32)
        outs = []
        for a in range(2):
            t = s[a * tq:(a + 1) * tq] + bias_sc[a]
            if edge:
                k_pos = q_pos_base + qi * tq - BAND_PAST + lax.broadcasted_iota(jnp.int32, (1, window), 1)
                t = jnp.where(k_pos >= 0, t, NEG_INF)
            p = jnp.exp(t - jnp.max(t, axis=1, keepdims=True))
            l = jnp.sum(p, axis=1, keepdims=True)
            outs.append(jnp.dot(p.astype(BF16), vb, preferred_element_type=F32) / l)
        lo = lax.broadcasted_iota(jnp.int32, outs[0].shape, 1) < HEAD_DIM
        o_ref[0] = jnp.where(lo, outs[0], outs[1]).astype(BF16)

    if n_edge:
        pl.when(qi < n_edge)(lambda: attend(True))
        pl.when(qi >= n_edge)(lambda: attend(False))
    else:
        attend(False)


def _band(q, k, v, cvec, *, tq, window, q_pos_base, windowed):
    b, t_q, width = q.shape
    t_k = k.shape[1]
    pairs = width // LANES
    n_edge = max(0, -(-(BAND_PAST - q_pos_base) // tq))
    qspec = pl.BlockSpec((1, tq, LANES), lambda p, bi, i: (bi, i, p))
    kvspec = pl.BlockSpec((1, t_k, LANES), lambda p, bi, i: (bi, 0, p))
    cspec = pl.BlockSpec((1, 2, cvec.shape[2]), lambda p, bi, i: (p, 0, 0))
    return pl.pallas_call(
        functools.partial(_band_kernel, tq=tq, window=window, q_pos_base=q_pos_base, windowed=windowed,
                          n_edge=n_edge),
        grid=(pairs, b, t_q // tq),
        in_specs=[cspec, qspec, kvspec, kvspec],
        out_specs=qspec,
        out_shape=jax.ShapeDtypeStruct(q.shape, BF16),
        scratch_shapes=[pltpu.VMEM((2, tq, window), F32)],
        compiler_params=_params(3),
        name="band_attn",
    )(cvec, q, k, v)


def _band_t_kernel(c_ref, q_ref, k_ref, vt_ref, o_ref, s0_sc, s1_sc, mu_sc, bias_sc, *, tq, t):
    window = BAND_PAST + tq
    n_q = t // tq
    s_bufs = (s0_sc, s1_sc)

    @pl.when(pl.program_id(1) == 0)
    def _():
        row = lax.broadcasted_iota(jnp.int32, (tq, window), 0)
        col = lax.broadcasted_iota(jnp.int32, (tq, window), 1)
        visible = ((col >> 6) >= (row >> 6)) & ((col >> 6) <= (row >> 6) + BAND_PAST // CHUNK)
        for a in range(2):
            c = jnp.broadcast_to(c_ref[0, a:a + 1, :], (tq, c_ref.shape[2]))
            bias = pltpu.roll(c, 0, 1, stride=1, stride_axis=0)[:, :window]
            bias_sc[a] = jnp.where(visible, bias, NEG_INF).T

    lane_q = lax.broadcasted_iota(jnp.int32, (2 * tq, LANES), 1)
    q_extra = jnp.where(lane_q == 0, 1.0, 0.0).astype(BF16)

    def stage_a(qb, slot):
        w0 = pl.multiple_of(qb * tq, tq)
        q2 = jnp.concatenate([_stack_pair(q_ref[0, pl.ds(w0, tq), :]), q_extra], axis=1)
        pad_row = (w0 + lax.broadcasted_iota(jnp.int32, (window, LANES), 0)) < BAND_PAST
        lane_k = lax.broadcasted_iota(jnp.int32, (window, LANES), 1)
        k_extra = jnp.where(pad_row & (lane_k == 0), NEG_INF, 0.0).astype(BF16)
        kk = jnp.concatenate([k_ref[0, pl.ds(w0, window), :], k_extra], axis=1)
        st = lax.dot_general(kk, q2, _NT, preferred_element_type=F32)
        for a in range(2):
            u = st[:, a * tq:(a + 1) * tq] + bias_sc[a]
            s_bufs[slot][:, a * tq:(a + 1) * tq] = u
            mu_sc[slot, :, a * tq:(a + 1) * tq] = jnp.max(u, axis=0, keepdims=True)

    def stage_b(qb, slot):
        w0 = pl.multiple_of(qb * tq, tq)
        vta = _with_ones(vt_ref[0, :, pl.ds(w0, window)])
        outs = []
        for a in range(2):
            p = jnp.exp(s_bufs[slot][:, a * tq:(a + 1) * tq] - mu_sc[slot, :, a * tq:(a + 1) * tq])
            r = jnp.dot(vta, p.astype(BF16), preferred_element_type=F32)
            outs.append(r[:LANES] / r[LANES:LANES + 1])
        lo = lax.broadcasted_iota(jnp.int32, outs[0].shape, 0) < HEAD_DIM
        o_ref[0, pl.ds(w0, tq), :] = jnp.where(lo, outs[0], outs[1]).T.astype(BF16)

    stage_a(0, 0)

    def two_blocks(i, carry):
        stage_a(2 * i + 1, 1)
        stage_b(2 * i, 0)
        stage_a(2 * i + 2, 0)
        stage_b(2 * i + 1, 1)
        return carry

    lax.fori_loop(0, n_q // 2 - 1, two_blocks, 0)
    stage_a(n_q - 1, 1)
    stage_b(n_q - 2, 0)
    stage_b(n_q - 1, 1)


def _band_t(q, k_pad, vt_pad, cvec, *, tq):
    b, t, width = q.shape
    pairs = width // LANES
    window = BAND_PAST + tq
    assert (t // tq) % 2 == 0 and t // tq >= 2
    qspec = pl.BlockSpec((1, t, LANES), lambda p, bi: (bi, 0, p))
    kspec = pl.BlockSpec((1, k_pad.shape[1], LANES), lambda p, bi: (bi, 0, p))
    vspec = pl.BlockSpec((1, LANES, vt_pad.shape[2]), lambda p, bi: (bi, p, 0))
    cspec = pl.BlockSpec((1, 2, cvec.shape[2]), lambda p, bi: (p, 0, 0))
    return pl.pallas_call(
        functools.partial(_band_t_kernel, tq=tq, t=t),
        grid=(pairs, b),
        in_specs=[cspec, qspec, kspec, vspec],
        out_specs=qspec,
        out_shape=jax.ShapeDtypeStruct(q.shape, BF16),
        scratch_shapes=[pltpu.VMEM((window, 2 * tq), F32), pltpu.VMEM((window, 2 * tq), F32),
                        pltpu.VMEM((2, 1, 2 * tq), F32), pltpu.VMEM((2, window, tq), F32)],
        compiler_params=_params(2),
        name="band_t_attn",
    )(cvec, q, k_pad, vt_pad)


def _proj_odd_kernel(x_ref, g_ref, w_ref, wt_ref, q_ref, k_ref, v_ref, kt_ref, vt_ref, *,
                     tail_every, v_feature_major):
    h = _rms(x_ref[...], g_ref[...]).astype(BF16)

    def seg(i):
        return jnp.dot(h, w_ref[:, i * D_MODEL:(i + 1) * D_MODEL], preferred_element_type=F32)

    def seg_t(i):
        return lax.dot_general(wt_ref[i], h, _NT, preferred_element_type=F32)

    is_tail = pl.program_id(0) % tail_every == tail_every - 1
    q_ref[...] = (seg(0) * QK_SCALE).astype(BF16)
    k_ref[...] = seg(1).astype(BF16)
    if v_feature_major:
        vzt = seg_t(1)
        v_ref[0] = vzt.astype(BF16)

        @pl.when(is_tail)
        def _():
            kt_ref[0] = seg_t(0)
            vt_ref[0] = vzt
    else:
        v_ref[...] = seg(2).astype(BF16)

        @pl.when(is_tail)
        def _():
            kt_ref[0] = seg_t(0)
            vt_ref[0] = seg_t(1)


def _proj_odd(x, g, w, w_t, *, tail_every, v_feature_major):
    m = x.shape[0]
    tm = min(ROW_TILE, m)
    n = m // tm
    row = pl.BlockSpec((tm, D_MODEL), lambda i: (i, 0))
    tail = pl.BlockSpec((1, D_MODEL, tm), lambda i: (i // tail_every, 0, 0))
    bfo = jax.ShapeDtypeStruct((m, D_MODEL), BF16)
    tailo = jax.ShapeDtypeStruct((n // tail_every, D_MODEL, tm), F32)
    w_vt = w_t
    if v_feature_major:
        vo = jax.ShapeDtypeStruct((n // tail_every, D_MODEL, tail_every * tm), BF16)
        vspec = pl.BlockSpec((1, D_MODEL, tm), lambda i: (i // tail_every, 0, i % tail_every))
    else:
        vo, vspec = bfo, row
    return pl.pallas_call(
        functools.partial(_proj_odd_kernel, tail_every=tail_every, v_feature_major=v_feature_major),
        grid=(n,),
        in_specs=[row, _const_spec((1, D_MODEL)), _const_spec(w.shape), _const_spec(w_vt.shape)],
        out_specs=[row, row, vspec, tail, tail],
        out_shape=[bfo, bfo, vo, tailo, tailo],
        compiler_params=_params(1),
        name="proj_odd",
    )(x, g, w, w_vt)


def _post_kernel(*refs, n_o):
    o_refs = refs[:n_o]
    x_ref, wo_ref, gpm_ref, gpre_ref, wup_ref, wdn_ref, gpf_ref, out_ref = refs[n_o:]
    o = o_refs[0][...] if n_o == 1 else jnp.concatenate([r[...] for r in o_refs], axis=1)
    mixed = jnp.dot(o, wo_ref[...], preferred_element_type=F32)
    x1 = x_ref[...] + _rms(mixed, gpm_ref[...])
    h = _rms(x1, gpre_ref[...]).astype(BF16)
    acc = jnp.zeros(x1.shape, F32)
    for c in range(D_FF // FF_CHUNK):
        u = jnp.dot(h, wup_ref[:, c * FF_CHUNK:(c + 1) * FF_CHUNK], preferred_element_type=F32)
        u = jnp.maximum(u, 0.0)
        acc = acc + jnp.dot((u * u).astype(BF16), wdn_ref[c * FF_CHUNK:(c + 1) * FF_CHUNK, :],
                            preferred_element_type=F32)
    out_ref[...] = x1 + _rms(acc, gpf_ref[...])


def _post(o_list, x, wo, gpm, gpre, wup, wdn, gpf):
    m = x.shape[0]
    tm = min(ROW_TILE, m)
    row = lambda w: pl.BlockSpec((tm, w), lambda i: (i, 0))
    gspec = _const_spec((1, D_MODEL))
    return pl.pallas_call(
        functools.partial(_post_kernel, n_o=len(o_list)),
        grid=(m // tm,),
        in_specs=[row(o.shape[1]) for o in o_list]
                 + [row(D_MODEL), _const_spec(wo.shape), gspec, gspec,
                    _const_spec(wup.shape), _const_spec(wdn.shape), gspec],
        out_specs=row(D_MODEL),
        out_shape=jax.ShapeDtypeStruct((m, D_MODEL), F32),
        compiler_params=_params(1),
        name="post_mix_mlp",
    )(*o_list, x, wo, gpm, gpre, wup, wdn, gpf)


def _band_bias_vectors(table):
    n = 2 * BAND_PAST
    m = np.arange(n)
    d = np.where(m <= n - REL_CLIP, -m, n - m)
    idx = np.clip(d + BAND_PAST, -REL_CLIP, REL_CLIP) + REL_CLIP
    return jnp.take(table.astype(F32), jnp.asarray(idx, jnp.int32), axis=1).reshape(H_C // 2, 2, n)


def _pad_time(x, front, back):
    return jnp.pad(x, ((0, 0), (front, back), (0, 0)))


def kernel(x_prompt, x_sample, cache_a_k, cache_a_v, cache_b_k, cache_b_v, cache_b_logf, cache_c_k, cache_c_v,
           w_in_even, b_forget, lam_q1, lam_k1, lam_q2, lam_k2, subln_g, w_out_even, w_in_odd, rel_bias,
           w_out_odd, g_pre_mix, g_post_mix, g_pre_ffn, g_post_ffn, w_ffn_up, w_ffn_down):
    b_p, t_p, _ = x_prompt.shape
    b_s, t_s, _ = x_sample.shape
    past = cache_b_logf.shape[2]
    m_p, m_s = b_p * t_p, b_s * t_s
    xp = x_prompt.reshape(m_p, D_MODEL)
    xs = x_sample.reshape(m_s, D_MODEL)
    gvec = lambda a, l: a[l].reshape(1, D_MODEL)

    lam_init = 0.8 - 0.6 * math.exp(-0.3 * 0)
    w_in = w_in_even[0]
    w_main = w_in[:, :N_SEG * A_W].astype(BF16)
    w_f = w_in[:, N_SEG * A_W:]
    wft = jnp.pad(w_f.T, ((0, 16 - H_B), (0, 0))).astype(BF16)
    bf_col = b_forget[0].reshape(H_B, 1)
    slopes = 2.0 ** (-8.0 * jnp.arange(1, H_A + 1, dtype=F32) / H_A)
    lamv = jnp.stack([lam_q1[0], lam_k1[0], lam_q2[0], lam_k2[0]]).astype(F32)
    sub_g = subln_g[0].reshape(1, 2 * HEAD_DIM)
    wo0 = w_out_even[0].astype(BF16)
    wup0, wdn0 = w_ffn_up[0].astype(BF16), w_ffn_down[0].astype(BF16)

    w_t = jnp.stack([w_in[:, s * A_W:(s + 1) * A_W].T for s in (2, 4, 5)]).astype(BF16)

    def even_layer(x, bsz, t, caches):
        m = x.shape[0]
        tm = min(ROW_TILE, m)
        (qa, ka, kab, va, vab, qb, kb, kbb, vb, vbb, lft) = _proj_even(
            x, gvec(g_pre_mix, 0), w_main, w_t, wft, bf_col,
            tiles_per_batch=(t // tm) if caches is None else None)
        r3 = lambda a: a.reshape(bsz, t, A_W)
        lft = lft.transpose(1, 0, 2).reshape(H_B, bsz, t).transpose(1, 0, 2)
        tok_head = lambda a: a.reshape(1, bsz, t, H_A, 2 * HEAD_DIM)
        if caches is None:
            oa = _flash_t("diff", r3(qa), r3(kab), vab, (slopes, lamv, sub_g), tq=512, lam_init=lam_init)
            ob = _flash_t("fox", r3(qb), r3(kbb), vbb, (_cumsum_time(lft),), tq=512)
            time_minor = lambda a: a.reshape(bsz, H_B, HEAD_DIM, t).transpose(0, 3, 1, 2)[None]
            new = (tok_head(ka), tok_head(va), time_minor(kb), time_minor(vb), lft.transpose(0, 2, 1)[None])
        else:
            c_ka, c_va, c_kb, c_vb, c_lf = caches
            p_len = c_ka.shape[1]
            t_all = p_len + t
            t_pad = -(-t_all // LANES) * LANES
            cat = lambda c, new: _pad_time(
                jnp.concatenate([c.reshape(bsz, p_len, A_W).astype(BF16), r3(new)], axis=1), 0, t_pad - t_all)
            kab3, vab3, kbb3, vbb3 = cat(c_ka, kab), cat(c_va, vab), cat(c_kb, kbb), cat(c_vb, vbb)
            lf_all = jnp.concatenate([c_lf.astype(F32).transpose(0, 2, 1), lft], axis=2)
            cum = _cumsum_time(jnp.pad(lf_all, ((0, 0), (0, 0), (0, t_pad - t_all))))
            oa = _flash("diff", r3(qa), kab3, vab3, (slopes, lamv, sub_g), tq=t, tk=t_pad, q_pos0=p_len,
                        lam_init=lam_init)
            ob = _flash("fox", r3(qb), kbb3, vbb3, (cum,), tq=t, tk=t_pad, q_pos0=p_len)
            new = (tok_head(ka), tok_head(va), kb.reshape(1, bsz, t, H_B, HEAD_DIM),
                   vb.reshape(1, bsz, t, H_B, HEAD_DIM), lft.transpose(0, 2, 1)[None])
        x_out = _post([oa.reshape(m, A_W), ob.reshape(m, A_W)], x, wo0, gvec(g_post_mix, 0),
                      gvec(g_pre_ffn, 0), wup0, wdn0, gvec(g_post_ffn, 0))
        return x_out, new

    xp, new_p_even = even_layer(xp, b_p, t_p, None)
    xs, new_s_even = even_layer(xs, b_s, t_s, (cache_a_k[0], cache_a_v[0], cache_b_k[0], cache_b_v[0],
                                              cache_b_logf[0]))

    w_odd = w_in_odd[0].astype(BF16)
    w_odd_t = jnp.stack([w_in_odd[0][:, s * D_MODEL:(s + 1) * D_MODEL].T for s in (1, 2)]).astype(BF16)
    cvec = _band_bias_vectors(rel_bias[0])
    to_cache = lambda a: a.reshape(a.shape[0], H_C, HEAD_DIM, a.shape[2]).transpose(0, 3, 1, 2)[None]
    from_cache = lambda c: c.transpose(0, 2, 3, 1).reshape(c.shape[0], D_MODEL, c.shape[1])
    wo1 = w_out_odd[0].astype(BF16)
    wup1, wdn1 = w_ffn_up[1].astype(BF16), w_ffn_down[1].astype(BF16)

    def odd_layer(x, bsz, t, caches):
        m = x.shape[0]
        tm = min(ROW_TILE, m)
        tail_every = (t // tm) if caches is None else 1
        q, k, v, k_tail, v_tail = _proj_odd(x, gvec(g_pre_mix, 1), w_odd, w_odd_t, tail_every=tail_every,
                                            v_feature_major=caches is None)
        r3 = lambda a: a.reshape(bsz, t, D_MODEL)
        if caches is None:
            kp = _pad_time(r3(k), BAND_PAST, 0)
            vtp = jnp.pad(v, ((0, 0), (0, 0), (BAND_PAST, 0)))
            o = _band_t(r3(q), kp, vtp, cvec, tq=256)
            assert tm == min(BAND_PAST, t)
            new = (to_cache(k_tail), to_cache(v_tail))
        else:
            c_k, c_v = caches
            w_len = c_k.shape[1]
            window = -(-(w_len + t) // LANES) * LANES
            cat = lambda c, new_: _pad_time(
                jnp.concatenate([c.reshape(bsz, w_len, D_MODEL).astype(BF16), r3(new_)], axis=1),
                0, window - w_len - t)
            o = _band(r3(q), cat(c_k, k), cat(c_v, v), cvec, tq=t, window=window, q_pos_base=past,
                      windowed=False)
            upd = lambda c, tail: to_cache(jnp.concatenate(
                [from_cache(c)[:, :, t:], tail.reshape(D_MODEL, bsz, t).transpose(1, 0, 2)], axis=2))
            new = (upd(c_k, k_tail), upd(c_v, v_tail))
        x_out = _post([o.reshape(m, D_MODEL)], x, wo1, gvec(g_post_mix, 1), gvec(g_pre_ffn, 1),
                      wup1, wdn1, gvec(g_post_ffn, 1))
        return x_out, new

    xp, new_p_odd = odd_layer(xp, b_p, t_p, None)
    xs, new_s_odd = odd_layer(xs, b_s, t_s, (cache_c_k[0], cache_c_v[0]))

    return (xp.reshape(b_p, t_p, D_MODEL), xs.reshape(b_s, t_s, D_MODEL),
            *new_p_even, *new_p_odd, *new_s_even, *new_s_odd)
```

```python
import functools
import math

import numpy as np
import jax
import jax.numpy as jnp
from jax import lax
from jax.experimental import pallas as pl
from jax.experimental.pallas import tpu as pltpu

F32 = jnp.float32
BF16 = jnp.bfloat16

D_MODEL = 1024
HEAD_DIM = 64
CHUNK = 64
H_A = 4
H_B = 8
H_C = 16
A_W = 512
N_SEG = 6
BAND_PAST = 512
REL_CLIP = 256
D_FF = 4 * D_MODEL
NORM_EPS = 1e-6
NEG_INF = -1e30
QK_SCALE = HEAD_DIM ** -0.5

LANES = 128
ROW_TILE = 512
FF_CHUNK = 1024
VMEM_LIMIT = 56 * 1024 * 1024

_NT = (((1,), (1,)), ((), ()))


def _rms(x, g):
    ms = jnp.mean(x * x, axis=-1, keepdims=True)
    return x * lax.rsqrt(ms + NORM_EPS) * g


def _log_sigmoid(x):
    t = -x
    return -(jnp.maximum(t, 0.0) + jnp.log1p(jnp.exp(-jnp.abs(t))))


def _const_spec(shape):
    nd = len(shape)
    return pl.BlockSpec(shape, lambda *_: (0,) * nd, pipeline_mode=pl.Buffered(1))


def _params(n_axes):
    return pltpu.CompilerParams(dimension_semantics=("arbitrary",) * n_axes,
                                vmem_limit_bytes=VMEM_LIMIT)


def _proj_even_kernel(x_ref, g_ref, w_ref, wt_ref, wft_ref, bft_ref,
                      qa_ref, ka_ref, kab_ref, va_ref, vab_ref,
                      qb_ref, kb_ref, kbb_ref, vb_ref, vbb_ref, lft_ref, *, feature_major):
    h = _rms(x_ref[...], g_ref[...]).astype(BF16)
    tm = h.shape[0]

    def seg(i):
        return jnp.dot(h, w_ref[:, i * A_W:(i + 1) * A_W], preferred_element_type=F32)

    def seg_t(i):
        return lax.dot_general(wt_ref[i], h, _NT, preferred_element_type=F32)

    def store_token_head_rows(ref, z):
        for hd in range(H_A):
            ref[pl.ds(hd, tm, stride=H_A), :] = z[:, hd * LANES:(hd + 1) * LANES]

    qa_ref[...] = (seg(0) * QK_SCALE).astype(BF16)
    z = seg(1)
    store_token_head_rows(ka_ref, z)
    kab_ref[...] = z.astype(BF16)
    z = seg(2)
    store_token_head_rows(va_ref, z)
    if feature_major:
        vab_ref[0] = seg_t(0).astype(BF16)
    else:
        vab_ref[...] = z.astype(BF16)
    qb_ref[...] = (seg(3) * QK_SCALE).astype(BF16)
    z = seg(4)
    kbb_ref[...] = z.astype(BF16)
    if feature_major:
        kb_ref[0] = seg_t(1)
        zt = seg_t(2)
        vb_ref[0] = zt
        vbb_ref[0] = zt.astype(BF16)
    else:
        kb_ref[...] = z
        z = seg(5)
        vb_ref[...] = z
        vbb_ref[...] = z.astype(BF16)
    fzt = lax.dot_general(wft_ref[...], h, _NT, preferred_element_type=F32)
    lft_ref[0] = _log_sigmoid(fzt[:H_B] + bft_ref[...])


def _proj_even(x, g, w_main, w_t, wft, bf_col, *, tiles_per_batch=None):
    m = x.shape[0]
    tm = min(ROW_TILE, m)
    n = m // tm
    row = lambda w: pl.BlockSpec((tm, w), lambda i: (i, 0))
    bfo = jax.ShapeDtypeStruct((m, A_W), BF16)
    tok_head = jax.ShapeDtypeStruct((m * H_A, LANES), F32)
    tok_head_spec = pl.BlockSpec((tm * H_A, LANES), lambda i: (i, 0))
    if tiles_per_batch:
        tpb = tiles_per_batch
        fm = lambda dt: jax.ShapeDtypeStruct((n // tpb, A_W, tpb * tm), dt)
        fm_spec = pl.BlockSpec((1, A_W, tm), lambda i: (i // tpb, 0, i % tpb))
        outs = [(bfo, row(A_W)), (tok_head, tok_head_spec), (bfo, row(A_W)), (tok_head, tok_head_spec),
                (fm(BF16), fm_spec), (bfo, row(A_W)), (fm(F32), fm_spec), (bfo, row(A_W)),
                (fm(F32), fm_spec), (fm(BF16), fm_spec)]
    else:
        f32o = jax.ShapeDtypeStruct((m, A_W), F32)
        outs = [(bfo, row(A_W)), (tok_head, tok_head_spec), (bfo, row(A_W)), (tok_head, tok_head_spec),
                (bfo, row(A_W)), (bfo, row(A_W)), (f32o, row(A_W)), (bfo, row(A_W)),
                (f32o, row(A_W)), (bfo, row(A_W))]
    outs.append((jax.ShapeDtypeStruct((n, H_B, tm), F32), pl.BlockSpec((1, H_B, tm), lambda i: (i, 0, 0))))
    return pl.pallas_call(
        functools.partial(_proj_even_kernel, feature_major=bool(tiles_per_batch)),
        grid=(n,),
        in_specs=[row(D_MODEL), _const_spec((1, D_MODEL)), _const_spec(w_main.shape),
                  _const_spec(w_t.shape), _const_spec(wft.shape), _const_spec(bf_col.shape)],
        out_specs=[s for _, s in outs],
        out_shape=[o for o, _ in outs],
        compiler_params=_params(1),
        name="proj_even",
    )(x, g, w_main, w_t, wft, bf_col)


def _cumsum_kernel(x_ref, o_ref):
    x = x_ref[0]
    t = x.shape[1]
    col = lax.broadcasted_iota(jnp.int32, x.shape, 1)
    s = 1
    while s < t:
        x = x + jnp.where(col >= s, pltpu.roll(x, s, 1), 0.0)
        s *= 2
    o_ref[0] = x


def _cumsum_time(x):
    b, h, t = x.shape
    spec = pl.BlockSpec((1, h, t), lambda i: (i, 0, 0))
    return pl.pallas_call(
        _cumsum_kernel, grid=(b,), in_specs=[spec], out_specs=spec,
        out_shape=jax.ShapeDtypeStruct(x.shape, F32),
        compiler_params=_params(1), name="cumsum_time",
    )(x)


def _stack_pair(q):
    q32 = q.astype(F32)
    lo = lax.broadcasted_iota(jnp.int32, q32.shape, 1) < HEAD_DIM
    return jnp.concatenate([jnp.where(lo, q32, 0.0), jnp.where(lo, 0.0, q32)], axis=0).astype(BF16)


def _sample_attn_kernel(*refs, mode, t, past, lam_init):
    if mode == "diff":
        slopes_ref, lamv_ref, g_ref, q_ref, kc_ref, vc_ref, kn_ref, vn_ref, o_ref = refs
    elif mode == "fox":
        cum_ref, q_ref, kc_ref, vc_ref, kn_ref, vn_ref, o_ref = refs
    else:
        c_ref, q_ref, kc_ref, vc_ref, kn_ref, vn_ref, o_ref = refs
    pairs = q_ref.shape[2] // LANES
    n_c = vc_ref.shape[1] // pairs if mode == "diff" else kc_ref.shape[2]
    row = lax.broadcasted_iota(jnp.int32, (t, n_c), 0)
    col = lax.broadcasted_iota(jnp.int32, (t, n_c), 1)
    row_n = lax.broadcasted_iota(jnp.int32, (t, t), 0)
    col_n = lax.broadcasted_iota(jnp.int32, (t, t), 1)
    lo = lax.broadcasted_iota(jnp.int32, (t, LANES), 1) < HEAD_DIM

    for p in range(pairs):
        lanes = slice(p * LANES, (p + 1) * LANES)
        q2 = _stack_pair(q_ref[0, :, lanes])
        kn = kn_ref[0, :, lanes]
        vn = vn_ref[0, :, lanes]
        if mode == "diff":
            kc = kc_ref[0, pl.ds(p, n_c, stride=pairs), :].astype(BF16)
            vc = vc_ref[0, pl.ds(p, n_c, stride=pairs), :].astype(BF16)
            s_c = lax.dot_general(q2, kc, _NT, preferred_element_type=F32)
        else:
            kc = kc_ref[0, lanes, :].astype(BF16)
            vc = vc_ref[0, lanes, :].astype(BF16)
            s_c = jnp.dot(q2, kc, preferred_element_type=F32)
        s_n = lax.dot_general(q2, kn, _NT, preferred_element_type=F32)

        outs = []
        for a in range(2):
            rows = slice(a * t, (a + 1) * t)
            if mode == "diff":
                slope = slopes_ref[p]
                t_c = s_c[rows] - slope * (past + row - col).astype(F32)
                t_n = s_n[rows] - slope * jnp.abs(row_n - col_n).astype(F32)
            elif mode == "fox":
                ck = cum_ref[0, 2 * p + a]
                t_c = s_c[rows] - ck[:, :n_c]
                t_n = jnp.where(col_n <= row_n, s_n[rows] - ck[:, n_c:n_c + t], NEG_INF)
            else:
                c = jnp.broadcast_to(c_ref[p, a:a + 1, :], (t, c_ref.shape[2]))
                bias = pltpu.roll(c, 0, 1, stride=1, stride_axis=0)
                t_c = s_c[rows] + bias[:, :n_c]
                t_n = s_n[rows] + bias[:, n_c:n_c + t]
            m = jnp.maximum(jnp.max(t_c, axis=1, keepdims=True), jnp.max(t_n, axis=1, keepdims=True))
            p_c = jnp.exp(t_c - m)
            p_n = jnp.exp(t_n - m)
            l = jnp.sum(p_c, axis=1, keepdims=True) + jnp.sum(p_n, axis=1, keepdims=True)
            if mode == "diff":
                acc = jnp.dot(p_c.astype(BF16), vc, preferred_element_type=F32)
            else:
                acc = lax.dot_general(p_c.astype(BF16), vc, _NT, preferred_element_type=F32)
            acc = acc + jnp.dot(p_n.astype(BF16), vn, preferred_element_type=F32)
            outs.append(acc / l)
        if mode == "diff":
            lv = lamv_ref[...]
            lam = (jnp.exp(jnp.sum(lv[0:1] * lv[1:2], axis=1, keepdims=True))
                   - jnp.exp(jnp.sum(lv[2:3] * lv[3:4], axis=1, keepdims=True)) + lam_init)
            oa = outs[0] - lam * outs[1]
            o_ref[0, :, lanes] = (_rms(oa, g_ref[...]) * (1.0 - lam_init)).astype(BF16)
        else:
            o_ref[0, :, lanes] = jnp.where(lo, outs[0], outs[1]).astype(BF16)


def _sample_attn(mode, q, kc, vc, kn, vn, extras, *, past, lam_init=0.0):
    b, t, width = q.shape
    whole = lambda a: pl.BlockSpec((1,) + a.shape[1:], lambda i: (i,) + (0,) * (a.ndim - 1))
    const = lambda a: pl.BlockSpec(a.shape, lambda i: (0,) * a.ndim)
    if mode == "diff":
        slopes, lamv, g = extras
        extra_specs = [pl.BlockSpec(memory_space=pltpu.SMEM), const(lamv), const(g)]
    elif mode == "fox":
        extras = (extras[0][:, :, None, :],)
        extra_specs = [whole(extras[0])]
    else:
        extra_specs = [const(extras[0])]
    return pl.pallas_call(
        functools.partial(_sample_attn_kernel, mode=mode, t=t, past=past, lam_init=lam_init),
        grid=(b,),
        in_specs=extra_specs + [whole(q), whole(kc), whole(vc), whole(kn), whole(vn)],
        out_specs=whole(q),
        out_shape=jax.ShapeDtypeStruct(q.shape, BF16),
        compiler_params=_params(1),
        name="sample_" + mode,
    )(*extras, q, kc, vc, kn, vn)


ONES_ROWS = 16


def _with_ones(vt):
    return jnp.concatenate([vt, jnp.ones((ONES_ROWS, vt.shape[1]), BF16)], axis=0)


AUG_TERMS = 3


def _lane_terms(lane, first, terms):
    out = jnp.zeros(terms[0].shape, F32)
    for t, v in enumerate(terms):
        out = jnp.where(lane == first + t, v, out)
    return out


def _flash_t_kernel(*refs, mode, tq, lam_init):
    tk = tq
    if mode == "diff":
        (slopes_ref, lamv_ref, g_ref, q_ref, k_ref, vt_ref, o_ref,
         m_sc, acc_sc, q2_sc, s0_sc, s1_sc, mu_sc, kaug_sc, bdiag_sc) = refs
    else:
        cum_ref, q_ref, k_ref, vt_ref, o_ref, m_sc, acc_sc, q2_sc, s0_sc, s1_sc, mu_sc, kaug_sc, bdiag_sc = refs
    pair = pl.program_id(0)
    qi = pl.program_id(2)
    t_all = kaug_sc.shape[0]
    s_bufs = (s0_sc, s1_sc)

    @pl.when((pl.program_id(1) == 0) & (qi == 0))
    def _():
        row = lax.broadcasted_iota(jnp.int32, (tk, tq), 0)
        col = lax.broadcasted_iota(jnp.int32, (tk, tq), 1)
        lane_q = lax.broadcasted_iota(jnp.int32, (2 * tq, LANES), 1)
        first_q = jnp.where(lax.broadcasted_iota(jnp.int32, (2 * tq, LANES), 0) < tq, 0, AUG_TERMS)
        if mode == "diff":
            slope = slopes_ref[pair]
            visible = (row >> 6) <= (col >> 6)
            bdiag_sc[...] = jnp.where(visible, -2.0 * slope * jnp.maximum(row - col, 0).astype(F32), NEG_INF)
            q2_sc[:, LANES:] = jnp.where(lane_q < AUG_TERMS, 1.0, 0.0).astype(BF16)
            for c0 in range(0, t_all, tk):
                j = c0 + lax.broadcasted_iota(jnp.int32, (tk, LANES), 0)
                lane = lax.broadcasted_iota(jnp.int32, (tk, LANES), 1)
                terms = [((j >> 8) << 8).astype(F32), (((j >> 4) & 15) << 4).astype(F32), (j & 15).astype(F32)]
                kaug_sc[c0:c0 + tk, :] = (slope * _lane_terms(lane, 0, terms)).astype(BF16)
        else:
            bdiag_sc[...] = jnp.where(row <= col, 0.0, NEG_INF)
            mine = (lane_q >= first_q) & (lane_q < first_q + AUG_TERMS)
            q2_sc[:, LANES:] = jnp.where(mine, -1.0, 0.0).astype(BF16)

    if mode == "fox":
        @pl.when(qi == 0)
        def _():
            for c0 in range(0, t_all, tk):
                lane = lax.broadcasted_iota(jnp.int32, (tk, LANES), 1)
                feat = jnp.zeros((tk, LANES), F32)
                for a in range(2):
                    r = cum_ref[0, 2 * pair + a, :, c0:c0 + tk]
                    ck = jnp.broadcast_to(r, (LANES, tk)).T
                    hi = ck.astype(BF16).astype(F32)
                    mid = (ck - hi).astype(BF16).astype(F32)
                    lo = ((ck - hi) - mid).astype(BF16).astype(F32)
                    feat = feat + _lane_terms(lane, AUG_TERMS * a, [hi, mid, lo])
                kaug_sc[c0:c0 + tk, :] = feat.astype(BF16)

    q2_sc[:, :LANES] = _stack_pair(q_ref[0])
    m_sc[...] = jnp.full(m_sc.shape, NEG_INF, F32)
    acc_sc[...] = jnp.zeros(acc_sc.shape, F32)

    def stage_a(k0, slot, diag):
        kk = jnp.concatenate([k_ref[0, pl.ds(k0, tk), :], kaug_sc[pl.ds(k0, tk), :]], axis=1)
        st = lax.dot_general(kk, q2_sc[...], _NT, preferred_element_type=F32)
        for a in range(2):
            sa = st[:, a * tq:(a + 1) * tq]
            if diag:
                sa = sa + bdiag_sc[...]
            s_bufs[slot][:, a * tq:(a + 1) * tq] = sa
            mu_sc[slot, :, a * tq:(a + 1) * tq] = jnp.max(sa, axis=0, keepdims=True)

    def stage_b(k0, slot):
        vt = vt_ref[0, :, pl.ds(k0, tk)]
        for a in range(2):
            m_prev = m_sc[a]
            m_next = jnp.maximum(m_prev, mu_sc[slot, :, a * tq:(a + 1) * tq])
            p = jnp.exp(s_bufs[slot][:, a * tq:(a + 1) * tq] - m_next)
            alpha = jnp.exp(m_prev - m_next)
            vta = _with_ones(vt if mode == "diff" else vt[a * HEAD_DIM:(a + 1) * HEAD_DIM])
            acc_sc[a] = alpha * acc_sc[a] + jnp.dot(vta, p.astype(BF16), preferred_element_type=F32)
            m_sc[a] = m_next

    blk = lambda j: pl.multiple_of(j * tk, tk)
    pl.when(qi == 0)(lambda: stage_a(0, 0, True))
    pl.when(qi > 0)(lambda: stage_a(0, 0, False))

    def two_blocks(i, carry):
        stage_a(blk(2 * i + 1), 1, False)
        stage_b(blk(2 * i), 0)
        stage_a(blk(2 * i + 2), 0, False)
        stage_b(blk(2 * i + 1), 1)
        return carry

    lax.fori_loop(0, (qi - 1) // 2, two_blocks, 0)

    @pl.when(qi == 0)
    def _():
        stage_b(0, 0)

    @pl.when(qi % 2 == 1)
    def _():
        stage_a(blk(qi), 1, True)
        stage_b(blk(qi - 1), 0)
        stage_b(blk(qi), 1)

    @pl.when((qi % 2 == 0) & (qi > 0))
    def _():
        stage_a(blk(qi - 1), 1, False)
        stage_b(blk(qi - 2), 0)
        stage_a(blk(qi), 0, True)
        stage_b(blk(qi - 1), 1)
        stage_b(blk(qi), 0)

    d = acc_sc.shape[1] - ONES_ROWS
    o0 = acc_sc[0, :d, :] / acc_sc[0, d:d + 1, :]
    o1 = acc_sc[1, :d, :] / acc_sc[1, d:d + 1, :]
    if mode == "diff":
        lv = lamv_ref[...]
        lam = (jnp.exp(jnp.sum(lv[0:1] * lv[1:2], axis=1, keepdims=True))
               - jnp.exp(jnp.sum(lv[2:3] * lv[3:4], axis=1, keepdims=True)) + lam_init)
        oa = o0 - lam * o1
        ms = jnp.mean(oa * oa, axis=0, keepdims=True)
        y = (oa * lax.rsqrt(ms + NORM_EPS)).T * g_ref[...]
        o_ref[0] = (y * (1.0 - lam_init)).astype(BF16)
    else:
        o_ref[0] = jnp.concatenate([o0, o1], axis=0).T.astype(BF16)


def _flash_t(mode, q, k, vt, extras, *, tq, lam_init=0.0):
    b, t, width = q.shape
    pairs = width // LANES
    qspec = pl.BlockSpec((1, tq, LANES), lambda p, bi, i: (bi, i, p))
    kspec = pl.BlockSpec((1, t, LANES), lambda p, bi, i: (bi, 0, p))
    vspec = pl.BlockSpec((1, LANES, t), lambda p, bi, i: (bi, p, 0))
    scratch = [pltpu.VMEM((2, 1, tq), F32),
               pltpu.VMEM((2, (LANES if mode == "diff" else HEAD_DIM) + ONES_ROWS, tq), F32),
               pltpu.VMEM((2 * tq, 2 * LANES), BF16),
               pltpu.VMEM((tq, 2 * tq), F32), pltpu.VMEM((tq, 2 * tq), F32),
               pltpu.VMEM((2, 1, 2 * tq), F32),
               pltpu.VMEM((t, LANES), BF16),
               pltpu.VMEM((tq, tq), F32)]
    if mode == "diff":
        slopes, lamv, g = extras
        extra_specs = [pl.BlockSpec(memory_space=pltpu.SMEM),
                       pl.BlockSpec(lamv.shape, lambda p, bi, i: (0, 0)),
                       pl.BlockSpec(g.shape, lambda p, bi, i: (0, 0))]
    else:
        extras = (extras[0][:, :, None, :],)
        extra_specs = [pl.BlockSpec((1,) + extras[0].shape[1:], lambda p, bi, i: (bi, 0, 0, 0))]
    return pl.pallas_call(
        functools.partial(_flash_t_kernel, mode=mode, tq=tq, lam_init=lam_init),
        grid=(pairs, b, t // tq),
        in_specs=extra_specs + [qspec, kspec, vspec],
        out_specs=qspec,
        out_shape=jax.ShapeDtypeStruct(q.shape, BF16),
        scratch_shapes=scratch,
        compiler_params=_params(3),
        name="flash_t_" + mode,
    )(*extras, q, k, vt)


def _band_t_kernel(c_ref, q_ref, k_ref, vt_ref, o_ref, s0_sc, s1_sc, mu_sc, bias_sc, *, tq, t):
    window = BAND_PAST + tq
    n_q = t // tq
    s_bufs = (s0_sc, s1_sc)

    @pl.when(pl.program_id(1) == 0)
    def _():
        row = lax.broadcasted_iota(jnp.int32, (tq, window), 0)
        col = lax.broadcasted_iota(jnp.int32, (tq, window), 1)
        visible = ((col >> 6) >= (row >> 6)) & ((col >> 6) <= (row >> 6) + BAND_PAST // CHUNK)
        for a in range(2):
            c = jnp.broadcast_to(c_ref[0, a:a + 1, :], (tq, c_ref.shape[2]))
            bias = pltpu.roll(c, 0, 1, stride=1, stride_axis=0)[:, :window]
            bias_sc[a] = jnp.where(visible, bias, NEG_INF).T

    lane_q = lax.broadcasted_iota(jnp.int32, (2 * tq, LANES), 1)
    q_extra = jnp.where(lane_q == 0, 1.0, 0.0).astype(BF16)

    def stage_a(qb, slot):
        w0 = pl.multiple_of(qb * tq, tq)
        q2 = jnp.concatenate([_stack_pair(q_ref[0, pl.ds(w0, tq), :]), q_extra], axis=1)
        pad_row = (w0 + lax.broadcasted_iota(jnp.int32, (window, LANES), 0)) < BAND_PAST
        lane_k = lax.broadcasted_iota(jnp.int32, (window, LANES), 1)
        k_extra = jnp.where(pad_row & (lane_k == 0), NEG_INF, 0.0).astype(BF16)
        kk = jnp.concatenate([k_ref[0, pl.ds(w0, window), :], k_extra], axis=1)
        st = lax.dot_general(kk, q2, _NT, preferred_element_type=F32)
        for a in range(2):
            u = st[:, a * tq:(a + 1) * tq] + bias_sc[a]
            s_bufs[slot][:, a * tq:(a + 1) * tq] = u
            mu_sc[slot, :, a * tq:(a + 1) * tq] = jnp.max(u, axis=0, keepdims=True)

    def stage_b(qb, slot):
        w0 = pl.multiple_of(qb * tq, tq)
        vt = vt_ref[0, :, pl.ds(w0, window)]
        outs = []
        for a in range(2):
            p = jnp.exp(s_bufs[slot][:, a * tq:(a + 1) * tq] - mu_sc[slot, :, a * tq:(a + 1) * tq])
            vta = _with_ones(vt[a * HEAD_DIM:(a + 1) * HEAD_DIM])
            r = jnp.dot(vta, p.astype(BF16), preferred_element_type=F32)
            outs.append(r[:HEAD_DIM] / r[HEAD_DIM:HEAD_DIM + 1])
        o_ref[0, pl.ds(w0, tq), :] = jnp.concatenate(outs, axis=0).T.astype(BF16)

    stage_a(0, 0)

    def two_blocks(i, carry):
        stage_a(2 * i + 1, 1)
        stage_b(2 * i, 0)
        stage_a(2 * i + 2, 0)
        stage_b(2 * i + 1, 1)
        return carry

    lax.fori_loop(0, n_q // 2 - 1, two_blocks, 0)
    stage_a(n_q - 1, 1)
    stage_b(n_q - 2, 0)
    stage_b(n_q - 1, 1)


def _band_t(q, k_pad, vt_pad, cvec, *, tq):
    b, t, width = q.shape
    pairs = width // LANES
    window = BAND_PAST + tq
    assert (t // tq) % 2 == 0 and t // tq >= 2
    qspec = pl.BlockSpec((1, t, LANES), lambda p, bi: (bi, 0, p))
    kspec = pl.BlockSpec((1, k_pad.shape[1], LANES), lambda p, bi: (bi, 0, p))
    vspec = pl.BlockSpec((1, LANES, vt_pad.shape[2]), lambda p, bi: (bi, p, 0))
    cspec = pl.BlockSpec((1, 2, cvec.shape[2]), lambda p, bi: (p, 0, 0))
    return pl.pallas_call(
        functools.partial(_band_t_kernel, tq=tq, t=t),
        grid=(pairs, b),
        in_specs=[cspec, qspec, kspec, vspec],
        out_specs=qspec,
        out_shape=jax.ShapeDtypeStruct(q.shape, BF16),
        scratch_shapes=[pltpu.VMEM((window, 2 * tq), F32), pltpu.VMEM((window, 2 * tq), F32),
                        pltpu.VMEM((2, 1, 2 * tq), F32), pltpu.VMEM((2, window, tq), F32)],
        compiler_params=_params(2),
        name="band_t_attn",
    )(cvec, q, k_pad, vt_pad)


def _proj_odd_kernel(x_ref, g_ref, w_ref, wt_ref, q_ref, k_ref, v_ref, kt_ref, vt_ref, *,
                     tail_every, v_feature_major):
    h = _rms(x_ref[...], g_ref[...]).astype(BF16)

    def seg(i):
        return jnp.dot(h, w_ref[:, i * D_MODEL:(i + 1) * D_MODEL], preferred_element_type=F32)

    def seg_t(i):
        return lax.dot_general(wt_ref[i], h, _NT, preferred_element_type=F32)

    is_tail = pl.program_id(0) % tail_every == tail_every - 1
    q_ref[...] = (seg(0) * QK_SCALE).astype(BF16)
    k_ref[...] = seg(1).astype(BF16)
    if v_feature_major:
        vzt = seg_t(1)
        v_ref[0] = vzt.astype(BF16)

        @pl.when(is_tail)
        def _():
            kt_ref[0] = seg_t(0)
            vt_ref[0] = vzt
    else:
        v_ref[...] = seg(2).astype(BF16)

        @pl.when(is_tail)
        def _():
            kt_ref[0] = seg_t(0)
            vt_ref[0] = seg_t(1)


def _proj_odd(x, g, w, w_t, *, tail_every, v_feature_major):
    m = x.shape[0]
    tm = min(ROW_TILE, m)
    n = m // tm
    row = pl.BlockSpec((tm, D_MODEL), lambda i: (i, 0))
    tail = pl.BlockSpec((1, D_MODEL, tm), lambda i: (i // tail_every, 0, 0))
    bfo = jax.ShapeDtypeStruct((m, D_MODEL), BF16)
    tailo = jax.ShapeDtypeStruct((n // tail_every, D_MODEL, tm), F32)
    w_vt = w_t
    if v_feature_major:
        vo = jax.ShapeDtypeStruct((n // tail_every, D_MODEL, tail_every * tm), BF16)
        vspec = pl.BlockSpec((1, D_MODEL, tm), lambda i: (i // tail_every, 0, i % tail_every))
    else:
        vo, vspec = bfo, row
    return pl.pallas_call(
        functools.partial(_proj_odd_kernel, tail_every=tail_every, v_feature_major=v_feature_major),
        grid=(n,),
        in_specs=[row, _const_spec((1, D_MODEL)), _const_spec(w.shape), _const_spec(w_vt.shape)],
        out_specs=[row, row, vspec, tail, tail],
        out_shape=[bfo, bfo, vo, tailo, tailo],
        compiler_params=_params(1),
        name="proj_odd",
    )(x, g, w, w_vt)


def _post_kernel(*refs, n_o):
    o_refs = refs[:n_o]
    x_ref, wo_ref, gpm_ref, gpre_ref, wup_ref, wdn_ref, gpf_ref, out_ref = refs[n_o:]
    o = o_refs[0][...] if n_o == 1 else jnp.concatenate([r[...] for r in o_refs], axis=1)
    mixed = jnp.dot(o, wo_ref[...], preferred_element_type=F32)
    x1 = x_ref[...] + _rms(mixed, gpm_ref[...])
    h = _rms(x1, gpre_ref[...]).astype(BF16)
    acc = jnp.zeros(x1.shape, F32)
    for c in range(D_FF // FF_CHUNK):
        u = jnp.dot(h, wup_ref[:, c * FF_CHUNK:(c + 1) * FF_CHUNK], preferred_element_type=F32)
        u = jnp.maximum(u, 0.0)
        acc = acc + jnp.dot((u * u).astype(BF16), wdn_ref[c * FF_CHUNK:(c + 1) * FF_CHUNK, :],
                            preferred_element_type=F32)
    out_ref[...] = x1 + _rms(acc, gpf_ref[...])


def _post(o_list, x, wo, gpm, gpre, wup, wdn, gpf):
    m = x.shape[0]
    tm = min(ROW_TILE, m)
    row = lambda w: pl.BlockSpec((tm, w), lambda i: (i, 0))
    gspec = _const_spec((1, D_MODEL))
    return pl.pallas_call(
        functools.partial(_post_kernel, n_o=len(o_list)),
        grid=(m // tm,),
        in_specs=[row(o.shape[1]) for o in o_list]
                 + [row(D_MODEL), _const_spec(wo.shape), gspec, gspec,
                    _const_spec(wup.shape), _const_spec(wdn.shape), gspec],
        out_specs=row(D_MODEL),
        out_shape=jax.ShapeDtypeStruct((m, D_MODEL), F32),
        compiler_params=_params(1),
        name="post_mix_mlp",
    )(*o_list, x, wo, gpm, gpre, wup, wdn, gpf)


def _band_bias_vectors(table):
    n = 2 * BAND_PAST
    m = np.arange(n)
    d = np.where(m <= n - REL_CLIP, -m, n - m)
    idx = np.clip(d + BAND_PAST, -REL_CLIP, REL_CLIP) + REL_CLIP
    return jnp.take(table.astype(F32), jnp.asarray(idx, jnp.int32), axis=1).reshape(H_C // 2, 2, n)


def _pad_time(x, front, back):
    return jnp.pad(x, ((0, 0), (front, back), (0, 0)))


def kernel(x_prompt, x_sample, cache_a_k, cache_a_v, cache_b_k, cache_b_v, cache_b_logf, cache_c_k, cache_c_v,
           w_in_even, b_forget, lam_q1, lam_k1, lam_q2, lam_k2, subln_g, w_out_even, w_in_odd, rel_bias,
           w_out_odd, g_pre_mix, g_post_mix, g_pre_ffn, g_post_ffn, w_ffn_up, w_ffn_down):
    b_p, t_p, _ = x_prompt.shape
    b_s, t_s, _ = x_sample.shape
    past = cache_b_logf.shape[2]
    m_p, m_s = b_p * t_p, b_s * t_s
    xp = x_prompt.reshape(m_p, D_MODEL)
    xs = x_sample.reshape(m_s, D_MODEL)
    gvec = lambda a, l: a[l].reshape(1, D_MODEL)

    lam_init = 0.8 - 0.6 * math.exp(-0.3 * 0)
    w_in = w_in_even[0]
    w_main = w_in[:, :N_SEG * A_W].astype(BF16)
    w_f = w_in[:, N_SEG * A_W:]
    wft = jnp.pad(w_f.T, ((0, 16 - H_B), (0, 0))).astype(BF16)
    bf_col = b_forget[0].reshape(H_B, 1)
    slopes = 2.0 ** (-8.0 * jnp.arange(1, H_A + 1, dtype=F32) / H_A)
    lamv = jnp.stack([lam_q1[0], lam_k1[0], lam_q2[0], lam_k2[0]]).astype(F32)
    sub_g = subln_g[0].reshape(1, 2 * HEAD_DIM)
    wo0 = w_out_even[0].astype(BF16)
    wup0, wdn0 = w_ffn_up[0].astype(BF16), w_ffn_down[0].astype(BF16)

    w_t = jnp.stack([w_in[:, s * A_W:(s + 1) * A_W].T for s in (2, 4, 5)]).astype(BF16)

    def even_layer(x, bsz, t, caches):
        m = x.shape[0]
        tm = min(ROW_TILE, m)
        (qa, ka, kab, va, vab, qb, kb, kbb, vb, vbb, lft) = _proj_even(
            x, gvec(g_pre_mix, 0), w_main, w_t, wft, bf_col,
            tiles_per_batch=(t // tm) if caches is None else None)
        r3 = lambda a: a.reshape(bsz, t, A_W)
        lft = lft.transpose(1, 0, 2).reshape(H_B, bsz, t).transpose(1, 0, 2)
        tok_head = lambda a: a.reshape(1, bsz, t, H_A, 2 * HEAD_DIM)
        if caches is None:
            oa = _flash_t("diff", r3(qa), r3(kab), vab, (slopes, lamv, sub_g), tq=512, lam_init=lam_init)
            ob = _flash_t("fox", r3(qb), r3(kbb), vbb, (_cumsum_time(lft),), tq=512)
            time_minor = lambda a: a.reshape(bsz, H_B, HEAD_DIM, t).transpose(0, 3, 1, 2)[None]
            new = (tok_head(ka), tok_head(va), time_minor(kb), time_minor(vb), lft.transpose(0, 2, 1)[None])
        else:
            c_ka, c_va, c_kb, c_vb, c_lf = caches
            p_len = c_ka.shape[1]
            t_all = p_len + t
            t_pad = -(-t_all // LANES) * LANES
            assert p_len % CHUNK == 0 and t <= CHUNK
            lf_all = jnp.concatenate([c_lf.astype(F32).transpose(0, 2, 1), lft], axis=2)
            cum = _cumsum_time(jnp.pad(lf_all, ((0, 0), (0, 0), (0, t_pad - t_all))))
            tok_head_rows = lambda c: c.reshape(bsz, p_len * H_A, 2 * HEAD_DIM)
            feat_major = lambda c: c.transpose(0, 2, 3, 1).reshape(bsz, A_W, p_len)
            oa = _sample_attn("diff", r3(qa), tok_head_rows(c_ka), tok_head_rows(c_va), r3(kab), r3(vab),
                              (slopes, lamv, sub_g), past=p_len, lam_init=lam_init)
            ob = _sample_attn("fox", r3(qb), feat_major(c_kb), feat_major(c_vb), r3(kbb), r3(vbb),
                              (cum,), past=p_len)
            new = (tok_head(ka), tok_head(va), kb.reshape(1, bsz, t, H_B, HEAD_DIM),
                   vb.reshape(1, bsz, t, H_B, HEAD_DIM), lft.transpose(0, 2, 1)[None])
        x_out = _post([oa.reshape(m, A_W), ob.reshape(m, A_W)], x, wo0, gvec(g_post_mix, 0),
                      gvec(g_pre_ffn, 0), wup0, wdn0, gvec(g_post_ffn, 0))
        return x_out, new

    xp, new_p_even = even_layer(xp, b_p, t_p, None)
    xs, new_s_even = even_layer(xs, b_s, t_s, (cache_a_k[0], cache_a_v[0], cache_b_k[0], cache_b_v[0],
                                              cache_b_logf[0]))

    w_odd = w_in_odd[0].astype(BF16)
    w_odd_t = jnp.stack([w_in_odd[0][:, s * D_MODEL:(s + 1) * D_MODEL].T for s in (1, 2)]).astype(BF16)
    cvec = _band_bias_vectors(rel_bias[0])
    to_cache = lambda a: a.reshape(a.shape[0], H_C, HEAD_DIM, a.shape[2]).transpose(0, 3, 1, 2)[None]
    from_cache = lambda c: c.transpose(0, 2, 3, 1).reshape(c.shape[0], D_MODEL, c.shape[1])
    wo1 = w_out_odd[0].astype(BF16)
    wup1, wdn1 = w_ffn_up[1].astype(BF16), w_ffn_down[1].astype(BF16)

    def odd_layer(x, bsz, t, caches):
        m = x.shape[0]
        tm = min(ROW_TILE, m)
        tail_every = (t // tm) if caches is None else 1
        q, k, v, k_tail, v_tail = _proj_odd(x, gvec(g_pre_mix, 1), w_odd, w_odd_t, tail_every=tail_every,
                                            v_feature_major=caches is None)
        r3 = lambda a: a.reshape(bsz, t, D_MODEL)
        if caches is None:
            kp = _pad_time(r3(k), BAND_PAST, 0)
            vtp = jnp.pad(v, ((0, 0), (0, 0), (BAND_PAST, 0)))
            o = _band_t(r3(q), kp, vtp, cvec, tq=256)
            assert tm == min(BAND_PAST, t)
            new = (to_cache(k_tail), to_cache(v_tail))
        else:
            c_k, c_v = caches
            assert c_k.shape[1] == BAND_PAST and past % CHUNK == 0 and past >= BAND_PAST and t <= CHUNK
            o = _sample_attn("band", r3(q), from_cache(c_k), from_cache(c_v), r3(k), r3(v), (cvec,), past=past)
            upd = lambda c, tail: to_cache(jnp.concatenate(
                [from_cache(c)[:, :, t:], tail.reshape(D_MODEL, bsz, t).transpose(1, 0, 2)], axis=2))
            new = (upd(c_k, k_tail), upd(c_v, v_tail))
        x_out = _post([o.reshape(m, D_MODEL)], x, wo1, gvec(g_post_mix, 1), gvec(g_pre_ffn, 1),
                      wup1, wdn1, gvec(g_post_ffn, 1))
        return x_out, new

    xp, new_p_odd = odd_layer(xp, b_p, t_p, None)
    xs, new_s_odd = odd_layer(xs, b_s, t_s, (cache_c_k[0], cache_c_v[0]))

    return (xp.reshape(b_p, t_p, D_MODEL), xs.reshape(b_s, t_s, D_MODEL),
            *new_p_even, *new_p_odd, *new_s_even, *new_s_odd)
```

```python
import functools
import math

import numpy as np
import jax
import jax.numpy as jnp
from jax import lax
from jax.experimental import pallas as pl
from jax.experimental.pallas import tpu as pltpu

F32 = jnp.float32
BF16 = jnp.bfloat16

D_MODEL = 1024
HEAD_DIM = 64
CHUNK = 64
H_A = 4
H_B = 8
H_C = 16
A_W = 512
N_SEG = 6
BAND_PAST = 512
REL_CLIP = 256
D_FF = 4 * D_MODEL
NORM_EPS = 1e-6
NEG_INF = -1e30
QK_SCALE = HEAD_DIM ** -0.5

LANES = 128
ROW_TILE = 512
FF_CHUNK = 1024
VMEM_LIMIT = 56 * 1024 * 1024

_NT = (((1,), (1,)), ((), ()))


def _rms(x, g):
    ms = jnp.mean(x * x, axis=-1, keepdims=True)
    return x * lax.rsqrt(ms + NORM_EPS) * g


def _log_sigmoid(x):
    t = -x
    return -(jnp.maximum(t, 0.0) + jnp.log1p(jnp.exp(-jnp.abs(t))))


def _const_spec(shape):
    nd = len(shape)
    return pl.BlockSpec(shape, lambda *_: (0,) * nd, pipeline_mode=pl.Buffered(1))


def _params(n_axes):
    return pltpu.CompilerParams(dimension_semantics=("arbitrary",) * n_axes,
                                vmem_limit_bytes=VMEM_LIMIT)


def _proj_even_kernel(x_ref, g_ref, wt_ref, wft_ref, bft_ref,
                      qa_ref, ka_ref, kab_ref, va_ref, vab_ref,
                      qb_ref, kb_ref, kbb_ref, vb_ref, vbb_ref, lft_ref, *, feature_major):
    h = _rms(x_ref[...], g_ref[...]).astype(BF16)
    tm = h.shape[0]

    def seg(i):
        return lax.dot_general(h, wt_ref[i * A_W:(i + 1) * A_W, :], _NT, preferred_element_type=F32)

    def seg_t(i):
        return lax.dot_general(wt_ref[i * A_W:(i + 1) * A_W, :], h, _NT, preferred_element_type=F32)

    def store_token_head_rows(ref, z):
        for hd in range(H_A):
            ref[pl.ds(hd, tm, stride=H_A), :] = z[:, hd * LANES:(hd + 1) * LANES]

    qa_ref[...] = (seg(0) * QK_SCALE).astype(BF16)
    z = seg(1)
    store_token_head_rows(ka_ref, z)
    kab_ref[...] = z.astype(BF16)
    z = seg(2)
    store_token_head_rows(va_ref, z)
    if feature_major:
        vab_ref[0] = z.T.astype(BF16)
    else:
        vab_ref[...] = z.astype(BF16)
    qb_ref[...] = (seg(3) * QK_SCALE).astype(BF16)
    z = seg(4)
    kbb_ref[...] = z.astype(BF16)
    if feature_major:
        kb_ref[0] = z.T
        zt = seg_t(5)
        vb_ref[0] = zt
        vbb_ref[0] = zt.astype(BF16)
    else:
        kb_ref[...] = z
        z = seg(5)
        vb_ref[...] = z
        vbb_ref[...] = z.astype(BF16)
    fzt = lax.dot_general(wft_ref[...], h, _NT, preferred_element_type=F32)
    lft_ref[0] = _log_sigmoid(fzt[:H_B] + bft_ref[...])


def _proj_even(x, g, w_t, wft, bf_col, *, tiles_per_batch=None):
    m = x.shape[0]
    tm = min(ROW_TILE, m)
    n = m // tm
    row = lambda w: pl.BlockSpec((tm, w), lambda i: (i, 0))
    bfo = jax.ShapeDtypeStruct((m, A_W), BF16)
    tok_head = jax.ShapeDtypeStruct((m * H_A, LANES), F32)
    tok_head_spec = pl.BlockSpec((tm * H_A, LANES), lambda i: (i, 0))
    if tiles_per_batch:
        tpb = tiles_per_batch
        fm = lambda dt: jax.ShapeDtypeStruct((n // tpb, A_W, tpb * tm), dt)
        fm_spec = pl.BlockSpec((1, A_W, tm), lambda i: (i // tpb, 0, i % tpb))
        outs = [(bfo, row(A_W)), (tok_head, tok_head_spec), (bfo, row(A_W)), (tok_head, tok_head_spec),
                (fm(BF16), fm_spec), (bfo, row(A_W)), (fm(F32), fm_spec), (bfo, row(A_W)),
                (fm(F32), fm_spec), (fm(BF16), fm_spec)]
    else:
        f32o = jax.ShapeDtypeStruct((m, A_W), F32)
        outs = [(bfo, row(A_W)), (tok_head, tok_head_spec), (bfo, row(A_W)), (tok_head, tok_head_spec),
                (bfo, row(A_W)), (bfo, row(A_W)), (f32o, row(A_W)), (bfo, row(A_W)),
                (f32o, row(A_W)), (bfo, row(A_W))]
    outs.append((jax.ShapeDtypeStruct((n, H_B, tm), F32), pl.BlockSpec((1, H_B, tm), lambda i: (i, 0, 0))))
    return pl.pallas_call(
        functools.partial(_proj_even_kernel, feature_major=bool(tiles_per_batch)),
        grid=(n,),
        in_specs=[row(D_MODEL), _const_spec((1, D_MODEL)), _const_spec(w_t.shape),
                  _const_spec(wft.shape), _const_spec(bf_col.shape)],
        out_specs=[s for _, s in outs],
        out_shape=[o for o, _ in outs],
        compiler_params=_params(1),
        name="proj_even",
    )(x, g, w_t, wft, bf_col)


def _cumsum_kernel(x_ref, o_ref):
    x = x_ref[0]
    t = x.shape[1]
    col = lax.broadcasted_iota(jnp.int32, x.shape, 1)
    s = 1
    while s < t:
        x = x + jnp.where(col >= s, pltpu.roll(x, s, 1), 0.0)
        s *= 2
    o_ref[0] = x


def _cumsum_time(x):
    b, h, t = x.shape
    spec = pl.BlockSpec((1, h, t), lambda i: (i, 0, 0))
    return pl.pallas_call(
        _cumsum_kernel, grid=(b,), in_specs=[spec], out_specs=spec,
        out_shape=jax.ShapeDtypeStruct(x.shape, F32),
        compiler_params=_params(1), name="cumsum_time",
    )(x)


def _stack_pair(q):
    q32 = q.astype(F32)
    lo = lax.broadcasted_iota(jnp.int32, q32.shape, 1) < HEAD_DIM
    return jnp.concatenate([jnp.where(lo, q32, 0.0), jnp.where(lo, 0.0, q32)], axis=0).astype(BF16)


def _sample_attn_kernel(*refs, mode, t, past, lam_init):
    if mode == "diff":
        slopes_ref, lamv_ref, g_ref, q_ref, kc_ref, vc_ref, kn_ref, vn_ref, o_ref = refs
    elif mode == "fox":
        cum_ref, q_ref, kc_ref, vc_ref, kn_ref, vn_ref, o_ref = refs
    else:
        c_ref, q_ref, kc_ref, vc_ref, kn_ref, vn_ref, o_ref = refs
    pairs = q_ref.shape[2] // LANES
    n_c = vc_ref.shape[1] // pairs if mode == "diff" else kc_ref.shape[2]
    row = lax.broadcasted_iota(jnp.int32, (t, n_c), 0)
    col = lax.broadcasted_iota(jnp.int32, (t, n_c), 1)
    row_n = lax.broadcasted_iota(jnp.int32, (t, t), 0)
    col_n = lax.broadcasted_iota(jnp.int32, (t, t), 1)
    lo = lax.broadcasted_iota(jnp.int32, (t, LANES), 1) < HEAD_DIM

    for p in range(pairs):
        lanes = slice(p * LANES, (p + 1) * LANES)
        q2 = _stack_pair(q_ref[0, :, lanes])
        kn = kn_ref[0, :, lanes]
        vn = vn_ref[0, :, lanes]
        if mode == "diff":
            kc = kc_ref[0, pl.ds(p, n_c, stride=pairs), :].astype(BF16)
            vc = vc_ref[0, pl.ds(p, n_c, stride=pairs), :].astype(BF16)
            s_c = lax.dot_general(q2, kc, _NT, preferred_element_type=F32)
        else:
            kc = kc_ref[0, lanes, :].astype(BF16)
            vc = vc_ref[0, lanes, :].astype(BF16)
            s_c = jnp.dot(q2, kc, preferred_element_type=F32)
        s_n = lax.dot_general(q2, kn, _NT, preferred_element_type=F32)

        outs = []
        for a in range(2):
            rows = slice(a * t, (a + 1) * t)
            if mode == "diff":
                slope = slopes_ref[p]
                t_c = s_c[rows] - slope * (past + row - col).astype(F32)
                t_n = s_n[rows] - slope * jnp.abs(row_n - col_n).astype(F32)
            elif mode == "fox":
                ck = cum_ref[0, 2 * p + a]
                t_c = s_c[rows] - ck[:, :n_c]
                t_n = jnp.where(col_n <= row_n, s_n[rows] - ck[:, n_c:n_c + t], NEG_INF)
            else:
                c = jnp.broadcast_to(c_ref[p, a:a + 1, :], (t, c_ref.shape[2]))
                bias = pltpu.roll(c, 0, 1, stride=1, stride_axis=0)
                t_c = s_c[rows] + bias[:, :n_c]
                t_n = s_n[rows] + bias[:, n_c:n_c + t]
            m = jnp.maximum(jnp.max(t_c, axis=1, keepdims=True), jnp.max(t_n, axis=1, keepdims=True))
            p_c = jnp.exp(t_c - m)
            p_n = jnp.exp(t_n - m)
            l = jnp.sum(p_c, axis=1, keepdims=True) + jnp.sum(p_n, axis=1, keepdims=True)
            if mode == "diff":
                acc = jnp.dot(p_c.astype(BF16), vc, preferred_element_type=F32)
            else:
                acc = lax.dot_general(p_c.astype(BF16), vc, _NT, preferred_element_type=F32)
            acc = acc + jnp.dot(p_n.astype(BF16), vn, preferred_element_type=F32)
            outs.append(acc / l)
        if mode == "diff":
            lv = lamv_ref[...]
            lam = (jnp.exp(jnp.sum(lv[0:1] * lv[1:2], axis=1, keepdims=True))
                   - jnp.exp(jnp.sum(lv[2:3] * lv[3:4], axis=1, keepdims=True)) + lam_init)
            oa = outs[0] - lam * outs[1]
            o_ref[0, :, lanes] = (_rms(oa, g_ref[...]) * (1.0 - lam_init)).astype(BF16)
        else:
            o_ref[0, :, lanes] = jnp.where(lo, outs[0], outs[1]).astype(BF16)


def _sample_attn(mode, q, kc, vc, kn, vn, extras, *, past, lam_init=0.0):
    b, t, width = q.shape
    whole = lambda a: pl.BlockSpec((1,) + a.shape[1:], lambda i: (i,) + (0,) * (a.ndim - 1))
    const = lambda a: pl.BlockSpec(a.shape, lambda i: (0,) * a.ndim)
    if mode == "diff":
        slopes, lamv, g = extras
        extra_specs = [pl.BlockSpec(memory_space=pltpu.SMEM), const(lamv), const(g)]
    elif mode == "fox":
        extras = (extras[0][:, :, None, :],)
        extra_specs = [whole(extras[0])]
    else:
        extra_specs = [const(extras[0])]
    return pl.pallas_call(
        functools.partial(_sample_attn_kernel, mode=mode, t=t, past=past, lam_init=lam_init),
        grid=(b,),
        in_specs=extra_specs + [whole(q), whole(kc), whole(vc), whole(kn), whole(vn)],
        out_specs=whole(q),
        out_shape=jax.ShapeDtypeStruct(q.shape, BF16),
        compiler_params=_params(1),
        name="sample_" + mode,
    )(*extras, q, kc, vc, kn, vn)


ONES_ROWS = 16


def _with_ones(vt):
    return jnp.concatenate([vt, jnp.ones((ONES_ROWS, vt.shape[1]), BF16)], axis=0)


AUG_TERMS = 3


def _lane_terms(lane, first, terms):
    out = jnp.zeros(terms[0].shape, F32)
    for t, v in enumerate(terms):
        out = jnp.where(lane == first + t, v, out)
    return out


def _flash_t_kernel(*refs, mode, tq, lam_init):
    tk = tq
    if mode == "diff":
        (slopes_ref, lamv_ref, g_ref, q_ref, k_ref, vt_ref, o_ref,
         m_sc, acc_sc, q2_sc, s0_sc, s1_sc, mu_sc, kaug_sc, bdiag_sc) = refs
    else:
        cum_ref, q_ref, k_ref, vt_ref, o_ref, m_sc, acc_sc, q2_sc, s0_sc, s1_sc, mu_sc, kaug_sc, bdiag_sc = refs
    pair = pl.program_id(0)
    qi = pl.program_id(2)
    t_all = kaug_sc.shape[0]
    s_bufs = (s0_sc, s1_sc)

    @pl.when((pl.program_id(1) == 0) & (qi == 0))
    def _():
        row = lax.broadcasted_iota(jnp.int32, (tk, tq), 0)
        col = lax.broadcasted_iota(jnp.int32, (tk, tq), 1)
        lane_q = lax.broadcasted_iota(jnp.int32, (2 * tq, LANES), 1)
        first_q = jnp.where(lax.broadcasted_iota(jnp.int32, (2 * tq, LANES), 0) < tq, 0, AUG_TERMS)
        if mode == "diff":
            slope = slopes_ref[pair]
            visible = (row >> 6) <= (col >> 6)
            bdiag_sc[...] = jnp.where(visible, -2.0 * slope * jnp.maximum(row - col, 0).astype(F32), NEG_INF)
            q2_sc[:, LANES:] = jnp.where(lane_q < AUG_TERMS, 1.0, 0.0).astype(BF16)
            for c0 in range(0, t_all, tk):
                j = c0 + lax.broadcasted_iota(jnp.int32, (tk, LANES), 0)
                lane = lax.broadcasted_iota(jnp.int32, (tk, LANES), 1)
                terms = [((j >> 8) << 8).astype(F32), (((j >> 4) & 15) << 4).astype(F32), (j & 15).astype(F32)]
                kaug_sc[c0:c0 + tk, :] = (slope * _lane_terms(lane, 0, terms)).astype(BF16)
        else:
            bdiag_sc[...] = jnp.where(row <= col, 0.0, NEG_INF)
            mine = (lane_q >= first_q) & (lane_q < first_q + AUG_TERMS)
            q2_sc[:, LANES:] = jnp.where(mine, -1.0, 0.0).astype(BF16)

    if mode == "fox":
        @pl.when(qi == 0)
        def _():
            for c0 in range(0, t_all, tk):
                lane = lax.broadcasted_iota(jnp.int32, (tk, LANES), 1)
                feat = jnp.zeros((tk, LANES), F32)
                for a in range(2):
                    r = cum_ref[0, 2 * pair + a, :, c0:c0 + tk]
                    ck = jnp.broadcast_to(r, (LANES, tk)).T
                    hi = ck.astype(BF16).astype(F32)
                    mid = (ck - hi).astype(BF16).astype(F32)
                    lo = ((ck - hi) - mid).astype(BF16).astype(F32)
                    feat = feat + _lane_terms(lane, AUG_TERMS * a, [hi, mid, lo])
                kaug_sc[c0:c0 + tk, :] = feat.astype(BF16)

    q2_sc[:, :LANES] = _stack_pair(q_ref[0])
    m_sc[...] = jnp.full(m_sc.shape, NEG_INF, F32)
    acc_sc[...] = jnp.zeros(acc_sc.shape, F32)

    def stage_a(k0, slot, diag):
        kk = jnp.concatenate([k_ref[0, pl.ds(k0, tk), :], kaug_sc[pl.ds(k0, tk), :]], axis=1)
        st = lax.dot_general(kk, q2_sc[...], _NT, preferred_element_type=F32)
        for a in range(2):
            sa = st[:, a * tq:(a + 1) * tq]
            if diag:
                sa = sa + bdiag_sc[...]
            s_bufs[slot][:, a * tq:(a + 1) * tq] = sa
            mu_sc[slot, :, a * tq:(a + 1) * tq] = jnp.max(sa, axis=0, keepdims=True)

    def stage_b(k0, slot):
        vt = vt_ref[0, :, pl.ds(k0, tk)]
        for a in range(2):
            m_prev = m_sc[a]
            m_next = jnp.maximum(m_prev, mu_sc[slot, :, a * tq:(a + 1) * tq])
            p = jnp.exp(s_bufs[slot][:, a * tq:(a + 1) * tq] - m_next)
            alpha = jnp.exp(m_prev - m_next)
            vta = _with_ones(vt if mode == "diff" else vt[a * HEAD_DIM:(a + 1) * HEAD_DIM])
            acc_sc[a] = alpha * acc_sc[a] + jnp.dot(vta, p.astype(BF16), preferred_element_type=F32)
            m_sc[a] = m_next

    blk = lambda j: pl.multiple_of(j * tk, tk)
    pl.when(qi == 0)(lambda: stage_a(0, 0, True))
    pl.when(qi > 0)(lambda: stage_a(0, 0, False))

    def two_blocks(i, carry):
        stage_a(blk(2 * i + 1), 1, False)
        stage_b(blk(2 * i), 0)
        stage_a(blk(2 * i + 2), 0, False)
        stage_b(blk(2 * i + 1), 1)
        return carry

    lax.fori_loop(0, (qi - 1) // 2, two_blocks, 0)

    @pl.when(qi == 0)
    def _():
        stage_b(0, 0)

    @pl.when(qi % 2 == 1)
    def _():
        stage_a(blk(qi), 1, True)
        stage_b(blk(qi - 1), 0)
        stage_b(blk(qi), 1)

    @pl.when((qi % 2 == 0) & (qi > 0))
    def _():
        stage_a(blk(qi - 1), 1, False)
        stage_b(blk(qi - 2), 0)
        stage_a(blk(qi), 0, True)
        stage_b(blk(qi - 1), 1)
        stage_b(blk(qi), 0)

    d = acc_sc.shape[1] - ONES_ROWS
    o0 = acc_sc[0, :d, :] / acc_sc[0, d:d + 1, :]
    o1 = acc_sc[1, :d, :] / acc_sc[1, d:d + 1, :]
    if mode == "diff":
        lv = lamv_ref[...]
        lam = (jnp.exp(jnp.sum(lv[0:1] * lv[1:2], axis=1, keepdims=True))
               - jnp.exp(jnp.sum(lv[2:3] * lv[3:4], axis=1, keepdims=True)) + lam_init)
        oa = o0 - lam * o1
        ms = jnp.mean(oa * oa, axis=0, keepdims=True)
        y = (oa * lax.rsqrt(ms + NORM_EPS)).T * g_ref[...]
        o_ref[0] = (y * (1.0 - lam_init)).astype(BF16)
    else:
        o_ref[0] = jnp.concatenate([o0, o1], axis=0).T.astype(BF16)


def _flash_t(mode, q, k, vt, extras, *, tq, lam_init=0.0):
    b, t, width = q.shape
    pairs = width // LANES
    qspec = pl.BlockSpec((1, tq, LANES), lambda p, bi, i: (bi, i, p))
    kspec = pl.BlockSpec((1, t, LANES), lambda p, bi, i: (bi, 0, p))
    vspec = pl.BlockSpec((1, LANES, t), lambda p, bi, i: (bi, p, 0))
    scratch = [pltpu.VMEM((2, 1, tq), F32),
               pltpu.VMEM((2, (LANES if mode == "diff" else HEAD_DIM) + ONES_ROWS, tq), F32),
               pltpu.VMEM((2 * tq, 2 * LANES), BF16),
               pltpu.VMEM((tq, 2 * tq), F32), pltpu.VMEM((tq, 2 * tq), F32),
               pltpu.VMEM((2, 1, 2 * tq), F32),
               pltpu.VMEM((t, LANES), BF16),
               pltpu.VMEM((tq, tq), F32)]
    if mode == "diff":
        slopes, lamv, g = extras
        extra_specs = [pl.BlockSpec(memory_space=pltpu.SMEM),
                       pl.BlockSpec(lamv.shape, lambda p, bi, i: (0, 0)),
                       pl.BlockSpec(g.shape, lambda p, bi, i: (0, 0))]
    else:
        extras = (extras[0][:, :, None, :],)
        extra_specs = [pl.BlockSpec((1,) + extras[0].shape[1:], lambda p, bi, i: (bi, 0, 0, 0))]
    return pl.pallas_call(
        functools.partial(_flash_t_kernel, mode=mode, tq=tq, lam_init=lam_init),
        grid=(pairs, b, t // tq),
        in_specs=extra_specs + [qspec, kspec, vspec],
        out_specs=qspec,
        out_shape=jax.ShapeDtypeStruct(q.shape, BF16),
        scratch_shapes=scratch,
        compiler_params=_params(3),
        name="flash_t_" + mode,
    )(*extras, q, k, vt)


def _band_t_kernel(c_ref, q_ref, k_ref, vt_ref, o_ref, s0_sc, s1_sc, mu_sc, bias_sc, *, tq, t):
    window = BAND_PAST + tq
    n_q = t // tq
    s_bufs = (s0_sc, s1_sc)

    @pl.when(pl.program_id(1) == 0)
    def _():
        row = lax.broadcasted_iota(jnp.int32, (tq, window), 0)
        col = lax.broadcasted_iota(jnp.int32, (tq, window), 1)
        visible = ((col >> 6) >= (row >> 6)) & ((col >> 6) <= (row >> 6) + BAND_PAST // CHUNK)
        for a in range(2):
            c = jnp.broadcast_to(c_ref[0, a:a + 1, :], (tq, c_ref.shape[2]))
            bias = pltpu.roll(c, 0, 1, stride=1, stride_axis=0)[:, :window]
            bias_sc[a] = jnp.where(visible, bias, NEG_INF).T

    lane_q = lax.broadcasted_iota(jnp.int32, (2 * tq, LANES), 1)
    q_extra = jnp.where(lane_q == 0, 1.0, 0.0).astype(BF16)

    def stage_a(qb, slot):
        w0 = pl.multiple_of(qb * tq, tq)
        q2 = jnp.concatenate([_stack_pair(q_ref[0, pl.ds(w0, tq), :]), q_extra], axis=1)
        pad_row = (w0 + lax.broadcasted_iota(jnp.int32, (window, LANES), 0)) < BAND_PAST
        lane_k = lax.broadcasted_iota(jnp.int32, (window, LANES), 1)
        k_extra = jnp.where(pad_row & (lane_k == 0), NEG_INF, 0.0).astype(BF16)
        kk = jnp.concatenate([k_ref[0, pl.ds(w0, window), :], k_extra], axis=1)
        st = lax.dot_general(kk, q2, _NT, preferred_element_type=F32)
        for a in range(2):
            u = st[:, a * tq:(a + 1) * tq] + bias_sc[a]
            s_bufs[slot][:, a * tq:(a + 1) * tq] = u
            mu_sc[slot, :, a * tq:(a + 1) * tq] = jnp.max(u, axis=0, keepdims=True)

    def stage_b(qb, slot):
        w0 = pl.multiple_of(qb * tq, tq)
        vt = vt_ref[0, :, pl.ds(w0, window)]
        outs = []
        for a in range(2):
            p = jnp.exp(s_bufs[slot][:, a * tq:(a + 1) * tq] - mu_sc[slot, :, a * tq:(a + 1) * tq])
            vta = _with_ones(vt[a * HEAD_DIM:(a + 1) * HEAD_DIM])
            r = jnp.dot(vta, p.astype(BF16), preferred_element_type=F32)
            outs.append(r[:HEAD_DIM] / r[HEAD_DIM:HEAD_DIM + 1])
        o_ref[0, pl.ds(w0, tq), :] = jnp.concatenate(outs, axis=0).T.astype(BF16)

    stage_a(0, 0)

    def two_blocks(i, carry):
        stage_a(2 * i + 1, 1)
        stage_b(2 * i, 0)
        stage_a(2 * i + 2, 0)
        stage_b(2 * i + 1, 1)
        return carry

    lax.fori_loop(0, n_q // 2 - 1, two_blocks, 0)
    stage_a(n_q - 1, 1)
    stage_b(n_q - 2, 0)
    stage_b(n_q - 1, 1)


def _band_t(q, k_pad, vt_pad, cvec, *, tq):
    b, t, width = q.shape
    pairs = width // LANES
    window = BAND_PAST + tq
    assert (t // tq) % 2 == 0 and t // tq >= 2
    qspec = pl.BlockSpec((1, t, LANES), lambda p, bi: (bi, 0, p))
    kspec = pl.BlockSpec((1, k_pad.shape[1], LANES), lambda p, bi: (bi, 0, p))
    vspec = pl.BlockSpec((1, LANES, vt_pad.shape[2]), lambda p, bi: (bi, p, 0))
    cspec = pl.BlockSpec((1, 2, cvec.shape[2]), lambda p, bi: (p, 0, 0))
    return pl.pallas_call(
        functools.partial(_band_t_kernel, tq=tq, t=t),
        grid=(pairs, b),
        in_specs=[cspec, qspec, kspec, vspec],
        out_specs=qspec,
        out_shape=jax.ShapeDtypeStruct(q.shape, BF16),
        scratch_shapes=[pltpu.VMEM((window, 2 * tq), F32), pltpu.VMEM((window, 2 * tq), F32),
                        pltpu.VMEM((2, 1, 2 * tq), F32), pltpu.VMEM((2, window, tq), F32)],
        compiler_params=_params(2),
        name="band_t_attn",
    )(cvec, q, k_pad, vt_pad)


def _proj_odd_kernel(x_ref, g_ref, w_ref, wt_ref, q_ref, k_ref, v_ref, kt_ref, vt_ref, *,
                     tiles, pad_tiles, v_feature_major):
    r = pl.program_id(1)

    if pad_tiles:
        @pl.when(r < pad_tiles)
        def _():
            k_ref[...] = jnp.zeros(k_ref.shape, BF16)
            v_ref[...] = jnp.zeros(v_ref.shape, BF16)

    @pl.when(r >= pad_tiles)
    def _():
        h = _rms(x_ref[...], g_ref[...]).astype(BF16)

        def seg(i):
            return jnp.dot(h, w_ref[:, i * D_MODEL:(i + 1) * D_MODEL], preferred_element_type=F32)

        def seg_t(i):
            return lax.dot_general(wt_ref[i], h, _NT, preferred_element_type=F32)

        is_tail = r == pad_tiles + tiles - 1
        q_ref[...] = (seg(0) * QK_SCALE).astype(BF16)
        k_ref[...] = seg(1).astype(BF16)
        if v_feature_major:
            vzt = seg_t(1)
            v_ref[0] = vzt.astype(BF16)

            @pl.when(is_tail)
            def _():
                kt_ref[0] = seg_t(0)
                vt_ref[0] = vzt
        else:
            v_ref[...] = seg(2).astype(BF16)

            @pl.when(is_tail)
            def _():
                kt_ref[0] = seg_t(0)
                vt_ref[0] = seg_t(1)


def _proj_odd(x, g, w, w_t, *, tiles, pad_tiles, v_feature_major):
    m = x.shape[0]
    tm = min(ROW_TILE, m)
    nb = m // tm // tiles
    steps = pad_tiles + tiles
    src = lambda b, r: (b * tiles + jnp.maximum(r - pad_tiles, 0), 0)
    row = pl.BlockSpec((tm, D_MODEL), src)
    padded_row = pl.BlockSpec((tm, D_MODEL), lambda b, r: (b * steps + r, 0))
    tail = pl.BlockSpec((1, D_MODEL, tm), lambda b, r: (b, 0, 0))
    bfo = jax.ShapeDtypeStruct((m, D_MODEL), BF16)
    padded = jax.ShapeDtypeStruct((nb * steps * tm, D_MODEL), BF16)
    tailo = jax.ShapeDtypeStruct((nb, D_MODEL, tm), F32)
    if v_feature_major:
        vo = jax.ShapeDtypeStruct((nb, D_MODEL, steps * tm), BF16)
        vspec = pl.BlockSpec((1, D_MODEL, tm), lambda b, r: (b, 0, r))
    else:
        vo, vspec = padded, padded_row
    return pl.pallas_call(
        functools.partial(_proj_odd_kernel, tiles=tiles, pad_tiles=pad_tiles, v_feature_major=v_feature_major),
        grid=(nb, steps),
        in_specs=[row, _const_spec((1, D_MODEL)), _const_spec(w.shape), _const_spec(w_t.shape)],
        out_specs=[row, padded_row, vspec, tail, tail],
        out_shape=[bfo, padded, vo, tailo, tailo],
        compiler_params=_params(2),
        name="proj_odd",
    )(x, g, w, w_t)


def _post_kernel(*refs, n_o):
    o_refs = refs[:n_o]
    x_ref, wo_ref, gpm_ref, gpre_ref, wup_ref, wdn_ref, gpf_ref, out_ref = refs[n_o:]
    o = o_refs[0][...] if n_o == 1 else jnp.concatenate([r[...] for r in o_refs], axis=1)
    mixed = jnp.dot(o, wo_ref[...], preferred_element_type=F32)
    x1 = x_ref[...] + _rms(mixed, gpm_ref[...])
    h = _rms(x1, gpre_ref[...]).astype(BF16)
    acc = jnp.zeros(x1.shape, F32)
    for c in range(D_FF // FF_CHUNK):
        u = jnp.dot(h, wup_ref[:, c * FF_CHUNK:(c + 1) * FF_CHUNK], preferred_element_type=F32)
        u = jnp.maximum(u, 0.0)
        acc = acc + jnp.dot((u * u).astype(BF16), wdn_ref[c * FF_CHUNK:(c + 1) * FF_CHUNK, :],
                            preferred_element_type=F32)
    out_ref[...] = x1 + _rms(acc, gpf_ref[...])


def _post(o_list, x, wo, gpm, gpre, wup, wdn, gpf):
    m = x.shape[0]
    tm = min(ROW_TILE, m)
    row = lambda w: pl.BlockSpec((tm, w), lambda i: (i, 0))
    gspec = _const_spec((1, D_MODEL))
    return pl.pallas_call(
        functools.partial(_post_kernel, n_o=len(o_list)),
        grid=(m // tm,),
        in_specs=[row(o.shape[1]) for o in o_list]
                 + [row(D_MODEL), _const_spec(wo.shape), gspec, gspec,
                    _const_spec(wup.shape), _const_spec(wdn.shape), gspec],
        out_specs=row(D_MODEL),
        out_shape=jax.ShapeDtypeStruct((m, D_MODEL), F32),
        compiler_params=_params(1),
        name="post_mix_mlp",
    )(*o_list, x, wo, gpm, gpre, wup, wdn, gpf)


def _band_bias_vectors(table):
    n = 2 * BAND_PAST
    m = np.arange(n)
    d = np.where(m <= n - REL_CLIP, -m, n - m)
    idx = np.clip(d + BAND_PAST, -REL_CLIP, REL_CLIP) + REL_CLIP
    return jnp.take(table.astype(F32), jnp.asarray(idx, jnp.int32), axis=1).reshape(H_C // 2, 2, n)


def kernel(x_prompt, x_sample, cache_a_k, cache_a_v, cache_b_k, cache_b_v, cache_b_logf, cache_c_k, cache_c_v,
           w_in_even, b_forget, lam_q1, lam_k1, lam_q2, lam_k2, subln_g, w_out_even, w_in_odd, rel_bias,
           w_out_odd, g_pre_mix, g_post_mix, g_pre_ffn, g_post_ffn, w_ffn_up, w_ffn_down):
    b_p, t_p, _ = x_prompt.shape
    b_s, t_s, _ = x_sample.shape
    past = cache_b_logf.shape[2]
    m_p, m_s = b_p * t_p, b_s * t_s
    xp = x_prompt.reshape(m_p, D_MODEL)
    xs = x_sample.reshape(m_s, D_MODEL)
    gvec = lambda a, l: a[l].reshape(1, D_MODEL)

    lam_init = 0.8 - 0.6 * math.exp(-0.3 * 0)
    w_in_t = w_in_even[0].T
    w_t = w_in_t[:N_SEG * A_W].astype(BF16)
    wft = jnp.pad(w_in_t[N_SEG * A_W:], ((0, 16 - H_B), (0, 0))).astype(BF16)
    bf_col = b_forget[0].reshape(H_B, 1)
    slopes = 2.0 ** (-8.0 * jnp.arange(1, H_A + 1, dtype=F32) / H_A)
    lamv = jnp.stack([lam_q1[0], lam_k1[0], lam_q2[0], lam_k2[0]]).astype(F32)
    sub_g = subln_g[0].reshape(1, 2 * HEAD_DIM)
    wo0 = w_out_even[0].astype(BF16)
    wup0, wdn0 = w_ffn_up[0].astype(BF16), w_ffn_down[0].astype(BF16)

    def even_layer(x, bsz, t, caches):
        m = x.shape[0]
        tm = min(ROW_TILE, m)
        (qa, ka, kab, va, vab, qb, kb, kbb, vb, vbb, lft) = _proj_even(
            x, gvec(g_pre_mix, 0), w_t, wft, bf_col,
            tiles_per_batch=(t // tm) if caches is None else None)
        r3 = lambda a: a.reshape(bsz, t, A_W)
        lft = lft.transpose(1, 0, 2).reshape(H_B, bsz, t).transpose(1, 0, 2)
        tok_head = lambda a: a.reshape(1, bsz, t, H_A, 2 * HEAD_DIM)
        if caches is None:
            oa = _flash_t("diff", r3(qa), r3(kab), vab, (slopes, lamv, sub_g), tq=512, lam_init=lam_init)
            ob = _flash_t("fox", r3(qb), r3(kbb), vbb, (_cumsum_time(lft),), tq=512)
            time_minor = lambda a: a.reshape(bsz, H_B, HEAD_DIM, t).transpose(0, 3, 1, 2)[None]
            new = (tok_head(ka), tok_head(va), time_minor(kb), time_minor(vb), lft.transpose(0, 2, 1)[None])
        else:
            c_ka, c_va, c_kb, c_vb, c_lf = caches
            p_len = c_ka.shape[1]
            t_all = p_len + t
            t_pad = -(-t_all // LANES) * LANES
            assert p_len % CHUNK == 0 and t <= CHUNK
            lf_all = jnp.concatenate([c_lf.astype(F32).transpose(0, 2, 1), lft], axis=2)
            cum = _cumsum_time(jnp.pad(lf_all, ((0, 0), (0, 0), (0, t_pad - t_all))))
            tok_head_rows = lambda c: c.reshape(bsz, p_len * H_A, 2 * HEAD_DIM)
            feat_major = lambda c: c.transpose(0, 2, 3, 1).reshape(bsz, A_W, p_len)
            oa = _sample_attn("diff", r3(qa), tok_head_rows(c_ka), tok_head_rows(c_va), r3(kab), r3(vab),
                              (slopes, lamv, sub_g), past=p_len, lam_init=lam_init)
            ob = _sample_attn("fox", r3(qb), feat_major(c_kb), feat_major(c_vb), r3(kbb), r3(vbb),
                              (cum,), past=p_len)
            new = (tok_head(ka), tok_head(va), kb.reshape(1, bsz, t, H_B, HEAD_DIM),
                   vb.reshape(1, bsz, t, H_B, HEAD_DIM), lft.transpose(0, 2, 1)[None])
        x_out = _post([oa.reshape(m, A_W), ob.reshape(m, A_W)], x, wo0, gvec(g_post_mix, 0),
                      gvec(g_pre_ffn, 0), wup0, wdn0, gvec(g_post_ffn, 0))
        return x_out, new

    xp, new_p_even = even_layer(xp, b_p, t_p, None)
    xs, new_s_even = even_layer(xs, b_s, t_s, (cache_a_k[0], cache_a_v[0], cache_b_k[0], cache_b_v[0],
                                              cache_b_logf[0]))

    w_odd = w_in_odd[0].astype(BF16)
    w_odd_t = jnp.stack([w_in_odd[0][:, s * D_MODEL:(s + 1) * D_MODEL].T for s in (1, 2)]).astype(BF16)
    cvec = _band_bias_vectors(rel_bias[0])
    to_cache = lambda a: a.reshape(a.shape[0], H_C, HEAD_DIM, a.shape[2]).transpose(0, 3, 1, 2)[None]
    from_cache = lambda c: c.transpose(0, 2, 3, 1).reshape(c.shape[0], D_MODEL, c.shape[1])
    wo1 = w_out_odd[0].astype(BF16)
    wup1, wdn1 = w_ffn_up[1].astype(BF16), w_ffn_down[1].astype(BF16)

    def odd_layer(x, bsz, t, caches):
        m = x.shape[0]
        tm = min(ROW_TILE, m)
        r3 = lambda a: a.reshape(bsz, t, D_MODEL)
        if caches is None:
            assert BAND_PAST % tm == 0 and tm == min(BAND_PAST, t)
            q, kp, vtp, k_tail, v_tail = _proj_odd(x, gvec(g_pre_mix, 1), w_odd, w_odd_t, tiles=t // tm,
                                                   pad_tiles=BAND_PAST // tm, v_feature_major=True)
            o = _band_t(r3(q), kp.reshape(bsz, BAND_PAST + t, D_MODEL), vtp, cvec, tq=256)
            new = (to_cache(k_tail), to_cache(v_tail))
        else:
            c_k, c_v = caches
            assert c_k.shape[1] == BAND_PAST and past % CHUNK == 0 and past >= BAND_PAST and t <= CHUNK
            q, k, v, k_tail, v_tail = _proj_odd(x, gvec(g_pre_mix, 1), w_odd, w_odd_t, tiles=m // tm,
                                                pad_tiles=0, v_feature_major=False)
            o = _sample_attn("band", r3(q), from_cache(c_k), from_cache(c_v), r3(k), r3(v), (cvec,), past=past)
            upd = lambda c, tail: to_cache(jnp.concatenate(
                [from_cache(c)[:, :, t:], tail.reshape(D_MODEL, bsz, t).transpose(1, 0, 2)], axis=2))
            new = (upd(c_k, k_tail), upd(c_v, v_tail))
        x_out = _post([o.reshape(m, D_MODEL)], x, wo1, gvec(g_post_mix, 1), gvec(g_pre_ffn, 1),
                      wup1, wdn1, gvec(g_post_ffn, 1))
        return x_out, new

    xp, new_p_odd = odd_layer(xp, b_p, t_p, None)
    xs, new_s_odd = odd_layer(xs, b_s, t_s, (cache_c_k[0], cache_c_v[0]))

    return (xp.reshape(b_p, t_p, D_MODEL), xs.reshape(b_s, t_s, D_MODEL),
            *new_p_even, *new_p_odd, *new_s_even, *new_s_odd)
```

```python
import functools
import math

import numpy as np
import jax
import jax.numpy as jnp
from jax import lax
from jax.experimental import pallas as pl
from jax.experimental.pallas import tpu as pltpu

F32 = jnp.float32
BF16 = jnp.bfloat16

D_MODEL = 1024
HEAD_DIM = 64
CHUNK = 64
H_A = 4
H_B = 8
H_C = 16
A_W = 512
N_SEG = 6
BAND_PAST = 512
REL_CLIP = 256
D_FF = 4 * D_MODEL
NORM_EPS = 1e-6
NEG_INF = -1e30
QK_SCALE = HEAD_DIM ** -0.5

LANES = 128
ROW_TILE = 512
FF_CHUNK = 1024
VMEM_LIMIT = 56 * 1024 * 1024

_NT = (((1,), (1,)), ((), ()))


def _rms(x, g):
    ms = jnp.mean(x * x, axis=-1, keepdims=True)
    return x * lax.rsqrt(ms + NORM_EPS) * g


def _log_sigmoid(x):
    t = -x
    return -(jnp.maximum(t, 0.0) + jnp.log1p(jnp.exp(-jnp.abs(t))))


def _const_spec(shape):
    nd = len(shape)
    return pl.BlockSpec(shape, lambda *_: (0,) * nd, pipeline_mode=pl.Buffered(1))


def _params(n_axes):
    return pltpu.CompilerParams(dimension_semantics=("arbitrary",) * n_axes,
                                vmem_limit_bytes=VMEM_LIMIT)


def _proj_even_kernel(x_ref, g_ref, wt_ref, wft_ref, bft_ref,
                      qa_ref, ka_ref, kab_ref, va_ref, vab_ref,
                      qb_ref, kb_ref, kbb_ref, vb_ref, vbb_ref, lft_ref, *, feature_major):
    h = _rms(x_ref[...], g_ref[...]).astype(BF16)
    tm = h.shape[0]

    def seg(i):
        return lax.dot_general(h, wt_ref[i * A_W:(i + 1) * A_W, :], _NT, preferred_element_type=F32)

    def seg_t(i):
        return lax.dot_general(wt_ref[i * A_W:(i + 1) * A_W, :], h, _NT, preferred_element_type=F32)

    def store_token_head_rows(ref, z):
        for hd in range(H_A):
            ref[pl.ds(hd, tm, stride=H_A), :] = z[:, hd * LANES:(hd + 1) * LANES]

    qa_ref[...] = (seg(0) * QK_SCALE).astype(BF16)
    z = seg(1)
    store_token_head_rows(ka_ref, z)
    kab_ref[...] = z.astype(BF16)
    z = seg(2)
    store_token_head_rows(va_ref, z)
    if feature_major:
        vab_ref[0] = z.T.astype(BF16)
    else:
        vab_ref[...] = z.astype(BF16)
    qb_ref[...] = (seg(3) * QK_SCALE).astype(BF16)
    z = seg(4)
    kbb_ref[...] = z.astype(BF16)
    if feature_major:
        kb_ref[0] = z.T
        zt = seg_t(5)
        vb_ref[0] = zt
        vbb_ref[0] = zt.astype(BF16)
    else:
        kb_ref[...] = z
        z = seg(5)
        vb_ref[...] = z
        vbb_ref[...] = z.astype(BF16)
    fzt = lax.dot_general(wft_ref[...], h, _NT, preferred_element_type=F32)
    lft_ref[0] = _log_sigmoid(fzt[:H_B] + bft_ref[...])


def _proj_even(x, g, w_t, wft, bf_col, *, tiles_per_batch=None):
    m = x.shape[0]
    tm = min(ROW_TILE, m)
    n = m // tm
    row = lambda w: pl.BlockSpec((tm, w), lambda i: (i, 0))
    bfo = jax.ShapeDtypeStruct((m, A_W), BF16)
    tok_head = jax.ShapeDtypeStruct((m * H_A, LANES), F32)
    tok_head_spec = pl.BlockSpec((tm * H_A, LANES), lambda i: (i, 0))
    if tiles_per_batch:
        tpb = tiles_per_batch
        fm = lambda dt: jax.ShapeDtypeStruct((n // tpb, A_W, tpb * tm), dt)
        fm_spec = pl.BlockSpec((1, A_W, tm), lambda i: (i // tpb, 0, i % tpb))
        outs = [(bfo, row(A_W)), (tok_head, tok_head_spec), (bfo, row(A_W)), (tok_head, tok_head_spec),
                (fm(BF16), fm_spec), (bfo, row(A_W)), (fm(F32), fm_spec), (bfo, row(A_W)),
                (fm(F32), fm_spec), (fm(BF16), fm_spec)]
    else:
        f32o = jax.ShapeDtypeStruct((m, A_W), F32)
        outs = [(bfo, row(A_W)), (tok_head, tok_head_spec), (bfo, row(A_W)), (tok_head, tok_head_spec),
                (bfo, row(A_W)), (bfo, row(A_W)), (f32o, row(A_W)), (bfo, row(A_W)),
                (f32o, row(A_W)), (bfo, row(A_W))]
    outs.append((jax.ShapeDtypeStruct((n, H_B, tm), F32), pl.BlockSpec((1, H_B, tm), lambda i: (i, 0, 0))))
    return pl.pallas_call(
        functools.partial(_proj_even_kernel, feature_major=bool(tiles_per_batch)),
        grid=(n,),
        in_specs=[row(D_MODEL), _const_spec((1, D_MODEL)), _const_spec(w_t.shape),
                  _const_spec(wft.shape), _const_spec(bf_col.shape)],
        out_specs=[s for _, s in outs],
        out_shape=[o for o, _ in outs],
        compiler_params=_params(1),
        name="proj_even",
    )(x, g, w_t, wft, bf_col)


def _cumsum_kernel(x_ref, o_ref):
    x = x_ref[0]
    t = x.shape[1]
    col = lax.broadcasted_iota(jnp.int32, x.shape, 1)
    s = 1
    while s < t:
        x = x + jnp.where(col >= s, pltpu.roll(x, s, 1), 0.0)
        s *= 2
    o_ref[0] = x


def _cumsum_time(x):
    b, h, t = x.shape
    spec = pl.BlockSpec((1, h, t), lambda i: (i, 0, 0))
    return pl.pallas_call(
        _cumsum_kernel, grid=(b,), in_specs=[spec], out_specs=spec,
        out_shape=jax.ShapeDtypeStruct(x.shape, F32),
        compiler_params=_params(1), name="cumsum_time",
    )(x)


def _stack_pair(q):
    q32 = q.astype(F32)
    lo = lax.broadcasted_iota(jnp.int32, q32.shape, 1) < HEAD_DIM
    return jnp.concatenate([jnp.where(lo, q32, 0.0), jnp.where(lo, 0.0, q32)], axis=0).astype(BF16)


def _sample_attn_kernel(*refs, mode, t, past, lam_init):
    if mode == "diff":
        slopes_ref, lamv_ref, g_ref, q_ref, kc_ref, vc_ref, kn_ref, vn_ref, o_ref = refs
    elif mode == "fox":
        cum_ref, q_ref, kc_ref, vc_ref, kn_ref, vn_ref, o_ref = refs
    else:
        c_ref, q_ref, kc_ref, vc_ref, kn_ref, vn_ref, o_ref = refs
    pairs = q_ref.shape[2] // LANES
    n_c = vc_ref.shape[1] // pairs if mode == "diff" else kc_ref.shape[2]
    row = lax.broadcasted_iota(jnp.int32, (t, n_c), 0)
    col = lax.broadcasted_iota(jnp.int32, (t, n_c), 1)
    row_n = lax.broadcasted_iota(jnp.int32, (t, t), 0)
    col_n = lax.broadcasted_iota(jnp.int32, (t, t), 1)
    lo = lax.broadcasted_iota(jnp.int32, (t, LANES), 1) < HEAD_DIM

    for p in range(pairs):
        lanes = slice(p * LANES, (p + 1) * LANES)
        q2 = _stack_pair(q_ref[0, :, lanes])
        kn = kn_ref[0, :, lanes]
        vn = vn_ref[0, :, lanes]
        if mode == "diff":
            kc = kc_ref[0, pl.ds(p, n_c, stride=pairs), :].astype(BF16)
            vc = vc_ref[0, pl.ds(p, n_c, stride=pairs), :].astype(BF16)
            s_c = lax.dot_general(q2, kc, _NT, preferred_element_type=F32)
        else:
            kc = kc_ref[0, lanes, :].astype(BF16)
            vc = vc_ref[0, lanes, :].astype(BF16)
            s_c = jnp.dot(q2, kc, preferred_element_type=F32)
        s_n = lax.dot_general(q2, kn, _NT, preferred_element_type=F32)

        outs = []
        for a in range(2):
            rows = slice(a * t, (a + 1) * t)
            if mode == "diff":
                slope = slopes_ref[p]
                t_c = s_c[rows] - slope * (past + row - col).astype(F32)
                t_n = s_n[rows] - slope * jnp.abs(row_n - col_n).astype(F32)
            elif mode == "fox":
                ck = cum_ref[0, 2 * p + a]
                t_c = s_c[rows] - ck[:, :n_c]
                t_n = jnp.where(col_n <= row_n, s_n[rows] - ck[:, n_c:n_c + t], NEG_INF)
            else:
                c = jnp.broadcast_to(c_ref[p, a:a + 1, :], (t, c_ref.shape[2]))
                bias = pltpu.roll(c, 0, 1, stride=1, stride_axis=0)
                t_c = s_c[rows] + bias[:, :n_c]
                t_n = s_n[rows] + bias[:, n_c:n_c + t]
            m = jnp.maximum(jnp.max(t_c, axis=1, keepdims=True), jnp.max(t_n, axis=1, keepdims=True))
            p_c = jnp.exp(t_c - m)
            p_n = jnp.exp(t_n - m)
            l = jnp.sum(p_c, axis=1, keepdims=True) + jnp.sum(p_n, axis=1, keepdims=True)
            if mode == "diff":
                acc = jnp.dot(p_c.astype(BF16), vc, preferred_element_type=F32)
            else:
                acc = lax.dot_general(p_c.astype(BF16), vc, _NT, preferred_element_type=F32)
            acc = acc + jnp.dot(p_n.astype(BF16), vn, preferred_element_type=F32)
            outs.append(acc / l)
        if mode == "diff":
            lv = lamv_ref[...]
            lam = (jnp.exp(jnp.sum(lv[0:1] * lv[1:2], axis=1, keepdims=True))
                   - jnp.exp(jnp.sum(lv[2:3] * lv[3:4], axis=1, keepdims=True)) + lam_init)
            oa = outs[0] - lam * outs[1]
            o_ref[0, :, lanes] = (_rms(oa, g_ref[...]) * (1.0 - lam_init)).astype(BF16)
        else:
            o_ref[0, :, lanes] = jnp.where(lo, outs[0], outs[1]).astype(BF16)


def _sample_attn(mode, q, kc, vc, kn, vn, extras, *, past, lam_init=0.0):
    b, t, width = q.shape
    whole = lambda a: pl.BlockSpec((1,) + a.shape[1:], lambda i: (i,) + (0,) * (a.ndim - 1))
    const = lambda a: pl.BlockSpec(a.shape, lambda i: (0,) * a.ndim)
    if mode == "diff":
        slopes, lamv, g = extras
        extra_specs = [pl.BlockSpec(memory_space=pltpu.SMEM), const(lamv), const(g)]
    elif mode == "fox":
        extras = (extras[0][:, :, None, :],)
        extra_specs = [whole(extras[0])]
    else:
        extra_specs = [const(extras[0])]
    return pl.pallas_call(
        functools.partial(_sample_attn_kernel, mode=mode, t=t, past=past, lam_init=lam_init),
        grid=(b,),
        in_specs=extra_specs + [whole(q), whole(kc), whole(vc), whole(kn), whole(vn)],
        out_specs=whole(q),
        out_shape=jax.ShapeDtypeStruct(q.shape, BF16),
        compiler_params=_params(1),
        name="sample_" + mode,
    )(*extras, q, kc, vc, kn, vn)


ONES_ROWS = 16


def _with_ones(vt):
    return jnp.concatenate([vt, jnp.ones((ONES_ROWS, vt.shape[1]), BF16)], axis=0)


AUG_TERMS = 3
FLASH_STREAMS = 2


def _lane_terms(lane, first, terms):
    out = jnp.zeros(terms[0].shape, F32)
    for t, v in enumerate(terms):
        out = jnp.where(lane == first + t, v, out)
    return out


def _flash_t_kernel(*refs, mode, tq, lam_init):
    tk = tq
    n_extra = 3 if mode == "diff" else 1
    extra_refs, (q_ref, k_ref, vt_ref, o_ref, m_sc, acc_sc, q2_sc, mu_sc, kaug_sc, bdiag_sc) = (
        refs[:n_extra], refs[n_extra:n_extra + 10])
    s_bufs = refs[n_extra + 10:]
    if mode == "diff":
        slopes_ref, lamv_ref, g_ref = extra_refs
    else:
        (cum_ref,) = extra_refs
    pair = pl.program_id(0)
    qi = pl.program_id(2)
    t_all = kaug_sc.shape[1]
    streams = range(FLASH_STREAMS)

    @pl.when((pl.program_id(1) == 0) & (qi == 0))
    def _():
        row = lax.broadcasted_iota(jnp.int32, (tk, tq), 0)
        col = lax.broadcasted_iota(jnp.int32, (tk, tq), 1)
        lane_q = lax.broadcasted_iota(jnp.int32, (2 * tq, LANES), 1)
        first_q = jnp.where(lax.broadcasted_iota(jnp.int32, (2 * tq, LANES), 0) < tq, 0, AUG_TERMS)
        if mode == "diff":
            slope = slopes_ref[pair]
            visible = (row >> 6) <= (col >> 6)
            bdiag_sc[...] = jnp.where(visible, -2.0 * slope * jnp.maximum(row - col, 0).astype(F32), NEG_INF)
            for s in streams:
                q2_sc[s, :, LANES:] = jnp.where(lane_q < AUG_TERMS, 1.0, 0.0).astype(BF16)
            for c0 in range(0, t_all, tk):
                j = c0 + lax.broadcasted_iota(jnp.int32, (tk, LANES), 0)
                lane = lax.broadcasted_iota(jnp.int32, (tk, LANES), 1)
                terms = [((j >> 8) << 8).astype(F32), (((j >> 4) & 15) << 4).astype(F32), (j & 15).astype(F32)]
                kaug_sc[0, c0:c0 + tk, :] = (slope * _lane_terms(lane, 0, terms)).astype(BF16)
        else:
            bdiag_sc[...] = jnp.where(row <= col, 0.0, NEG_INF)
            mine = (lane_q >= first_q) & (lane_q < first_q + AUG_TERMS)
            for s in streams:
                q2_sc[s, :, LANES:] = jnp.where(mine, -1.0, 0.0).astype(BF16)

    if mode == "fox":
        @pl.when(qi == 0)
        def _():
            for s in streams:
                for c0 in range(0, t_all, tk):
                    lane = lax.broadcasted_iota(jnp.int32, (tk, LANES), 1)
                    feat = jnp.zeros((tk, LANES), F32)
                    for a in range(2):
                        r = cum_ref[s, 2 * pair + a, :, c0:c0 + tk]
                        ck = jnp.broadcast_to(r, (LANES, tk)).T
                        hi = ck.astype(BF16).astype(F32)
                        mid = (ck - hi).astype(BF16).astype(F32)
                        lo = ((ck - hi) - mid).astype(BF16).astype(F32)
                        feat = feat + _lane_terms(lane, AUG_TERMS * a, [hi, mid, lo])
                    kaug_sc[s, c0:c0 + tk, :] = feat.astype(BF16)

    for s in streams:
        q2_sc[s, :, :LANES] = _stack_pair(q_ref[s])
    m_sc[...] = jnp.full(m_sc.shape, NEG_INF, F32)
    acc_sc[...] = jnp.zeros(acc_sc.shape, F32)

    def stage_a(k0, slot, diag):
        for s in streams:
            kaug = kaug_sc[0 if mode == "diff" else s, pl.ds(k0, tk), :]
            kk = jnp.concatenate([k_ref[s, pl.ds(k0, tk), :], kaug], axis=1)
            st = lax.dot_general(kk, q2_sc[s], _NT, preferred_element_type=F32)
            for a in range(2):
                sa = st[:, a * tq:(a + 1) * tq]
                if diag:
                    sa = sa + bdiag_sc[...]
                s_bufs[2 * s + slot][:, a * tq:(a + 1) * tq] = sa
                mu_sc[s, slot, :, a * tq:(a + 1) * tq] = jnp.max(sa, axis=0, keepdims=True)

    def stage_b(k0, slot):
        for s in streams:
            vt = vt_ref[s, :, pl.ds(k0, tk)]
            for a in range(2):
                m_prev = m_sc[s, a]
                m_next = jnp.maximum(m_prev, mu_sc[s, slot, :, a * tq:(a + 1) * tq])
                p = jnp.exp(s_bufs[2 * s + slot][:, a * tq:(a + 1) * tq] - m_next)
                alpha = jnp.exp(m_prev - m_next)
                vta = _with_ones(vt if mode == "diff" else vt[a * HEAD_DIM:(a + 1) * HEAD_DIM])
                acc_sc[s, a] = alpha * acc_sc[s, a] + jnp.dot(vta, p.astype(BF16), preferred_element_type=F32)
                m_sc[s, a] = m_next

    blk = lambda j: pl.multiple_of(j * tk, tk)
    pl.when(qi == 0)(lambda: stage_a(0, 0, True))
    pl.when(qi > 0)(lambda: stage_a(0, 0, False))

    def two_blocks(i, carry):
        stage_a(blk(2 * i + 1), 1, False)
        stage_b(blk(2 * i), 0)
        stage_a(blk(2 * i + 2), 0, False)
        stage_b(blk(2 * i + 1), 1)
        return carry

    lax.fori_loop(0, (qi - 1) // 2, two_blocks, 0)

    @pl.when(qi == 0)
    def _():
        stage_b(0, 0)

    @pl.when(qi % 2 == 1)
    def _():
        stage_a(blk(qi), 1, True)
        stage_b(blk(qi - 1), 0)
        stage_b(blk(qi), 1)

    @pl.when((qi % 2 == 0) & (qi > 0))
    def _():
        stage_a(blk(qi - 1), 1, False)
        stage_b(blk(qi - 2), 0)
        stage_a(blk(qi), 0, True)
        stage_b(blk(qi - 1), 1)
        stage_b(blk(qi), 0)

    d = acc_sc.shape[2] - ONES_ROWS
    for s in streams:
        o0 = acc_sc[s, 0, :d, :] / acc_sc[s, 0, d:d + 1, :]
        o1 = acc_sc[s, 1, :d, :] / acc_sc[s, 1, d:d + 1, :]
        if mode == "diff":
            lv = lamv_ref[...]
            lam = (jnp.exp(jnp.sum(lv[0:1] * lv[1:2], axis=1, keepdims=True))
                   - jnp.exp(jnp.sum(lv[2:3] * lv[3:4], axis=1, keepdims=True)) + lam_init)
            oa = o0 - lam * o1
            ms = jnp.mean(oa * oa, axis=0, keepdims=True)
            y = (oa * lax.rsqrt(ms + NORM_EPS)).T * g_ref[...]
            o_ref[s] = (y * (1.0 - lam_init)).astype(BF16)
        else:
            o_ref[s] = jnp.concatenate([o0, o1], axis=0).T.astype(BF16)


def _flash_t(mode, q, k, vt, extras, *, tq, lam_init=0.0):
    b, t, width = q.shape
    pairs = width // LANES
    ns = FLASH_STREAMS
    assert b % ns == 0
    qspec = pl.BlockSpec((ns, tq, LANES), lambda p, bi, i: (bi, i, p))
    kspec = pl.BlockSpec((ns, t, LANES), lambda p, bi, i: (bi, 0, p))
    vspec = pl.BlockSpec((ns, LANES, t), lambda p, bi, i: (bi, p, 0))
    acc_rows = (LANES if mode == "diff" else HEAD_DIM) + ONES_ROWS
    scratch = [pltpu.VMEM((ns, 2, 1, tq), F32),
               pltpu.VMEM((ns, 2, acc_rows, tq), F32),
               pltpu.VMEM((ns, 2 * tq, 2 * LANES), BF16),
               pltpu.VMEM((ns, 2, 1, 2 * tq), F32),
               pltpu.VMEM((1 if mode == "diff" else ns, t, LANES), BF16),
               pltpu.VMEM((tq, tq), F32)]
    scratch += [pltpu.VMEM((tq, 2 * tq), F32)] * (2 * ns)
    if mode == "diff":
        slopes, lamv, g = extras
        extra_specs = [pl.BlockSpec(memory_space=pltpu.SMEM),
                       pl.BlockSpec(lamv.shape, lambda p, bi, i: (0, 0)),
                       pl.BlockSpec(g.shape, lambda p, bi, i: (0, 0))]
    else:
        extras = (extras[0][:, :, None, :],)
        extra_specs = [pl.BlockSpec((ns,) + extras[0].shape[1:], lambda p, bi, i: (bi, 0, 0, 0))]
    return pl.pallas_call(
        functools.partial(_flash_t_kernel, mode=mode, tq=tq, lam_init=lam_init),
        grid=(pairs, b // ns, t // tq),
        in_specs=extra_specs + [qspec, kspec, vspec],
        out_specs=qspec,
        out_shape=jax.ShapeDtypeStruct(q.shape, BF16),
        scratch_shapes=scratch,
        compiler_params=_params(3),
        name="flash_t_" + mode,
    )(*extras, q, k, vt)


def _band_t_kernel(c_ref, q_ref, k_ref, vt_ref, o_ref, s0_sc, s1_sc, mu_sc, bias_sc, *, tq, t):
    window = BAND_PAST + tq
    n_q = t // tq
    s_bufs = (s0_sc, s1_sc)

    @pl.when(pl.program_id(1) == 0)
    def _():
        row = lax.broadcasted_iota(jnp.int32, (tq, window), 0)
        col = lax.broadcasted_iota(jnp.int32, (tq, window), 1)
        visible = ((col >> 6) >= (row >> 6)) & ((col >> 6) <= (row >> 6) + BAND_PAST // CHUNK)
        for a in range(2):
            c = jnp.broadcast_to(c_ref[0, a:a + 1, :], (tq, c_ref.shape[2]))
            bias = pltpu.roll(c, 0, 1, stride=1, stride_axis=0)[:, :window]
            bias_sc[a] = jnp.where(visible, bias, NEG_INF).T

    lane_q = lax.broadcasted_iota(jnp.int32, (2 * tq, LANES), 1)
    q_extra = jnp.where(lane_q == 0, 1.0, 0.0).astype(BF16)

    def stage_a(qb, slot):
        w0 = pl.multiple_of(qb * tq, tq)
        q2 = jnp.concatenate([_stack_pair(q_ref[0, pl.ds(w0, tq), :]), q_extra], axis=1)
        pad_row = (w0 + lax.broadcasted_iota(jnp.int32, (window, LANES), 0)) < BAND_PAST
        lane_k = lax.broadcasted_iota(jnp.int32, (window, LANES), 1)
        k_extra = jnp.where(pad_row & (lane_k == 0), NEG_INF, 0.0).astype(BF16)
        kk = jnp.concatenate([k_ref[0, pl.ds(w0, window), :], k_extra], axis=1)
        st = lax.dot_general(kk, q2, _NT, preferred_element_type=F32)
        for a in range(2):
            u = st[:, a * tq:(a + 1) * tq] + bias_sc[a]
            s_bufs[slot][:, a * tq:(a + 1) * tq] = u
            mu_sc[slot, :, a * tq:(a + 1) * tq] = jnp.max(u, axis=0, keepdims=True)

    def stage_b(qb, slot):
        w0 = pl.multiple_of(qb * tq, tq)
        vt = vt_ref[0, :, pl.ds(w0, window)]
        outs = []
        for a in range(2):
            p = jnp.exp(s_bufs[slot][:, a * tq:(a + 1) * tq] - mu_sc[slot, :, a * tq:(a + 1) * tq])
            vta = _with_ones(vt[a * HEAD_DIM:(a + 1) * HEAD_DIM])
            r = jnp.dot(vta, p.astype(BF16), preferred_element_type=F32)
            outs.append(r[:HEAD_DIM] / r[HEAD_DIM:HEAD_DIM + 1])
        o_ref[0, pl.ds(w0, tq), :] = jnp.concatenate(outs, axis=0).T.astype(BF16)

    stage_a(0, 0)

    def two_blocks(i, carry):
        stage_a(2 * i + 1, 1)
        stage_b(2 * i, 0)
        stage_a(2 * i + 2, 0)
        stage_b(2 * i + 1, 1)
        return carry

    lax.fori_loop(0, n_q // 2 - 1, two_blocks, 0)
    stage_a(n_q - 1, 1)
    stage_b(n_q - 2, 0)
    stage_b(n_q - 1, 1)


def _band_t(q, k_pad, vt_pad, cvec, *, tq):
    b, t, width = q.shape
    pairs = width // LANES
    window = BAND_PAST + tq
    assert (t // tq) % 2 == 0 and t // tq >= 2
    qspec = pl.BlockSpec((1, t, LANES), lambda p, bi: (bi, 0, p))
    kspec = pl.BlockSpec((1, k_pad.shape[1], LANES), lambda p, bi: (bi, 0, p))
    vspec = pl.BlockSpec((1, LANES, vt_pad.shape[2]), lambda p, bi: (bi, p, 0))
    cspec = pl.BlockSpec((1, 2, cvec.shape[2]), lambda p, bi: (p, 0, 0))
    return pl.pallas_call(
        functools.partial(_band_t_kernel, tq=tq, t=t),
        grid=(pairs, b),
        in_specs=[cspec, qspec, kspec, vspec],
        out_specs=qspec,
        out_shape=jax.ShapeDtypeStruct(q.shape, BF16),
        scratch_shapes=[pltpu.VMEM((window, 2 * tq), F32), pltpu.VMEM((window, 2 * tq), F32),
                        pltpu.VMEM((2, 1, 2 * tq), F32), pltpu.VMEM((2, window, tq), F32)],
        compiler_params=_params(2),
        name="band_t_attn",
    )(cvec, q, k_pad, vt_pad)


def _proj_odd_kernel(x_ref, g_ref, w_ref, wt_ref, q_ref, k_ref, v_ref, kt_ref, vt_ref, *,
                     tiles, pad_tiles, v_feature_major):
    r = pl.program_id(1)

    if pad_tiles:
        @pl.when(r < pad_tiles)
        def _():
            k_ref[...] = jnp.zeros(k_ref.shape, BF16)
            v_ref[...] = jnp.zeros(v_ref.shape, BF16)

    @pl.when(r >= pad_tiles)
    def _():
        h = _rms(x_ref[...], g_ref[...]).astype(BF16)

        def seg(i):
            return jnp.dot(h, w_ref[:, i * D_MODEL:(i + 1) * D_MODEL], preferred_element_type=F32)

        def seg_t(i):
            return lax.dot_general(wt_ref[i], h, _NT, preferred_element_type=F32)

        is_tail = r == pad_tiles + tiles - 1
        q_ref[...] = (seg(0) * QK_SCALE).astype(BF16)
        k_ref[...] = seg(1).astype(BF16)
        if v_feature_major:
            vzt = seg_t(1)
            v_ref[0] = vzt.astype(BF16)

            @pl.when(is_tail)
            def _():
                kt_ref[0] = seg_t(0)
                vt_ref[0] = vzt
        else:
            v_ref[...] = seg(2).astype(BF16)

            @pl.when(is_tail)
            def _():
                kt_ref[0] = seg_t(0)
                vt_ref[0] = seg_t(1)


def _proj_odd(x, g, w, w_t, *, tiles, pad_tiles, v_feature_major):
    m = x.shape[0]
    tm = min(ROW_TILE, m)
    nb = m // tm // tiles
    steps = pad_tiles + tiles
    src = lambda b, r: (b * tiles + jnp.maximum(r - pad_tiles, 0), 0)
    row = pl.BlockSpec((tm, D_MODEL), src)
    padded_row = pl.BlockSpec((tm, D_MODEL), lambda b, r: (b * steps + r, 0))
    tail = pl.BlockSpec((1, D_MODEL, tm), lambda b, r: (b, 0, 0))
    bfo = jax.ShapeDtypeStruct((m, D_MODEL), BF16)
    padded = jax.ShapeDtypeStruct((nb * steps * tm, D_MODEL), BF16)
    tailo = jax.ShapeDtypeStruct((nb, D_MODEL, tm), F32)
    if v_feature_major:
        vo = jax.ShapeDtypeStruct((nb, D_MODEL, steps * tm), BF16)
        vspec = pl.BlockSpec((1, D_MODEL, tm), lambda b, r: (b, 0, r))
    else:
        vo, vspec = padded, padded_row
    return pl.pallas_call(
        functools.partial(_proj_odd_kernel, tiles=tiles, pad_tiles=pad_tiles, v_feature_major=v_feature_major),
        grid=(nb, steps),
        in_specs=[row, _const_spec((1, D_MODEL)), _const_spec(w.shape), _const_spec(w_t.shape)],
        out_specs=[row, padded_row, vspec, tail, tail],
        out_shape=[bfo, padded, vo, tailo, tailo],
        compiler_params=_params(2),
        name="proj_odd",
    )(x, g, w, w_t)


def _post_kernel(*refs, n_o):
    o_refs = refs[:n_o]
    x_ref, wo_ref, gpm_ref, gpre_ref, wup_ref, wdn_ref, gpf_ref, out_ref = refs[n_o:]
    o = o_refs[0][...] if n_o == 1 else jnp.concatenate([r[...] for r in o_refs], axis=1)
    mixed = jnp.dot(o, wo_ref[...], preferred_element_type=F32)
    x1 = x_ref[...] + _rms(mixed, gpm_ref[...])
    h = _rms(x1, gpre_ref[...]).astype(BF16)
    acc = jnp.zeros(x1.shape, F32)
    for c in range(D_FF // FF_CHUNK):
        u = jnp.dot(h, wup_ref[:, c * FF_CHUNK:(c + 1) * FF_CHUNK], preferred_element_type=F32)
        u = jnp.maximum(u, 0.0)
        acc = acc + jnp.dot((u * u).astype(BF16), wdn_ref[c * FF_CHUNK:(c + 1) * FF_CHUNK, :],
                            preferred_element_type=F32)
    out_ref[...] = x1 + _rms(acc, gpf_ref[...])


def _post(o_list, x, wo, gpm, gpre, wup, wdn, gpf):
    m = x.shape[0]
    tm = min(ROW_TILE, m)
    row = lambda w: pl.BlockSpec((tm, w), lambda i: (i, 0))
    gspec = _const_spec((1, D_MODEL))
    return pl.pallas_call(
        functools.partial(_post_kernel, n_o=len(o_list)),
        grid=(m // tm,),
        in_specs=[row(o.shape[1]) for o in o_list]
                 + [row(D_MODEL), _const_spec(wo.shape), gspec, gspec,
                    _const_spec(wup.shape), _const_spec(wdn.shape), gspec],
        out_specs=row(D_MODEL),
        out_shape=jax.ShapeDtypeStruct((m, D_MODEL), F32),
        compiler_params=_params(1),
        name="post_mix_mlp",
    )(*o_list, x, wo, gpm, gpre, wup, wdn, gpf)


def _band_bias_vectors(table):
    n = 2 * BAND_PAST
    m = np.arange(n)
    d = np.where(m <= n - REL_CLIP, -m, n - m)
    idx = np.clip(d + BAND_PAST, -REL_CLIP, REL_CLIP) + REL_CLIP
    return jnp.take(table.astype(F32), jnp.asarray(idx, jnp.int32), axis=1).reshape(H_C // 2, 2, n)


def kernel(x_prompt, x_sample, cache_a_k, cache_a_v, cache_b_k, cache_b_v, cache_b_logf, cache_c_k, cache_c_v,
           w_in_even, b_forget, lam_q1, lam_k1, lam_q2, lam_k2, subln_g, w_out_even, w_in_odd, rel_bias,
           w_out_odd, g_pre_mix, g_post_mix, g_pre_ffn, g_post_ffn, w_ffn_up, w_ffn_down):
    b_p, t_p, _ = x_prompt.shape
    b_s, t_s, _ = x_sample.shape
    past = cache_b_logf.shape[2]
    m_p, m_s = b_p * t_p, b_s * t_s
    xp = x_prompt.reshape(m_p, D_MODEL)
    xs = x_sample.reshape(m_s, D_MODEL)
    gvec = lambda a, l: a[l].reshape(1, D_MODEL)

    lam_init = 0.8 - 0.6 * math.exp(-0.3 * 0)
    w_in_t = w_in_even[0].T
    w_t = w_in_t[:N_SEG * A_W].astype(BF16)
    wft = jnp.pad(w_in_t[N_SEG * A_W:], ((0, 16 - H_B), (0, 0))).astype(BF16)
    bf_col = b_forget[0].reshape(H_B, 1)
    slopes = 2.0 ** (-8.0 * jnp.arange(1, H_A + 1, dtype=F32) / H_A)
    lamv = jnp.stack([lam_q1[0], lam_k1[0], lam_q2[0], lam_k2[0]]).astype(F32)
    sub_g = subln_g[0].reshape(1, 2 * HEAD_DIM)
    wo0 = w_out_even[0].astype(BF16)
    wup0, wdn0 = w_ffn_up[0].astype(BF16), w_ffn_down[0].astype(BF16)

    def even_layer(x, bsz, t, caches):
        m = x.shape[0]
        tm = min(ROW_TILE, m)
        (qa, ka, kab, va, vab, qb, kb, kbb, vb, vbb, lft) = _proj_even(
            x, gvec(g_pre_mix, 0), w_t, wft, bf_col,
            tiles_per_batch=(t // tm) if caches is None else None)
        r3 = lambda a: a.reshape(bsz, t, A_W)
        lft = lft.transpose(1, 0, 2).reshape(H_B, bsz, t).transpose(1, 0, 2)
        tok_head = lambda a: a.reshape(1, bsz, t, H_A, 2 * HEAD_DIM)
        if caches is None:
            oa = _flash_t("diff", r3(qa), r3(kab), vab, (slopes, lamv, sub_g), tq=512, lam_init=lam_init)
            ob = _flash_t("fox", r3(qb), r3(kbb), vbb, (_cumsum_time(lft),), tq=512)
            time_minor = lambda a: a.reshape(bsz, H_B, HEAD_DIM, t).transpose(0, 3, 1, 2)[None]
            new = (tok_head(ka), tok_head(va), time_minor(kb), time_minor(vb), lft.transpose(0, 2, 1)[None])
        else:
            c_ka, c_va, c_kb, c_vb, c_lf = caches
            p_len = c_ka.shape[1]
            t_all = p_len + t
            t_pad = -(-t_all // LANES) * LANES
            assert p_len % CHUNK == 0 and t <= CHUNK
            lf_all = jnp.concatenate([c_lf.astype(F32).transpose(0, 2, 1), lft], axis=2)
            cum = _cumsum_time(jnp.pad(lf_all, ((0, 0), (0, 0), (0, t_pad - t_all))))
            tok_head_rows = lambda c: c.reshape(bsz, p_len * H_A, 2 * HEAD_DIM)
            feat_major = lambda c: c.transpose(0, 2, 3, 1).reshape(bsz, A_W, p_len)
            oa = _sample_attn("diff", r3(qa), tok_head_rows(c_ka), tok_head_rows(c_va), r3(kab), r3(vab),
                              (slopes, lamv, sub_g), past=p_len, lam_init=lam_init)
            ob = _sample_attn("fox", r3(qb), feat_major(c_kb), feat_major(c_vb), r3(kbb), r3(vbb),
                              (cum,), past=p_len)
            new = (tok_head(ka), tok_head(va), kb.reshape(1, bsz, t, H_B, HEAD_DIM),
                   vb.reshape(1, bsz, t, H_B, HEAD_DIM), lft.transpose(0, 2, 1)[None])
        x_out = _post([oa.reshape(m, A_W), ob.reshape(m, A_W)], x, wo0, gvec(g_post_mix, 0),
                      gvec(g_pre_ffn, 0), wup0, wdn0, gvec(g_post_ffn, 0))
        return x_out, new

    xp, new_p_even = even_layer(xp, b_p, t_p, None)
    xs, new_s_even = even_layer(xs, b_s, t_s, (cache_a_k[0], cache_a_v[0], cache_b_k[0], cache_b_v[0],
                                              cache_b_logf[0]))

    w_odd = w_in_odd[0].astype(BF16)
    w_odd_t = jnp.stack([w_in_odd[0][:, s * D_MODEL:(s + 1) * D_MODEL].T for s in (1, 2)]).astype(BF16)
    cvec = _band_bias_vectors(rel_bias[0])
    to_cache = lambda a: a.reshape(a.shape[0], H_C, HEAD_DIM, a.shape[2]).transpose(0, 3, 1, 2)[None]
    from_cache = lambda c: c.transpose(0, 2, 3, 1).reshape(c.shape[0], D_MODEL, c.shape[1])
    wo1 = w_out_odd[0].astype(BF16)
    wup1, wdn1 = w_ffn_up[1].astype(BF16), w_ffn_down[1].astype(BF16)

    def odd_layer(x, bsz, t, caches):
        m = x.shape[0]
        tm = min(ROW_TILE, m)
        r3 = lambda a: a.reshape(bsz, t, D_MODEL)
        if caches is None:
            assert BAND_PAST % tm == 0 and tm == min(BAND_PAST, t)
            q, kp, vtp, k_tail, v_tail = _proj_odd(x, gvec(g_pre_mix, 1), w_odd, w_odd_t, tiles=t // tm,
                                                   pad_tiles=BAND_PAST // tm, v_feature_major=True)
            o = _band_t(r3(q), kp.reshape(bsz, BAND_PAST + t, D_MODEL), vtp, cvec, tq=256)
            new = (to_cache(k_tail), to_cache(v_tail))
        else:
            c_k, c_v = caches
            assert c_k.shape[1] == BAND_PAST and past % CHUNK == 0 and past >= BAND_PAST and t <= CHUNK
            q, k, v, k_tail, v_tail = _proj_odd(x, gvec(g_pre_mix, 1), w_odd, w_odd_t, tiles=m // tm,
                                                pad_tiles=0, v_feature_major=False)
            o = _sample_attn("band", r3(q), from_cache(c_k), from_cache(c_v), r3(k), r3(v), (cvec,), past=past)
            upd = lambda c, tail: to_cache(jnp.concatenate(
                [from_cache(c)[:, :, t:], tail.reshape(D_MODEL, bsz, t).transpose(1, 0, 2)], axis=2))
            new = (upd(c_k, k_tail), upd(c_v, v_tail))
        x_out = _post([o.reshape(m, D_MODEL)], x, wo1, gvec(g_post_mix, 1), gvec(g_pre_ffn, 1),
                      wup1, wdn1, gvec(g_post_ffn, 1))
        return x_out, new

    xp, new_p_odd = odd_layer(xp, b_p, t_p, None)
    xs, new_s_odd = odd_layer(xs, b_s, t_s, (cache_c_k[0], cache_c_v[0]))

    return (xp.reshape(b_p, t_p, D_MODEL), xs.reshape(b_s, t_s, D_MODEL),
            *new_p_even, *new_p_odd, *new_s_even, *new_s_odd)
```

```python
import functools
import math

import numpy as np
import jax
import jax.numpy as jnp
from jax import lax
from jax.experimental import pallas as pl
from jax.experimental.pallas import tpu as pltpu

F32 = jnp.float32
BF16 = jnp.bfloat16

D_MODEL = 1024
HEAD_DIM = 64
CHUNK = 64
H_A = 4
H_B = 8
H_C = 16
A_W = 512
N_SEG = 6
BAND_PAST = 512
REL_CLIP = 256
D_FF = 4 * D_MODEL
NORM_EPS = 1e-6
NEG_INF = -1e30
QK_SCALE = HEAD_DIM ** -0.5

LANES = 128
ROW_TILE = 512
FF_CHUNK = 1024
VMEM_LIMIT = 56 * 1024 * 1024

_NT = (((1,), (1,)), ((), ()))


def _rms(x, g):
    ms = jnp.mean(x * x, axis=-1, keepdims=True)
    return x * lax.rsqrt(ms + NORM_EPS) * g


def _log_sigmoid(x):
    t = -x
    return -(jnp.maximum(t, 0.0) + jnp.log1p(jnp.exp(-jnp.abs(t))))


def _const_spec(shape):
    nd = len(shape)
    return pl.BlockSpec(shape, lambda *_: (0,) * nd, pipeline_mode=pl.Buffered(1))


def _params(n_axes):
    return pltpu.CompilerParams(dimension_semantics=("arbitrary",) * n_axes,
                                vmem_limit_bytes=VMEM_LIMIT)


def _proj_even_kernel(x_ref, g_ref, wt_ref, wft_ref, bft_ref,
                      qa_ref, ka_ref, kab_ref, va_ref, vab_ref,
                      qb_ref, kb_ref, kbb_ref, vb_ref, vbb_ref, lft_ref, *, feature_major):
    h = _rms(x_ref[...], g_ref[...]).astype(BF16)
    tm = h.shape[0]

    def seg(i):
        return lax.dot_general(h, wt_ref[i * A_W:(i + 1) * A_W, :], _NT, preferred_element_type=F32)

    def seg_t(i):
        return lax.dot_general(wt_ref[i * A_W:(i + 1) * A_W, :], h, _NT, preferred_element_type=F32)

    def store_token_head_rows(ref, z):
        for hd in range(H_A):
            ref[pl.ds(hd, tm, stride=H_A), :] = z[:, hd * LANES:(hd + 1) * LANES]

    qa_ref[...] = (seg(0) * QK_SCALE).astype(BF16)
    z = seg(1)
    store_token_head_rows(ka_ref, z)
    kab_ref[...] = z.astype(BF16)
    z = seg(2)
    store_token_head_rows(va_ref, z)
    if feature_major:
        vab_ref[0] = z.T.astype(BF16)
    else:
        vab_ref[...] = z.astype(BF16)
    qb_ref[...] = (seg(3) * QK_SCALE).astype(BF16)
    z = seg(4)
    kbb_ref[...] = z.astype(BF16)
    if feature_major:
        kb_ref[0] = z.T
        zt = seg_t(5)
        vb_ref[0] = zt
        vbb_ref[0] = zt.astype(BF16)
    else:
        kb_ref[...] = z
        z = seg(5)
        vb_ref[...] = z
        vbb_ref[...] = z.astype(BF16)
    fzt = lax.dot_general(wft_ref[...], h, _NT, preferred_element_type=F32)
    lft_ref[0] = _log_sigmoid(fzt[:H_B] + bft_ref[...])


def _proj_even(x, g, w_t, wft, bf_col, *, tiles_per_batch=None):
    m = x.shape[0]
    tm = min(ROW_TILE, m)
    n = m // tm
    row = lambda w: pl.BlockSpec((tm, w), lambda i: (i, 0))
    bfo = jax.ShapeDtypeStruct((m, A_W), BF16)
    tok_head = jax.ShapeDtypeStruct((m * H_A, LANES), F32)
    tok_head_spec = pl.BlockSpec((tm * H_A, LANES), lambda i: (i, 0))
    if tiles_per_batch:
        tpb = tiles_per_batch
        fm = lambda dt: jax.ShapeDtypeStruct((n // tpb, A_W, tpb * tm), dt)
        fm_spec = pl.BlockSpec((1, A_W, tm), lambda i: (i // tpb, 0, i % tpb))
        outs = [(bfo, row(A_W)), (tok_head, tok_head_spec), (bfo, row(A_W)), (tok_head, tok_head_spec),
                (fm(BF16), fm_spec), (bfo, row(A_W)), (fm(F32), fm_spec), (bfo, row(A_W)),
                (fm(F32), fm_spec), (fm(BF16), fm_spec)]
    else:
        f32o = jax.ShapeDtypeStruct((m, A_W), F32)
        outs = [(bfo, row(A_W)), (tok_head, tok_head_spec), (bfo, row(A_W)), (tok_head, tok_head_spec),
                (bfo, row(A_W)), (bfo, row(A_W)), (f32o, row(A_W)), (bfo, row(A_W)),
                (f32o, row(A_W)), (bfo, row(A_W))]
    outs.append((jax.ShapeDtypeStruct((n, H_B, tm), F32), pl.BlockSpec((1, H_B, tm), lambda i: (i, 0, 0))))
    return pl.pallas_call(
        functools.partial(_proj_even_kernel, feature_major=bool(tiles_per_batch)),
        grid=(n,),
        in_specs=[row(D_MODEL), _const_spec((1, D_MODEL)), _const_spec(w_t.shape),
                  _const_spec(wft.shape), _const_spec(bf_col.shape)],
        out_specs=[s for _, s in outs],
        out_shape=[o for o, _ in outs],
        compiler_params=_params(1),
        name="proj_even",
    )(x, g, w_t, wft, bf_col)


def _cumsum_kernel(x_ref, o_ref):
    x = x_ref[0]
    t = x.shape[1]
    col = lax.broadcasted_iota(jnp.int32, x.shape, 1)
    s = 1
    while s < t:
        x = x + jnp.where(col >= s, pltpu.roll(x, s, 1), 0.0)
        s *= 2
    o_ref[0] = x


def _cumsum_time(x):
    b, h, t = x.shape
    spec = pl.BlockSpec((1, h, t), lambda i: (i, 0, 0))
    return pl.pallas_call(
        _cumsum_kernel, grid=(b,), in_specs=[spec], out_specs=spec,
        out_shape=jax.ShapeDtypeStruct(x.shape, F32),
        compiler_params=_params(1), name="cumsum_time",
    )(x)


def _stack_pair(q):
    q32 = q.astype(F32)
    lo = lax.broadcasted_iota(jnp.int32, q32.shape, 1) < HEAD_DIM
    return jnp.concatenate([jnp.where(lo, q32, 0.0), jnp.where(lo, 0.0, q32)], axis=0).astype(BF16)


def _sample_attn_kernel(*refs, mode, t, past, lam_init):
    if mode == "diff":
        slopes_ref, lamv_ref, g_ref, q_ref, kc_ref, vc_ref, kn_ref, vn_ref, o_ref = refs
    elif mode == "fox":
        cum_ref, q_ref, kc_ref, vc_ref, kn_ref, vn_ref, o_ref = refs
    else:
        c_ref, q_ref, kc_ref, vc_ref, kn_ref, vn_ref, o_ref = refs
    pairs = q_ref.shape[2] // LANES
    n_c = vc_ref.shape[1] // pairs if mode == "diff" else kc_ref.shape[2]
    row = lax.broadcasted_iota(jnp.int32, (t, n_c), 0)
    col = lax.broadcasted_iota(jnp.int32, (t, n_c), 1)
    row_n = lax.broadcasted_iota(jnp.int32, (t, t), 0)
    col_n = lax.broadcasted_iota(jnp.int32, (t, t), 1)
    lo = lax.broadcasted_iota(jnp.int32, (t, LANES), 1) < HEAD_DIM

    for p in range(pairs):
        lanes = slice(p * LANES, (p + 1) * LANES)
        q2 = _stack_pair(q_ref[0, :, lanes])
        kn = kn_ref[0, :, lanes]
        vn = vn_ref[0, :, lanes]
        if mode == "diff":
            kc = kc_ref[0, pl.ds(p, n_c, stride=pairs), :].astype(BF16)
            vc = vc_ref[0, pl.ds(p, n_c, stride=pairs), :].astype(BF16)
            s_c = lax.dot_general(q2, kc, _NT, preferred_element_type=F32)
        else:
            kc = kc_ref[0, lanes, :].astype(BF16)
            vc = vc_ref[0, lanes, :].astype(BF16)
            s_c = jnp.dot(q2, kc, preferred_element_type=F32)
        s_n = lax.dot_general(q2, kn, _NT, preferred_element_type=F32)

        outs = []
        for a in range(2):
            rows = slice(a * t, (a + 1) * t)
            if mode == "diff":
                slope = slopes_ref[p]
                t_c = s_c[rows] - slope * (past + row - col).astype(F32)
                t_n = s_n[rows] - slope * jnp.abs(row_n - col_n).astype(F32)
            elif mode == "fox":
                ck = cum_ref[0, 2 * p + a]
                t_c = s_c[rows] - ck[:, :n_c]
                t_n = jnp.where(col_n <= row_n, s_n[rows] - ck[:, n_c:n_c + t], NEG_INF)
            else:
                c = jnp.broadcast_to(c_ref[p, a:a + 1, :], (t, c_ref.shape[2]))
                bias = pltpu.roll(c, 0, 1, stride=1, stride_axis=0)
                t_c = s_c[rows] + bias[:, :n_c]
                t_n = s_n[rows] + bias[:, n_c:n_c + t]
            m = jnp.maximum(jnp.max(t_c, axis=1, keepdims=True), jnp.max(t_n, axis=1, keepdims=True))
            p_c = jnp.exp(t_c - m)
            p_n = jnp.exp(t_n - m)
            l = jnp.sum(p_c, axis=1, keepdims=True) + jnp.sum(p_n, axis=1, keepdims=True)
            if mode == "diff":
                acc = jnp.dot(p_c.astype(BF16), vc, preferred_element_type=F32)
            else:
                acc = lax.dot_general(p_c.astype(BF16), vc, _NT, preferred_element_type=F32)
            acc = acc + jnp.dot(p_n.astype(BF16), vn, preferred_element_type=F32)
            outs.append(acc / l)
        if mode == "diff":
            lv = lamv_ref[...]
            lam = (jnp.exp(jnp.sum(lv[0:1] * lv[1:2], axis=1, keepdims=True))
                   - jnp.exp(jnp.sum(lv[2:3] * lv[3:4], axis=1, keepdims=True)) + lam_init)
            oa = outs[0] - lam * outs[1]
            o_ref[0, :, lanes] = (_rms(oa, g_ref[...]) * (1.0 - lam_init)).astype(BF16)
        else:
            o_ref[0, :, lanes] = jnp.where(lo, outs[0], outs[1]).astype(BF16)


def _sample_attn(mode, q, kc, vc, kn, vn, extras, *, past, lam_init=0.0):
    b, t, width = q.shape
    whole = lambda a: pl.BlockSpec((1,) + a.shape[1:], lambda i: (i,) + (0,) * (a.ndim - 1))
    const = lambda a: pl.BlockSpec(a.shape, lambda i: (0,) * a.ndim)
    if mode == "diff":
        slopes, lamv, g = extras
        extra_specs = [pl.BlockSpec(memory_space=pltpu.SMEM), const(lamv), const(g)]
    elif mode == "fox":
        extras = (extras[0][:, :, None, :],)
        extra_specs = [whole(extras[0])]
    else:
        extra_specs = [const(extras[0])]
    return pl.pallas_call(
        functools.partial(_sample_attn_kernel, mode=mode, t=t, past=past, lam_init=lam_init),
        grid=(b,),
        in_specs=extra_specs + [whole(q), whole(kc), whole(vc), whole(kn), whole(vn)],
        out_specs=whole(q),
        out_shape=jax.ShapeDtypeStruct(q.shape, BF16),
        compiler_params=_params(1),
        name="sample_" + mode,
    )(*extras, q, kc, vc, kn, vn)


ONES_ROWS = 16


def _with_ones(vt):
    return jnp.concatenate([vt, jnp.ones((ONES_ROWS, vt.shape[1]), BF16)], axis=0)


AUG_TERMS = 3
FLASH_STREAMS = 4
BAND_STREAMS = 2


def _lane_terms(lane, first, terms):
    out = jnp.zeros(terms[0].shape, F32)
    for t, v in enumerate(terms):
        out = jnp.where(lane == first + t, v, out)
    return out


def _flash_t_kernel(*refs, mode, tq, lam_init):
    tk = tq
    n_extra = 3 if mode == "diff" else 1
    extra_refs, (q_ref, k_ref, vt_ref, o_ref, m_sc, acc_sc, q2_sc, mu_sc, kaug_sc, bdiag_sc) = (
        refs[:n_extra], refs[n_extra:n_extra + 10])
    s_bufs = refs[n_extra + 10:]
    if mode == "diff":
        slopes_ref, lamv_ref, g_ref = extra_refs
    else:
        (cum_ref,) = extra_refs
    pair = pl.program_id(0)
    qi = pl.program_id(2)
    t_all = kaug_sc.shape[1]
    streams = range(FLASH_STREAMS)

    @pl.when((pl.program_id(1) == 0) & (qi == 0))
    def _():
        row = lax.broadcasted_iota(jnp.int32, (tk, tq), 0)
        col = lax.broadcasted_iota(jnp.int32, (tk, tq), 1)
        lane_q = lax.broadcasted_iota(jnp.int32, (2 * tq, LANES), 1)
        first_q = jnp.where(lax.broadcasted_iota(jnp.int32, (2 * tq, LANES), 0) < tq, 0, AUG_TERMS)
        if mode == "diff":
            slope = slopes_ref[pair]
            visible = (row >> 6) <= (col >> 6)
            bdiag_sc[...] = jnp.where(visible, -2.0 * slope * jnp.maximum(row - col, 0).astype(F32), NEG_INF)
            for s in streams:
                q2_sc[s, :, LANES:] = jnp.where(lane_q < AUG_TERMS, 1.0, 0.0).astype(BF16)
            for c0 in range(0, t_all, tk):
                j = c0 + lax.broadcasted_iota(jnp.int32, (tk, LANES), 0)
                lane = lax.broadcasted_iota(jnp.int32, (tk, LANES), 1)
                terms = [((j >> 8) << 8).astype(F32), (((j >> 4) & 15) << 4).astype(F32), (j & 15).astype(F32)]
                kaug_sc[0, c0:c0 + tk, :] = (slope * _lane_terms(lane, 0, terms)).astype(BF16)
        else:
            bdiag_sc[...] = jnp.where(row <= col, 0.0, NEG_INF)
            mine = (lane_q >= first_q) & (lane_q < first_q + AUG_TERMS)
            for s in streams:
                q2_sc[s, :, LANES:] = jnp.where(mine, -1.0, 0.0).astype(BF16)

    if mode == "fox":
        @pl.when(qi == 0)
        def _():
            for s in streams:
                for c0 in range(0, t_all, tk):
                    lane = lax.broadcasted_iota(jnp.int32, (tk, LANES), 1)
                    feat = jnp.zeros((tk, LANES), F32)
                    for a in range(2):
                        r = cum_ref[s, 2 * pair + a, :, c0:c0 + tk]
                        ck = jnp.broadcast_to(r, (LANES, tk)).T
                        hi = ck.astype(BF16).astype(F32)
                        mid = (ck - hi).astype(BF16).astype(F32)
                        lo = ((ck - hi) - mid).astype(BF16).astype(F32)
                        feat = feat + _lane_terms(lane, AUG_TERMS * a, [hi, mid, lo])
                    kaug_sc[s, c0:c0 + tk, :] = feat.astype(BF16)

    for s in streams:
        q2_sc[s, :, :LANES] = _stack_pair(q_ref[s])
    m_sc[...] = jnp.full(m_sc.shape, NEG_INF, F32)
    acc_sc[...] = jnp.zeros(acc_sc.shape, F32)

    def stage_a(k0, slot, diag):
        for s in streams:
            kaug = kaug_sc[0 if mode == "diff" else s, pl.ds(k0, tk), :]
            kk = jnp.concatenate([k_ref[s, pl.ds(k0, tk), :], kaug], axis=1)
            st = lax.dot_general(kk, q2_sc[s], _NT, preferred_element_type=F32)
            for a in range(2):
                sa = st[:, a * tq:(a + 1) * tq]
                if diag:
                    sa = sa + bdiag_sc[...]
                s_bufs[2 * s + slot][:, a * tq:(a + 1) * tq] = sa
                mu_sc[s, slot, :, a * tq:(a + 1) * tq] = jnp.max(sa, axis=0, keepdims=True)

    def stage_b(k0, slot):
        for s in streams:
            vt = vt_ref[s, :, pl.ds(k0, tk)]
            for a in range(2):
                m_prev = m_sc[s, a]
                m_next = jnp.maximum(m_prev, mu_sc[s, slot, :, a * tq:(a + 1) * tq])
                p = jnp.exp(s_bufs[2 * s + slot][:, a * tq:(a + 1) * tq] - m_next)
                alpha = jnp.exp(m_prev - m_next)
                vta = _with_ones(vt if mode == "diff" else vt[a * HEAD_DIM:(a + 1) * HEAD_DIM])
                acc_sc[s, a] = alpha * acc_sc[s, a] + jnp.dot(vta, p.astype(BF16), preferred_element_type=F32)
                m_sc[s, a] = m_next

    blk = lambda j: pl.multiple_of(j * tk, tk)
    pl.when(qi == 0)(lambda: stage_a(0, 0, True))
    pl.when(qi > 0)(lambda: stage_a(0, 0, False))

    def two_blocks(i, carry):
        stage_a(blk(2 * i + 1), 1, False)
        stage_b(blk(2 * i), 0)
        stage_a(blk(2 * i + 2), 0, False)
        stage_b(blk(2 * i + 1), 1)
        return carry

    lax.fori_loop(0, (qi - 1) // 2, two_blocks, 0)

    @pl.when(qi == 0)
    def _():
        stage_b(0, 0)

    @pl.when(qi % 2 == 1)
    def _():
        stage_a(blk(qi), 1, True)
        stage_b(blk(qi - 1), 0)
        stage_b(blk(qi), 1)

    @pl.when((qi % 2 == 0) & (qi > 0))
    def _():
        stage_a(blk(qi - 1), 1, False)
        stage_b(blk(qi - 2), 0)
        stage_a(blk(qi), 0, True)
        stage_b(blk(qi - 1), 1)
        stage_b(blk(qi), 0)

    d = acc_sc.shape[2] - ONES_ROWS
    for s in streams:
        o0 = acc_sc[s, 0, :d, :] / acc_sc[s, 0, d:d + 1, :]
        o1 = acc_sc[s, 1, :d, :] / acc_sc[s, 1, d:d + 1, :]
        if mode == "diff":
            lv = lamv_ref[...]
            lam = (jnp.exp(jnp.sum(lv[0:1] * lv[1:2], axis=1, keepdims=True))
                   - jnp.exp(jnp.sum(lv[2:3] * lv[3:4], axis=1, keepdims=True)) + lam_init)
            oa = o0 - lam * o1
            ms = jnp.mean(oa * oa, axis=0, keepdims=True)
            y = (oa * lax.rsqrt(ms + NORM_EPS)).T * g_ref[...]
            o_ref[s] = (y * (1.0 - lam_init)).astype(BF16)
        else:
            o_ref[s] = jnp.concatenate([o0, o1], axis=0).T.astype(BF16)


def _flash_t(mode, q, k, vt, extras, *, tq, lam_init=0.0):
    b, t, width = q.shape
    pairs = width // LANES
    ns = FLASH_STREAMS
    assert b % ns == 0
    qspec = pl.BlockSpec((ns, tq, LANES), lambda p, bi, i: (bi, i, p))
    kspec = pl.BlockSpec((ns, t, LANES), lambda p, bi, i: (bi, 0, p))
    vspec = pl.BlockSpec((ns, LANES, t), lambda p, bi, i: (bi, p, 0))
    acc_rows = (LANES if mode == "diff" else HEAD_DIM) + ONES_ROWS
    scratch = [pltpu.VMEM((ns, 2, 1, tq), F32),
               pltpu.VMEM((ns, 2, acc_rows, tq), F32),
               pltpu.VMEM((ns, 2 * tq, 2 * LANES), BF16),
               pltpu.VMEM((ns, 2, 1, 2 * tq), F32),
               pltpu.VMEM((1 if mode == "diff" else ns, t, LANES), BF16),
               pltpu.VMEM((tq, tq), F32)]
    scratch += [pltpu.VMEM((tq, 2 * tq), F32)] * (2 * ns)
    if mode == "diff":
        slopes, lamv, g = extras
        extra_specs = [pl.BlockSpec(memory_space=pltpu.SMEM),
                       pl.BlockSpec(lamv.shape, lambda p, bi, i: (0, 0)),
                       pl.BlockSpec(g.shape, lambda p, bi, i: (0, 0))]
    else:
        extras = (extras[0][:, :, None, :],)
        extra_specs = [pl.BlockSpec((ns,) + extras[0].shape[1:], lambda p, bi, i: (bi, 0, 0, 0))]
    return pl.pallas_call(
        functools.partial(_flash_t_kernel, mode=mode, tq=tq, lam_init=lam_init),
        grid=(pairs, b // ns, t // tq),
        in_specs=extra_specs + [qspec, kspec, vspec],
        out_specs=qspec,
        out_shape=jax.ShapeDtypeStruct(q.shape, BF16),
        scratch_shapes=scratch,
        compiler_params=_params(3),
        name="flash_t_" + mode,
    )(*extras, q, k, vt)


def _band_t_kernel(c_ref, q_ref, k_ref, vt_ref, o_ref, mu_sc, bias_sc, *s_bufs, tq, t):
    window = BAND_PAST + tq
    n_q = t // tq
    streams = range(BAND_STREAMS)

    @pl.when(pl.program_id(1) == 0)
    def _():
        row = lax.broadcasted_iota(jnp.int32, (tq, window), 0)
        col = lax.broadcasted_iota(jnp.int32, (tq, window), 1)
        visible = ((col >> 6) >= (row >> 6)) & ((col >> 6) <= (row >> 6) + BAND_PAST // CHUNK)
        for a in range(2):
            c = jnp.broadcast_to(c_ref[0, a:a + 1, :], (tq, c_ref.shape[2]))
            bias = pltpu.roll(c, 0, 1, stride=1, stride_axis=0)[:, :window]
            bias_sc[a] = jnp.where(visible, bias, NEG_INF).T

    lane_q = lax.broadcasted_iota(jnp.int32, (2 * tq, LANES), 1)
    q_extra = jnp.where(lane_q == 0, 1.0, 0.0).astype(BF16)

    def stage_a(qb, slot):
        w0 = pl.multiple_of(qb * tq, tq)
        pad_row = (w0 + lax.broadcasted_iota(jnp.int32, (window, LANES), 0)) < BAND_PAST
        lane_k = lax.broadcasted_iota(jnp.int32, (window, LANES), 1)
        k_extra = jnp.where(pad_row & (lane_k == 0), NEG_INF, 0.0).astype(BF16)
        for s in streams:
            q2 = jnp.concatenate([_stack_pair(q_ref[s, pl.ds(w0, tq), :]), q_extra], axis=1)
            kk = jnp.concatenate([k_ref[s, pl.ds(w0, window), :], k_extra], axis=1)
            st = lax.dot_general(kk, q2, _NT, preferred_element_type=F32)
            for a in range(2):
                u = st[:, a * tq:(a + 1) * tq] + bias_sc[a]
                s_bufs[2 * s + slot][:, a * tq:(a + 1) * tq] = u
                mu_sc[s, slot, :, a * tq:(a + 1) * tq] = jnp.max(u, axis=0, keepdims=True)

    def stage_b(qb, slot):
        w0 = pl.multiple_of(qb * tq, tq)
        for s in streams:
            vt = vt_ref[s, :, pl.ds(w0, window)]
            outs = []
            for a in range(2):
                p = jnp.exp(s_bufs[2 * s + slot][:, a * tq:(a + 1) * tq]
                            - mu_sc[s, slot, :, a * tq:(a + 1) * tq])
                vta = _with_ones(vt[a * HEAD_DIM:(a + 1) * HEAD_DIM])
                r = jnp.dot(vta, p.astype(BF16), preferred_element_type=F32)
                outs.append(r[:HEAD_DIM] / r[HEAD_DIM:HEAD_DIM + 1])
            o_ref[s, pl.ds(w0, tq), :] = jnp.concatenate(outs, axis=0).T.astype(BF16)

    stage_a(0, 0)

    def two_blocks(i, carry):
        stage_a(2 * i + 1, 1)
        stage_b(2 * i, 0)
        stage_a(2 * i + 2, 0)
        stage_b(2 * i + 1, 1)
        return carry

    lax.fori_loop(0, n_q // 2 - 1, two_blocks, 0)
    stage_a(n_q - 1, 1)
    stage_b(n_q - 2, 0)
    stage_b(n_q - 1, 1)


def _band_t(q, k_pad, vt_pad, cvec, *, tq):
    b, t, width = q.shape
    pairs = width // LANES
    window = BAND_PAST + tq
    ns = BAND_STREAMS
    assert (t // tq) % 2 == 0 and t // tq >= 2 and b % ns == 0
    qspec = pl.BlockSpec((ns, t, LANES), lambda p, bi: (bi, 0, p))
    kspec = pl.BlockSpec((ns, k_pad.shape[1], LANES), lambda p, bi: (bi, 0, p))
    vspec = pl.BlockSpec((ns, LANES, vt_pad.shape[2]), lambda p, bi: (bi, p, 0))
    cspec = pl.BlockSpec((1, 2, cvec.shape[2]), lambda p, bi: (p, 0, 0))
    return pl.pallas_call(
        functools.partial(_band_t_kernel, tq=tq, t=t),
        grid=(pairs, b // ns),
        in_specs=[cspec, qspec, kspec, vspec],
        out_specs=qspec,
        out_shape=jax.ShapeDtypeStruct(q.shape, BF16),
        scratch_shapes=[pltpu.VMEM((ns, 2, 1, 2 * tq), F32),
                        pltpu.VMEM((2, window, tq), F32)]
                       + [pltpu.VMEM((window, 2 * tq), F32)] * (2 * ns),
        compiler_params=_params(2),
        name="band_t_attn",
    )(cvec, q, k_pad, vt_pad)


def _proj_odd_kernel(x_ref, g_ref, w_ref, wt_ref, q_ref, k_ref, v_ref, kt_ref, vt_ref, *,
                     tiles, pad_tiles, v_feature_major):
    r = pl.program_id(1)

    if pad_tiles:
        @pl.when(r < pad_tiles)
        def _():
            k_ref[...] = jnp.zeros(k_ref.shape, BF16)
            v_ref[...] = jnp.zeros(v_ref.shape, BF16)

    @pl.when(r >= pad_tiles)
    def _():
        h = _rms(x_ref[...], g_ref[...]).astype(BF16)

        def seg(i):
            return jnp.dot(h, w_ref[:, i * D_MODEL:(i + 1) * D_MODEL], preferred_element_type=F32)

        def seg_t(i):
            return lax.dot_general(wt_ref[i], h, _NT, preferred_element_type=F32)

        is_tail = r == pad_tiles + tiles - 1
        q_ref[...] = (seg(0) * QK_SCALE).astype(BF16)
        k_ref[...] = seg(1).astype(BF16)
        if v_feature_major:
            vzt = seg_t(1)
            v_ref[0] = vzt.astype(BF16)

            @pl.when(is_tail)
            def _():
                kt_ref[0] = seg_t(0)
                vt_ref[0] = vzt
        else:
            v_ref[...] = seg(2).astype(BF16)

            @pl.when(is_tail)
            def _():
                kt_ref[0] = seg_t(0)
                vt_ref[0] = seg_t(1)


def _proj_odd(x, g, w, w_t, *, tiles, pad_tiles, v_feature_major):
    m = x.shape[0]
    tm = min(ROW_TILE, m)
    nb = m // tm // tiles
    steps = pad_tiles + tiles
    src = lambda b, r: (b * tiles + jnp.maximum(r - pad_tiles, 0), 0)
    row = pl.BlockSpec((tm, D_MODEL), src)
    padded_row = pl.BlockSpec((tm, D_MODEL), lambda b, r: (b * steps + r, 0))
    tail = pl.BlockSpec((1, D_MODEL, tm), lambda b, r: (b, 0, 0))
    bfo = jax.ShapeDtypeStruct((m, D_MODEL), BF16)
    padded = jax.ShapeDtypeStruct((nb * steps * tm, D_MODEL), BF16)
    tailo = jax.ShapeDtypeStruct((nb, D_MODEL, tm), F32)
    if v_feature_major:
        vo = jax.ShapeDtypeStruct((nb, D_MODEL, steps * tm), BF16)
        vspec = pl.BlockSpec((1, D_MODEL, tm), lambda b, r: (b, 0, r))
    else:
        vo, vspec = padded, padded_row
    return pl.pallas_call(
        functools.partial(_proj_odd_kernel, tiles=tiles, pad_tiles=pad_tiles, v_feature_major=v_feature_major),
        grid=(nb, steps),
        in_specs=[row, _const_spec((1, D_MODEL)), _const_spec(w.shape), _const_spec(w_t.shape)],
        out_specs=[row, padded_row, vspec, tail, tail],
        out_shape=[bfo, padded, vo, tailo, tailo],
        compiler_params=_params(2),
        name="proj_odd",
    )(x, g, w, w_t)


def _post_kernel(*refs, n_o):
    o_refs = refs[:n_o]
    x_ref, wo_ref, gpm_ref, gpre_ref, wup_ref, wdn_ref, gpf_ref, out_ref = refs[n_o:]
    o = o_refs[0][...] if n_o == 1 else jnp.concatenate([r[...] for r in o_refs], axis=1)
    mixed = jnp.dot(o, wo_ref[...], preferred_element_type=F32)
    x1 = x_ref[...] + _rms(mixed, gpm_ref[...])
    h = _rms(x1, gpre_ref[...]).astype(BF16)
    acc = jnp.zeros(x1.shape, F32)
    for c in range(D_FF // FF_CHUNK):
        u = jnp.dot(h, wup_ref[:, c * FF_CHUNK:(c + 1) * FF_CHUNK], preferred_element_type=F32)
        u = jnp.maximum(u, 0.0)
        acc = acc + jnp.dot((u * u).astype(BF16), wdn_ref[c * FF_CHUNK:(c + 1) * FF_CHUNK, :],
                            preferred_element_type=F32)
    out_ref[...] = x1 + _rms(acc, gpf_ref[...])


def _post(o_list, x, wo, gpm, gpre, wup, wdn, gpf):
    m = x.shape[0]
    tm = min(ROW_TILE, m)
    row = lambda w: pl.BlockSpec((tm, w), lambda i: (i, 0))
    gspec = _const_spec((1, D_MODEL))
    return pl.pallas_call(
        functools.partial(_post_kernel, n_o=len(o_list)),
        grid=(m // tm,),
        in_specs=[row(o.shape[1]) for o in o_list]
                 + [row(D_MODEL), _const_spec(wo.shape), gspec, gspec,
                    _const_spec(wup.shape), _const_spec(wdn.shape), gspec],
        out_specs=row(D_MODEL),
        out_shape=jax.ShapeDtypeStruct((m, D_MODEL), F32),
        compiler_params=_params(1),
        name="post_mix_mlp",
    )(*o_list, x, wo, gpm, gpre, wup, wdn, gpf)


def _band_bias_vectors(table):
    n = 2 * BAND_PAST
    m = np.arange(n)
    d = np.where(m <= n - REL_CLIP, -m, n - m)
    idx = np.clip(d + BAND_PAST, -REL_CLIP, REL_CLIP) + REL_CLIP
    return jnp.take(table.astype(F32), jnp.asarray(idx, jnp.int32), axis=1).reshape(H_C // 2, 2, n)


def kernel(x_prompt, x_sample, cache_a_k, cache_a_v, cache_b_k, cache_b_v, cache_b_logf, cache_c_k, cache_c_v,
           w_in_even, b_forget, lam_q1, lam_k1, lam_q2, lam_k2, subln_g, w_out_even, w_in_odd, rel_bias,
           w_out_odd, g_pre_mix, g_post_mix, g_pre_ffn, g_post_ffn, w_ffn_up, w_ffn_down):
    b_p, t_p, _ = x_prompt.shape
    b_s, t_s, _ = x_sample.shape
    past = cache_b_logf.shape[2]
    m_p, m_s = b_p * t_p, b_s * t_s
    xp = x_prompt.reshape(m_p, D_MODEL)
    xs = x_sample.reshape(m_s, D_MODEL)
    gvec = lambda a, l: a[l].reshape(1, D_MODEL)

    lam_init = 0.8 - 0.6 * math.exp(-0.3 * 0)
    w_in_t = w_in_even[0].T
    w_t = w_in_t[:N_SEG * A_W].astype(BF16)
    wft = jnp.pad(w_in_t[N_SEG * A_W:], ((0, 16 - H_B), (0, 0))).astype(BF16)
    bf_col = b_forget[0].reshape(H_B, 1)
    slopes = 2.0 ** (-8.0 * jnp.arange(1, H_A + 1, dtype=F32) / H_A)
    lamv = jnp.stack([lam_q1[0], lam_k1[0], lam_q2[0], lam_k2[0]]).astype(F32)
    sub_g = subln_g[0].reshape(1, 2 * HEAD_DIM)
    wo0 = w_out_even[0].astype(BF16)
    wup0, wdn0 = w_ffn_up[0].astype(BF16), w_ffn_down[0].astype(BF16)

    def even_layer(x, bsz, t, caches):
        m = x.shape[0]
        tm = min(ROW_TILE, m)
        (qa, ka, kab, va, vab, qb, kb, kbb, vb, vbb, lft) = _proj_even(
            x, gvec(g_pre_mix, 0), w_t, wft, bf_col,
            tiles_per_batch=(t // tm) if caches is None else None)
        r3 = lambda a: a.reshape(bsz, t, A_W)
        lft = lft.transpose(1, 0, 2).reshape(H_B, bsz, t).transpose(1, 0, 2)
        tok_head = lambda a: a.reshape(1, bsz, t, H_A, 2 * HEAD_DIM)
        if caches is None:
            oa = _flash_t("diff", r3(qa), r3(kab), vab, (slopes, lamv, sub_g), tq=512, lam_init=lam_init)
            ob = _flash_t("fox", r3(qb), r3(kbb), vbb, (_cumsum_time(lft),), tq=512)
            time_minor = lambda a: a.reshape(bsz, H_B, HEAD_DIM, t).transpose(0, 3, 1, 2)[None]
            new = (tok_head(ka), tok_head(va), time_minor(kb), time_minor(vb), lft.transpose(0, 2, 1)[None])
        else:
            c_ka, c_va, c_kb, c_vb, c_lf = caches
            p_len = c_ka.shape[1]
            t_all = p_len + t
            t_pad = -(-t_all // LANES) * LANES
            assert p_len % CHUNK == 0 and t <= CHUNK
            lf_all = jnp.concatenate([c_lf.astype(F32).transpose(0, 2, 1), lft], axis=2)
            cum = _cumsum_time(jnp.pad(lf_all, ((0, 0), (0, 0), (0, t_pad - t_all))))
            tok_head_rows = lambda c: c.reshape(bsz, p_len * H_A, 2 * HEAD_DIM)
            feat_major = lambda c: c.transpose(0, 2, 3, 1).reshape(bsz, A_W, p_len)
            oa = _sample_attn("diff", r3(qa), tok_head_rows(c_ka), tok_head_rows(c_va), r3(kab), r3(vab),
                              (slopes, lamv, sub_g), past=p_len, lam_init=lam_init)
            ob = _sample_attn("fox", r3(qb), feat_major(c_kb), feat_major(c_vb), r3(kbb), r3(vbb),
                              (cum,), past=p_len)
            new = (tok_head(ka), tok_head(va), kb.reshape(1, bsz, t, H_B, HEAD_DIM),
                   vb.reshape(1, bsz, t, H_B, HEAD_DIM), lft.transpose(0, 2, 1)[None])
        x_out = _post([oa.reshape(m, A_W), ob.reshape(m, A_W)], x, wo0, gvec(g_post_mix, 0),
                      gvec(g_pre_ffn, 0), wup0, wdn0, gvec(g_post_ffn, 0))
        return x_out, new

    xp, new_p_even = even_layer(xp, b_p, t_p, None)
    xs, new_s_even = even_layer(xs, b_s, t_s, (cache_a_k[0], cache_a_v[0], cache_b_k[0], cache_b_v[0],
                                              cache_b_logf[0]))

    w_odd = w_in_odd[0].astype(BF16)
    w_odd_t = jnp.stack([w_in_odd[0][:, s * D_MODEL:(s + 1) * D_MODEL].T for s in (1, 2)]).astype(BF16)
    cvec = _band_bias_vectors(rel_bias[0])
    to_cache = lambda a: a.reshape(a.shape[0], H_C, HEAD_DIM, a.shape[2]).transpose(0, 3, 1, 2)[None]
    from_cache = lambda c: c.transpose(0, 2, 3, 1).reshape(c.shape[0], D_MODEL, c.shape[1])
    wo1 = w_out_odd[0].astype(BF16)
    wup1, wdn1 = w_ffn_up[1].astype(BF16), w_ffn_down[1].astype(BF16)

    def odd_layer(x, bsz, t, caches):
        m = x.shape[0]
        tm = min(ROW_TILE, m)
        r3 = lambda a: a.reshape(bsz, t, D_MODEL)
        if caches is None:
            assert BAND_PAST % tm == 0 and tm == min(BAND_PAST, t)
            q, kp, vtp, k_tail, v_tail = _proj_odd(x, gvec(g_pre_mix, 1), w_odd, w_odd_t, tiles=t // tm,
                                                   pad_tiles=BAND_PAST // tm, v_feature_major=True)
            o = _band_t(r3(q), kp.reshape(bsz, BAND_PAST + t, D_MODEL), vtp, cvec, tq=256)
            new = (to_cache(k_tail), to_cache(v_tail))
        else:
            c_k, c_v = caches
            assert c_k.shape[1] == BAND_PAST and past % CHUNK == 0 and past >= BAND_PAST and t <= CHUNK
            q, k, v, k_tail, v_tail = _proj_odd(x, gvec(g_pre_mix, 1), w_odd, w_odd_t, tiles=m // tm,
                                                pad_tiles=0, v_feature_major=False)
            o = _sample_attn("band", r3(q), from_cache(c_k), from_cache(c_v), r3(k), r3(v), (cvec,), past=past)
            upd = lambda c, tail: to_cache(jnp.concatenate(
                [from_cache(c)[:, :, t:], tail.reshape(D_MODEL, bsz, t).transpose(1, 0, 2)], axis=2))
            new = (upd(c_k, k_tail), upd(c_v, v_tail))
        x_out = _post([o.reshape(m, D_MODEL)], x, wo1, gvec(g_post_mix, 1), gvec(g_pre_ffn, 1),
                      wup1, wdn1, gvec(g_post_ffn, 1))
        return x_out, new

    xp, new_p_odd = odd_layer(xp, b_p, t_p, None)
    xs, new_s_odd = odd_layer(xs, b_s, t_s, (cache_c_k[0], cache_c_v[0]))

    return (xp.reshape(b_p, t_p, D_MODEL), xs.reshape(b_s, t_s, D_MODEL),
            *new_p_even, *new_p_odd, *new_s_even, *new_s_odd)
```

```python
import functools
import math

import numpy as np
import jax
import jax.numpy as jnp
from jax import lax
from jax.experimental import pallas as pl
from jax.experimental.pallas import tpu as pltpu

F32 = jnp.float32
BF16 = jnp.bfloat16

D_MODEL = 1024
HEAD_DIM = 64
CHUNK = 64
H_A = 4
H_B = 8
H_C = 16
A_W = 512
N_SEG = 6
BAND_PAST = 512
REL_CLIP = 256
D_FF = 4 * D_MODEL
NORM_EPS = 1e-6
NEG_INF = -1e30
QK_SCALE = HEAD_DIM ** -0.5

LANES = 128
ROW_TILE = 512
FF_CHUNK = 1024
VMEM_LIMIT = 56 * 1024 * 1024

_NT = (((1,), (1,)), ((), ()))


def _rms(x, g):
    ms = jnp.mean(x * x, axis=-1, keepdims=True)
    return x * lax.rsqrt(ms + NORM_EPS) * g


def _log_sigmoid(x):
    t = -x
    return -(jnp.maximum(t, 0.0) + jnp.log1p(jnp.exp(-jnp.abs(t))))


def _const_spec(shape):
    nd = len(shape)
    return pl.BlockSpec(shape, lambda *_: (0,) * nd, pipeline_mode=pl.Buffered(1))


def _params(n_axes):
    return pltpu.CompilerParams(dimension_semantics=("arbitrary",) * n_axes,
                                vmem_limit_bytes=VMEM_LIMIT)


def _proj_even_kernel(x_ref, g_ref, wt_ref, wft_ref, bft_ref,
                      qa_ref, ka_ref, kab_ref, va_ref, vab_ref,
                      qb_ref, kb_ref, kbb_ref, vb_ref, vbb_ref, lft_ref, *, feature_major):
    h = _rms(x_ref[...], g_ref[...]).astype(BF16)
    tm = h.shape[0]

    def seg(i):
        return lax.dot_general(h, wt_ref[i * A_W:(i + 1) * A_W, :], _NT, preferred_element_type=F32)

    def seg_t(i):
        return lax.dot_general(wt_ref[i * A_W:(i + 1) * A_W, :], h, _NT, preferred_element_type=F32)

    def store_token_head_rows(ref, z):
        for hd in range(H_A):
            ref[pl.ds(hd, tm, stride=H_A), :] = z[:, hd * LANES:(hd + 1) * LANES]

    qa_ref[...] = (seg(0) * QK_SCALE).astype(BF16)
    z = seg(1)
    store_token_head_rows(ka_ref, z)
    kab_ref[...] = z.astype(BF16)
    z = seg(2)
    store_token_head_rows(va_ref, z)
    if feature_major:
        vab_ref[0] = z.T.astype(BF16)
    else:
        vab_ref[...] = z.astype(BF16)
    qb_ref[...] = (seg(3) * QK_SCALE).astype(BF16)
    z = seg(4)
    kbb_ref[...] = z.astype(BF16)
    if feature_major:
        kb_ref[0] = z.T
        zt = seg_t(5)
        vb_ref[0] = zt
        vbb_ref[0] = zt.astype(BF16)
    else:
        kb_ref[...] = z
        z = seg(5)
        vb_ref[...] = z
        vbb_ref[...] = z.astype(BF16)
    fzt = lax.dot_general(wft_ref[...], h, _NT, preferred_element_type=F32)
    lft_ref[0] = _log_sigmoid(fzt[:H_B] + bft_ref[...])


def _proj_even(x, g, w_t, wft, bf_col, *, tiles_per_batch=None):
    m = x.shape[0]
    tm = min(ROW_TILE, m)
    n = m // tm
    row = lambda w: pl.BlockSpec((tm, w), lambda i: (i, 0))
    bfo = jax.ShapeDtypeStruct((m, A_W), BF16)
    tok_head = jax.ShapeDtypeStruct((m * H_A, LANES), F32)
    tok_head_spec = pl.BlockSpec((tm * H_A, LANES), lambda i: (i, 0))
    if tiles_per_batch:
        tpb = tiles_per_batch
        fm = lambda dt: jax.ShapeDtypeStruct((n // tpb, A_W, tpb * tm), dt)
        fm_spec = pl.BlockSpec((1, A_W, tm), lambda i: (i // tpb, 0, i % tpb))
        outs = [(bfo, row(A_W)), (tok_head, tok_head_spec), (bfo, row(A_W)), (tok_head, tok_head_spec),
                (fm(BF16), fm_spec), (bfo, row(A_W)), (fm(F32), fm_spec), (bfo, row(A_W)),
                (fm(F32), fm_spec), (fm(BF16), fm_spec)]
    else:
        f32o = jax.ShapeDtypeStruct((m, A_W), F32)
        outs = [(bfo, row(A_W)), (tok_head, tok_head_spec), (bfo, row(A_W)), (tok_head, tok_head_spec),
                (bfo, row(A_W)), (bfo, row(A_W)), (f32o, row(A_W)), (bfo, row(A_W)),
                (f32o, row(A_W)), (bfo, row(A_W))]
    outs.append((jax.ShapeDtypeStruct((n, H_B, tm), F32), pl.BlockSpec((1, H_B, tm), lambda i: (i, 0, 0))))
    return pl.pallas_call(
        functools.partial(_proj_even_kernel, feature_major=bool(tiles_per_batch)),
        grid=(n,),
        in_specs=[row(D_MODEL), _const_spec((1, D_MODEL)), _const_spec(w_t.shape),
                  _const_spec(wft.shape), _const_spec(bf_col.shape)],
        out_specs=[s for _, s in outs],
        out_shape=[o for o, _ in outs],
        compiler_params=_params(1),
        name="proj_even",
    )(x, g, w_t, wft, bf_col)


def _cumsum_kernel(x_ref, o_ref):
    x = x_ref[0]
    t = x.shape[1]
    col = lax.broadcasted_iota(jnp.int32, x.shape, 1)
    s = 1
    while s < t:
        x = x + jnp.where(col >= s, pltpu.roll(x, s, 1), 0.0)
        s *= 2
    o_ref[0] = x


def _cumsum_time(x):
    b, h, t = x.shape
    spec = pl.BlockSpec((1, h, t), lambda i: (i, 0, 0))
    return pl.pallas_call(
        _cumsum_kernel, grid=(b,), in_specs=[spec], out_specs=spec,
        out_shape=jax.ShapeDtypeStruct(x.shape, F32),
        compiler_params=_params(1), name="cumsum_time",
    )(x)


def _stack_pair(q):
    q32 = q.astype(F32)
    lo = lax.broadcasted_iota(jnp.int32, q32.shape, 1) < HEAD_DIM
    return jnp.concatenate([jnp.where(lo, q32, 0.0), jnp.where(lo, 0.0, q32)], axis=0).astype(BF16)


def _sample_attn_kernel(*refs, mode, t, past, lam_init):
    if mode == "diff":
        slopes_ref, lamv_ref, g_ref, q_ref, kc_ref, vc_ref, kn_ref, vn_ref, o_ref = refs
    elif mode == "fox":
        cum_ref, q_ref, kc_ref, vc_ref, kn_ref, vn_ref, o_ref = refs
    else:
        c_ref, q_ref, kc_ref, vc_ref, kn_ref, vn_ref, knt_ref, vnt_ref, o_ref, kco_ref, vco_ref = refs
        kco_ref[0] = jnp.concatenate([kc_ref[0, :, t:], knt_ref[0]], axis=1)
        vco_ref[0] = jnp.concatenate([vc_ref[0, :, t:], vnt_ref[0]], axis=1)
    pairs = q_ref.shape[2] // LANES
    n_c = vc_ref.shape[1] // pairs if mode == "diff" else kc_ref.shape[2]
    row = lax.broadcasted_iota(jnp.int32, (t, n_c), 0)
    col = lax.broadcasted_iota(jnp.int32, (t, n_c), 1)
    row_n = lax.broadcasted_iota(jnp.int32, (t, t), 0)
    col_n = lax.broadcasted_iota(jnp.int32, (t, t), 1)
    lo = lax.broadcasted_iota(jnp.int32, (t, LANES), 1) < HEAD_DIM

    for p in range(pairs):
        lanes = slice(p * LANES, (p + 1) * LANES)
        q2 = _stack_pair(q_ref[0, :, lanes])
        kn = kn_ref[0, :, lanes]
        vn = vn_ref[0, :, lanes]
        if mode == "diff":
            kc = kc_ref[0, pl.ds(p, n_c, stride=pairs), :].astype(BF16)
            vc = vc_ref[0, pl.ds(p, n_c, stride=pairs), :].astype(BF16)
            s_c = lax.dot_general(q2, kc, _NT, preferred_element_type=F32)
        else:
            kc = kc_ref[0, lanes, :].astype(BF16)
            vc = vc_ref[0, lanes, :].astype(BF16)
            s_c = jnp.dot(q2, kc, preferred_element_type=F32)
        s_n = lax.dot_general(q2, kn, _NT, preferred_element_type=F32)

        outs = []
        for a in range(2):
            rows = slice(a * t, (a + 1) * t)
            if mode == "diff":
                slope = slopes_ref[p]
                t_c = s_c[rows] - slope * (past + row - col).astype(F32)
                t_n = s_n[rows] - slope * jnp.abs(row_n - col_n).astype(F32)
            elif mode == "fox":
                ck = cum_ref[0, 2 * p + a]
                t_c = s_c[rows] - ck[:, :n_c]
                t_n = jnp.where(col_n <= row_n, s_n[rows] - ck[:, n_c:n_c + t], NEG_INF)
            else:
                c = jnp.broadcast_to(c_ref[p, a:a + 1, :], (t, c_ref.shape[2]))
                bias = pltpu.roll(c, 0, 1, stride=1, stride_axis=0)
                t_c = s_c[rows] + bias[:, :n_c]
                t_n = s_n[rows] + bias[:, n_c:n_c + t]
            m = jnp.maximum(jnp.max(t_c, axis=1, keepdims=True), jnp.max(t_n, axis=1, keepdims=True))
            p_c = jnp.exp(t_c - m)
            p_n = jnp.exp(t_n - m)
            l = jnp.sum(p_c, axis=1, keepdims=True) + jnp.sum(p_n, axis=1, keepdims=True)
            if mode == "diff":
                acc = jnp.dot(p_c.astype(BF16), vc, preferred_element_type=F32)
            else:
                acc = lax.dot_general(p_c.astype(BF16), vc, _NT, preferred_element_type=F32)
            acc = acc + jnp.dot(p_n.astype(BF16), vn, preferred_element_type=F32)
            outs.append(acc / l)
        if mode == "diff":
            lv = lamv_ref[...]
            lam = (jnp.exp(jnp.sum(lv[0:1] * lv[1:2], axis=1, keepdims=True))
                   - jnp.exp(jnp.sum(lv[2:3] * lv[3:4], axis=1, keepdims=True)) + lam_init)
            oa = outs[0] - lam * outs[1]
            o_ref[0, :, lanes] = (_rms(oa, g_ref[...]) * (1.0 - lam_init)).astype(BF16)
        else:
            o_ref[0, :, lanes] = jnp.where(lo, outs[0], outs[1]).astype(BF16)


def _sample_attn(mode, q, kc, vc, kn, vn, extras, *, past, lam_init=0.0, new_t=()):
    b, t, width = q.shape
    whole = lambda a: pl.BlockSpec((1,) + a.shape[1:], lambda i: (i,) + (0,) * (a.ndim - 1))
    const = lambda a: pl.BlockSpec(a.shape, lambda i: (0,) * a.ndim)
    if mode == "diff":
        slopes, lamv, g = extras
        extra_specs = [pl.BlockSpec(memory_space=pltpu.SMEM), const(lamv), const(g)]
    elif mode == "fox":
        extras = (extras[0][:, :, None, :],)
        extra_specs = [whole(extras[0])]
    else:
        extra_specs = [const(extras[0])]
    operands = [q, kc, vc, kn, vn]
    out_specs, out_shape = whole(q), jax.ShapeDtypeStruct(q.shape, BF16)
    if mode == "band":
        operands += list(new_t)
        out_specs = [out_specs, whole(kc), whole(vc)]
        out_shape = [out_shape, jax.ShapeDtypeStruct(kc.shape, F32), jax.ShapeDtypeStruct(vc.shape, F32)]
    return pl.pallas_call(
        functools.partial(_sample_attn_kernel, mode=mode, t=t, past=past, lam_init=lam_init),
        grid=(b,),
        in_specs=extra_specs + [whole(a) for a in operands],
        out_specs=out_specs,
        out_shape=out_shape,
        compiler_params=_params(1),
        name="sample_" + mode,
    )(*extras, *operands)


ONES_ROWS = 16


def _with_ones(vt):
    return jnp.concatenate([vt, jnp.ones((ONES_ROWS, vt.shape[1]), BF16)], axis=0)


AUG_TERMS = 3
FLASH_STREAMS = 4
BAND_STREAMS = 2


def _lane_terms(lane, first, terms):
    out = jnp.zeros(terms[0].shape, F32)
    for t, v in enumerate(terms):
        out = jnp.where(lane == first + t, v, out)
    return out


def _flash_t_kernel(*refs, mode, tq, lam_init):
    tk = tq
    n_extra = 3 if mode == "diff" else 1
    extra_refs, (q_ref, k_ref, vt_ref, o_ref, m_sc, acc_sc, q2_sc, mu_sc, kaug_sc, bdiag_sc) = (
        refs[:n_extra], refs[n_extra:n_extra + 10])
    s_bufs = refs[n_extra + 10:]
    if mode == "diff":
        slopes_ref, lamv_ref, g_ref = extra_refs
    else:
        (cum_ref,) = extra_refs
    pair = pl.program_id(0)
    qi = pl.program_id(2)
    t_all = kaug_sc.shape[1]
    streams = range(FLASH_STREAMS)

    @pl.when((pl.program_id(1) == 0) & (qi == 0))
    def _():
        row = lax.broadcasted_iota(jnp.int32, (tk, tq), 0)
        col = lax.broadcasted_iota(jnp.int32, (tk, tq), 1)
        lane_q = lax.broadcasted_iota(jnp.int32, (2 * tq, LANES), 1)
        first_q = jnp.where(lax.broadcasted_iota(jnp.int32, (2 * tq, LANES), 0) < tq, 0, AUG_TERMS)
        if mode == "diff":
            slope = slopes_ref[pair]
            visible = (row >> 6) <= (col >> 6)
            bdiag_sc[...] = jnp.where(visible, -2.0 * slope * jnp.maximum(row - col, 0).astype(F32), NEG_INF)
            for s in streams:
                q2_sc[s, :, LANES:] = jnp.where(lane_q < AUG_TERMS, 1.0, 0.0).astype(BF16)
            for c0 in range(0, t_all, tk):
                j = c0 + lax.broadcasted_iota(jnp.int32, (tk, LANES), 0)
                lane = lax.broadcasted_iota(jnp.int32, (tk, LANES), 1)
                terms = [((j >> 8) << 8).astype(F32), (((j >> 4) & 15) << 4).astype(F32), (j & 15).astype(F32)]
                kaug_sc[0, c0:c0 + tk, :] = (slope * _lane_terms(lane, 0, terms)).astype(BF16)
        else:
            bdiag_sc[...] = jnp.where(row <= col, 0.0, NEG_INF)
            mine = (lane_q >= first_q) & (lane_q < first_q + AUG_TERMS)
            for s in streams:
                q2_sc[s, :, LANES:] = jnp.where(mine, -1.0, 0.0).astype(BF16)

    if mode == "fox":
        @pl.when(qi == 0)
        def _():
            for s in streams:
                for c0 in range(0, t_all, tk):
                    lane = lax.broadcasted_iota(jnp.int32, (tk, LANES), 1)
                    feat = jnp.zeros((tk, LANES), F32)
                    for a in range(2):
                        r = cum_ref[s, 2 * pair + a, :, c0:c0 + tk]
                        ck = jnp.broadcast_to(r, (LANES, tk)).T
                        hi = ck.astype(BF16).astype(F32)
                        mid = (ck - hi).astype(BF16).astype(F32)
                        lo = ((ck - hi) - mid).astype(BF16).astype(F32)
                        feat = feat + _lane_terms(lane, AUG_TERMS * a, [hi, mid, lo])
                    kaug_sc[s, c0:c0 + tk, :] = feat.astype(BF16)

    for s in streams:
        q2_sc[s, :, :LANES] = _stack_pair(q_ref[s])
    m_sc[...] = jnp.full(m_sc.shape, NEG_INF, F32)
    acc_sc[...] = jnp.zeros(acc_sc.shape, F32)

    def stage_a(k0, slot, diag):
        for s in streams:
            kaug = kaug_sc[0 if mode == "diff" else s, pl.ds(k0, tk), :]
            kk = jnp.concatenate([k_ref[s, pl.ds(k0, tk), :], kaug], axis=1)
            st = lax.dot_general(kk, q2_sc[s], _NT, preferred_element_type=F32)
            for a in range(2):
                sa = st[:, a * tq:(a + 1) * tq]
                if diag:
                    sa = sa + bdiag_sc[...]
                s_bufs[2 * s + slot][:, a * tq:(a + 1) * tq] = sa
                mu_sc[s, slot, :, a * tq:(a + 1) * tq] = jnp.max(sa, axis=0, keepdims=True)

    def stage_b(k0, slot):
        for s in streams:
            vt = vt_ref[s, :, pl.ds(k0, tk)]
            for a in range(2):
                m_prev = m_sc[s, a]
                m_next = jnp.maximum(m_prev, mu_sc[s, slot, :, a * tq:(a + 1) * tq])
                p = jnp.exp(s_bufs[2 * s + slot][:, a * tq:(a + 1) * tq] - m_next)
                alpha = jnp.exp(m_prev - m_next)
                vta = _with_ones(vt if mode == "diff" else vt[a * HEAD_DIM:(a + 1) * HEAD_DIM])
                acc_sc[s, a] = alpha * acc_sc[s, a] + jnp.dot(vta, p.astype(BF16), preferred_element_type=F32)
                m_sc[s, a] = m_next

    blk = lambda j: pl.multiple_of(j * tk, tk)
    pl.when(qi == 0)(lambda: stage_a(0, 0, True))
    pl.when(qi > 0)(lambda: stage_a(0, 0, False))

    def two_blocks(i, carry):
        stage_a(blk(2 * i + 1), 1, False)
        stage_b(blk(2 * i), 0)
        stage_a(blk(2 * i + 2), 0, False)
        stage_b(blk(2 * i + 1), 1)
        return carry

    lax.fori_loop(0, (qi - 1) // 2, two_blocks, 0)

    @pl.when(qi == 0)
    def _():
        stage_b(0, 0)

    @pl.when(qi % 2 == 1)
    def _():
        stage_a(blk(qi), 1, True)
        stage_b(blk(qi - 1), 0)
        stage_b(blk(qi), 1)

    @pl.when((qi % 2 == 0) & (qi > 0))
    def _():
        stage_a(blk(qi - 1), 1, False)
        stage_b(blk(qi - 2), 0)
        stage_a(blk(qi), 0, True)
        stage_b(blk(qi - 1), 1)
        stage_b(blk(qi), 0)

    d = acc_sc.shape[2] - ONES_ROWS
    for s in streams:
        o0 = acc_sc[s, 0, :d, :] / acc_sc[s, 0, d:d + 1, :]
        o1 = acc_sc[s, 1, :d, :] / acc_sc[s, 1, d:d + 1, :]
        if mode == "diff":
            lv = lamv_ref[...]
            lam = (jnp.exp(jnp.sum(lv[0:1] * lv[1:2], axis=1, keepdims=True))
                   - jnp.exp(jnp.sum(lv[2:3] * lv[3:4], axis=1, keepdims=True)) + lam_init)
            oa = o0 - lam * o1
            ms = jnp.mean(oa * oa, axis=0, keepdims=True)
            y = (oa * lax.rsqrt(ms + NORM_EPS)).T * g_ref[...]
            o_ref[s] = (y * (1.0 - lam_init)).astype(BF16)
        else:
            o_ref[s] = jnp.concatenate([o0, o1], axis=0).T.astype(BF16)


def _flash_t(mode, q, k, vt, extras, *, tq, lam_init=0.0):
    b, t, width = q.shape
    pairs = width // LANES
    ns = FLASH_STREAMS
    assert b % ns == 0
    qspec = pl.BlockSpec((ns, tq, LANES), lambda p, bi, i: (bi, i, p))
    kspec = pl.BlockSpec((ns, t, LANES), lambda p, bi, i: (bi, 0, p))
    vspec = pl.BlockSpec((ns, LANES, t), lambda p, bi, i: (bi, p, 0))
    acc_rows = (LANES if mode == "diff" else HEAD_DIM) + ONES_ROWS
    scratch = [pltpu.VMEM((ns, 2, 1, tq), F32),
               pltpu.VMEM((ns, 2, acc_rows, tq), F32),
               pltpu.VMEM((ns, 2 * tq, 2 * LANES), BF16),
               pltpu.VMEM((ns, 2, 1, 2 * tq), F32),
               pltpu.VMEM((1 if mode == "diff" else ns, t, LANES), BF16),
               pltpu.VMEM((tq, tq), F32)]
    scratch += [pltpu.VMEM((tq, 2 * tq), F32)] * (2 * ns)
    if mode == "diff":
        slopes, lamv, g = extras
        extra_specs = [pl.BlockSpec(memory_space=pltpu.SMEM),
                       pl.BlockSpec(lamv.shape, lambda p, bi, i: (0, 0)),
                       pl.BlockSpec(g.shape, lambda p, bi, i: (0, 0))]
    else:
        extras = (extras[0][:, :, None, :],)
        extra_specs = [pl.BlockSpec((ns,) + extras[0].shape[1:], lambda p, bi, i: (bi, 0, 0, 0))]
    return pl.pallas_call(
        functools.partial(_flash_t_kernel, mode=mode, tq=tq, lam_init=lam_init),
        grid=(pairs, b // ns, t // tq),
        in_specs=extra_specs + [qspec, kspec, vspec],
        out_specs=qspec,
        out_shape=jax.ShapeDtypeStruct(q.shape, BF16),
        scratch_shapes=scratch,
        compiler_params=_params(3),
        name="flash_t_" + mode,
    )(*extras, q, k, vt)


def _band_t_kernel(c_ref, q_ref, k_ref, vt_ref, o_ref, mu_sc, bias_sc, *s_bufs, tq, t):
    window = BAND_PAST + tq
    n_q = t // tq
    streams = range(BAND_STREAMS)

    @pl.when(pl.program_id(1) == 0)
    def _():
        row = lax.broadcasted_iota(jnp.int32, (tq, window), 0)
        col = lax.broadcasted_iota(jnp.int32, (tq, window), 1)
        visible = ((col >> 6) >= (row >> 6)) & ((col >> 6) <= (row >> 6) + BAND_PAST // CHUNK)
        for a in range(2):
            c = jnp.broadcast_to(c_ref[0, a:a + 1, :], (tq, c_ref.shape[2]))
            bias = pltpu.roll(c, 0, 1, stride=1, stride_axis=0)[:, :window]
            bias_sc[a] = jnp.where(visible, bias, NEG_INF).T

    lane_q = lax.broadcasted_iota(jnp.int32, (2 * tq, LANES), 1)
    q_extra = jnp.where(lane_q == 0, 1.0, 0.0).astype(BF16)

    def stage_a(qb, slot):
        w0 = pl.multiple_of(qb * tq, tq)
        pad_row = (w0 + lax.broadcasted_iota(jnp.int32, (window, LANES), 0)) < BAND_PAST
        lane_k = lax.broadcasted_iota(jnp.int32, (window, LANES), 1)
        k_extra = jnp.where(pad_row & (lane_k == 0), NEG_INF, 0.0).astype(BF16)
        for s in streams:
            q2 = jnp.concatenate([_stack_pair(q_ref[s, pl.ds(w0, tq), :]), q_extra], axis=1)
            kk = jnp.concatenate([k_ref[s, pl.ds(w0, window), :], k_extra], axis=1)
            st = lax.dot_general(kk, q2, _NT, preferred_element_type=F32)
            for a in range(2):
                u = st[:, a * tq:(a + 1) * tq] + bias_sc[a]
                s_bufs[2 * s + slot][:, a * tq:(a + 1) * tq] = u
                mu_sc[s, slot, :, a * tq:(a + 1) * tq] = jnp.max(u, axis=0, keepdims=True)

    def stage_b(qb, slot):
        w0 = pl.multiple_of(qb * tq, tq)
        for s in streams:
            vt = vt_ref[s, :, pl.ds(w0, window)]
            outs = []
            for a in range(2):
                p = jnp.exp(s_bufs[2 * s + slot][:, a * tq:(a + 1) * tq]
                            - mu_sc[s, slot, :, a * tq:(a + 1) * tq])
                vta = _with_ones(vt[a * HEAD_DIM:(a + 1) * HEAD_DIM])
                r = jnp.dot(vta, p.astype(BF16), preferred_element_type=F32)
                outs.append(r[:HEAD_DIM] / r[HEAD_DIM:HEAD_DIM + 1])
            o_ref[s, pl.ds(w0, tq), :] = jnp.concatenate(outs, axis=0).T.astype(BF16)

    stage_a(0, 0)

    def two_blocks(i, carry):
        stage_a(2 * i + 1, 1)
        stage_b(2 * i, 0)
        stage_a(2 * i + 2, 0)
        stage_b(2 * i + 1, 1)
        return carry

    lax.fori_loop(0, n_q // 2 - 1, two_blocks, 0)
    stage_a(n_q - 1, 1)
    stage_b(n_q - 2, 0)
    stage_b(n_q - 1, 1)


def _band_t(q, k_pad, vt_pad, cvec, *, tq):
    b, t, width = q.shape
    pairs = width // LANES
    window = BAND_PAST + tq
    ns = BAND_STREAMS
    assert (t // tq) % 2 == 0 and t // tq >= 2 and b % ns == 0
    qspec = pl.BlockSpec((ns, t, LANES), lambda p, bi: (bi, 0, p))
    kspec = pl.BlockSpec((ns, k_pad.shape[1], LANES), lambda p, bi: (bi, 0, p))
    vspec = pl.BlockSpec((ns, LANES, vt_pad.shape[2]), lambda p, bi: (bi, p, 0))
    cspec = pl.BlockSpec((1, 2, cvec.shape[2]), lambda p, bi: (p, 0, 0))
    return pl.pallas_call(
        functools.partial(_band_t_kernel, tq=tq, t=t),
        grid=(pairs, b // ns),
        in_specs=[cspec, qspec, kspec, vspec],
        out_specs=qspec,
        out_shape=jax.ShapeDtypeStruct(q.shape, BF16),
        scratch_shapes=[pltpu.VMEM((ns, 2, 1, 2 * tq), F32),
                        pltpu.VMEM((2, window, tq), F32)]
                       + [pltpu.VMEM((window, 2 * tq), F32)] * (2 * ns),
        compiler_params=_params(2),
        name="band_t_attn",
    )(cvec, q, k_pad, vt_pad)


def _proj_odd_kernel(x_ref, g_ref, w_ref, q_ref, k_ref, v_ref, kt_ref, vt_ref, *,
                     tiles, pad_tiles, v_feature_major):
    r = pl.program_id(1)

    if pad_tiles:
        @pl.when(r < pad_tiles)
        def _():
            k_ref[...] = jnp.zeros(k_ref.shape, BF16)
            v_ref[...] = jnp.zeros(v_ref.shape, BF16)

    @pl.when(r >= pad_tiles)
    def _():
        h = _rms(x_ref[...], g_ref[...]).astype(BF16)

        def seg(i):
            return jnp.dot(h, w_ref[:, i * D_MODEL:(i + 1) * D_MODEL], preferred_element_type=F32)

        is_tail = r == pad_tiles + tiles - 1
        q_ref[...] = (seg(0) * QK_SCALE).astype(BF16)
        kz = seg(1)
        k_ref[...] = kz.astype(BF16)
        vz = seg(2)
        if v_feature_major:
            v_ref[0] = vz.T.astype(BF16)
        else:
            v_ref[...] = vz.astype(BF16)

        @pl.when(is_tail)
        def _():
            kt_ref[0] = kz.T
            vt_ref[0] = vz.T


def _proj_odd(x, g, w, *, tiles, pad_tiles, v_feature_major):
    m = x.shape[0]
    tm = min(ROW_TILE, m)
    nb = m // tm // tiles
    steps = pad_tiles + tiles
    src = lambda b, r: (b * tiles + jnp.maximum(r - pad_tiles, 0), 0)
    row = pl.BlockSpec((tm, D_MODEL), src)
    padded_row = pl.BlockSpec((tm, D_MODEL), lambda b, r: (b * steps + r, 0))
    tail = pl.BlockSpec((1, D_MODEL, tm), lambda b, r: (b, 0, 0))
    bfo = jax.ShapeDtypeStruct((m, D_MODEL), BF16)
    padded = jax.ShapeDtypeStruct((nb * steps * tm, D_MODEL), BF16)
    tailo = jax.ShapeDtypeStruct((nb, D_MODEL, tm), F32)
    if v_feature_major:
        vo = jax.ShapeDtypeStruct((nb, D_MODEL, steps * tm), BF16)
        vspec = pl.BlockSpec((1, D_MODEL, tm), lambda b, r: (b, 0, r))
    else:
        vo, vspec = padded, padded_row
    return pl.pallas_call(
        functools.partial(_proj_odd_kernel, tiles=tiles, pad_tiles=pad_tiles, v_feature_major=v_feature_major),
        grid=(nb, steps),
        in_specs=[row, _const_spec((1, D_MODEL)), _const_spec(w.shape)],
        out_specs=[row, padded_row, vspec, tail, tail],
        out_shape=[bfo, padded, vo, tailo, tailo],
        compiler_params=_params(2),
        name="proj_odd",
    )(x, g, w)


def _post_kernel(*refs, n_o):
    o_refs = refs[:n_o]
    x_ref, wo_ref, gpm_ref, gpre_ref, wup_ref, wdn_ref, gpf_ref, out_ref = refs[n_o:]
    o = o_refs[0][...] if n_o == 1 else jnp.concatenate([r[...] for r in o_refs], axis=1)
    mixed = jnp.dot(o, wo_ref[...], preferred_element_type=F32)
    x1 = x_ref[...] + _rms(mixed, gpm_ref[...])
    h = _rms(x1, gpre_ref[...]).astype(BF16)
    acc = jnp.zeros(x1.shape, F32)
    for c in range(D_FF // FF_CHUNK):
        u = jnp.dot(h, wup_ref[:, c * FF_CHUNK:(c + 1) * FF_CHUNK], preferred_element_type=F32)
        u = jnp.maximum(u, 0.0)
        acc = acc + jnp.dot((u * u).astype(BF16), wdn_ref[c * FF_CHUNK:(c + 1) * FF_CHUNK, :],
                            preferred_element_type=F32)
    out_ref[...] = x1 + _rms(acc, gpf_ref[...])


def _post(o_list, x, wo, gpm, gpre, wup, wdn, gpf):
    m = x.shape[0]
    tm = min(ROW_TILE, m)
    row = lambda w: pl.BlockSpec((tm, w), lambda i: (i, 0))
    gspec = _const_spec((1, D_MODEL))
    return pl.pallas_call(
        functools.partial(_post_kernel, n_o=len(o_list)),
        grid=(m // tm,),
        in_specs=[row(o.shape[1]) for o in o_list]
                 + [row(D_MODEL), _const_spec(wo.shape), gspec, gspec,
                    _const_spec(wup.shape), _const_spec(wdn.shape), gspec],
        out_specs=row(D_MODEL),
        out_shape=jax.ShapeDtypeStruct((m, D_MODEL), F32),
        compiler_params=_params(1),
        name="post_mix_mlp",
    )(*o_list, x, wo, gpm, gpre, wup, wdn, gpf)


def _band_bias_vectors(table):
    n = 2 * BAND_PAST
    m = np.arange(n)
    d = np.where(m <= n - REL_CLIP, -m, n - m)
    idx = np.clip(d + BAND_PAST, -REL_CLIP, REL_CLIP) + REL_CLIP
    return jnp.take(table.astype(F32), jnp.asarray(idx, jnp.int32), axis=1).reshape(H_C // 2, 2, n)


def kernel(x_prompt, x_sample, cache_a_k, cache_a_v, cache_b_k, cache_b_v, cache_b_logf, cache_c_k, cache_c_v,
           w_in_even, b_forget, lam_q1, lam_k1, lam_q2, lam_k2, subln_g, w_out_even, w_in_odd, rel_bias,
           w_out_odd, g_pre_mix, g_post_mix, g_pre_ffn, g_post_ffn, w_ffn_up, w_ffn_down):
    b_p, t_p, _ = x_prompt.shape
    b_s, t_s, _ = x_sample.shape
    past = cache_b_logf.shape[2]
    m_p, m_s = b_p * t_p, b_s * t_s
    xp = x_prompt.reshape(m_p, D_MODEL)
    xs = x_sample.reshape(m_s, D_MODEL)
    gvec = lambda a, l: a[l].reshape(1, D_MODEL)

    lam_init = 0.8 - 0.6 * math.exp(-0.3 * 0)
    w_in_t = w_in_even[0].T
    w_t = w_in_t[:N_SEG * A_W].astype(BF16)
    wft = jnp.pad(w_in_t[N_SEG * A_W:], ((0, 16 - H_B), (0, 0))).astype(BF16)
    bf_col = b_forget[0].reshape(H_B, 1)
    slopes = 2.0 ** (-8.0 * jnp.arange(1, H_A + 1, dtype=F32) / H_A)
    lamv = jnp.stack([lam_q1[0], lam_k1[0], lam_q2[0], lam_k2[0]]).astype(F32)
    sub_g = subln_g[0].reshape(1, 2 * HEAD_DIM)
    wo0 = w_out_even[0].astype(BF16)
    wup0, wdn0 = w_ffn_up[0].astype(BF16), w_ffn_down[0].astype(BF16)

    def even_layer(x, bsz, t, caches):
        m = x.shape[0]
        tm = min(ROW_TILE, m)
        (qa, ka, kab, va, vab, qb, kb, kbb, vb, vbb, lft) = _proj_even(
            x, gvec(g_pre_mix, 0), w_t, wft, bf_col,
            tiles_per_batch=(t // tm) if caches is None else None)
        r3 = lambda a: a.reshape(bsz, t, A_W)
        lft = lft.transpose(1, 0, 2).reshape(H_B, bsz, t).transpose(1, 0, 2)
        tok_head = lambda a: a.reshape(1, bsz, t, H_A, 2 * HEAD_DIM)
        if caches is None:
            oa = _flash_t("diff", r3(qa), r3(kab), vab, (slopes, lamv, sub_g), tq=512, lam_init=lam_init)
            ob = _flash_t("fox", r3(qb), r3(kbb), vbb, (_cumsum_time(lft),), tq=512)
            time_minor = lambda a: a.reshape(bsz, H_B, HEAD_DIM, t).transpose(0, 3, 1, 2)[None]
            new = (tok_head(ka), tok_head(va), time_minor(kb), time_minor(vb), lft.transpose(0, 2, 1)[None])
        else:
            c_ka, c_va, c_kb, c_vb, c_lf = caches
            p_len = c_ka.shape[1]
            t_all = p_len + t
            t_pad = -(-t_all // LANES) * LANES
            assert p_len % CHUNK == 0 and t <= CHUNK
            lf_all = jnp.concatenate([c_lf.astype(F32).transpose(0, 2, 1), lft], axis=2)
            cum = _cumsum_time(jnp.pad(lf_all, ((0, 0), (0, 0), (0, t_pad - t_all))))
            tok_head_rows = lambda c: c.reshape(bsz, p_len * H_A, 2 * HEAD_DIM)
            feat_major = lambda c: c.transpose(0, 2, 3, 1).reshape(bsz, A_W, p_len)
            oa = _sample_attn("diff", r3(qa), tok_head_rows(c_ka), tok_head_rows(c_va), r3(kab), r3(vab),
                              (slopes, lamv, sub_g), past=p_len, lam_init=lam_init)
            ob = _sample_attn("fox", r3(qb), feat_major(c_kb), feat_major(c_vb), r3(kbb), r3(vbb),
                              (cum,), past=p_len)
            new = (tok_head(ka), tok_head(va), kb.reshape(1, bsz, t, H_B, HEAD_DIM),
                   vb.reshape(1, bsz, t, H_B, HEAD_DIM), lft.transpose(0, 2, 1)[None])
        x_out = _post([oa.reshape(m, A_W), ob.reshape(m, A_W)], x, wo0, gvec(g_post_mix, 0),
                      gvec(g_pre_ffn, 0), wup0, wdn0, gvec(g_post_ffn, 0))
        return x_out, new

    xp, new_p_even = even_layer(xp, b_p, t_p, None)
    xs, new_s_even = even_layer(xs, b_s, t_s, (cache_a_k[0], cache_a_v[0], cache_b_k[0], cache_b_v[0],
                                              cache_b_logf[0]))

    w_odd = w_in_odd[0].astype(BF16)
    cvec = _band_bias_vectors(rel_bias[0])
    to_cache = lambda a: a.reshape(a.shape[0], H_C, HEAD_DIM, a.shape[2]).transpose(0, 3, 1, 2)[None]
    from_cache = lambda c: c.transpose(0, 2, 3, 1).reshape(c.shape[0], D_MODEL, c.shape[1])
    wo1 = w_out_odd[0].astype(BF16)
    wup1, wdn1 = w_ffn_up[1].astype(BF16), w_ffn_down[1].astype(BF16)

    def odd_layer(x, bsz, t, caches):
        m = x.shape[0]
        tm = min(ROW_TILE, m)
        r3 = lambda a: a.reshape(bsz, t, D_MODEL)
        if caches is None:
            assert BAND_PAST % tm == 0 and tm == min(BAND_PAST, t)
            q, kp, vtp, k_tail, v_tail = _proj_odd(x, gvec(g_pre_mix, 1), w_odd, tiles=t // tm,
                                                   pad_tiles=BAND_PAST // tm, v_feature_major=True)
            o = _band_t(r3(q), kp.reshape(bsz, BAND_PAST + t, D_MODEL), vtp, cvec, tq=256)
            new = (to_cache(k_tail), to_cache(v_tail))
        else:
            c_k, c_v = caches
            assert c_k.shape[1] == BAND_PAST and past % CHUNK == 0 and past >= BAND_PAST and t <= CHUNK
            q, k, v, k_tail, v_tail = _proj_odd(x, gvec(g_pre_mix, 1), w_odd, tiles=m // tm,
                                                pad_tiles=0, v_feature_major=False)
            per_stream = lambda tail: tail.reshape(D_MODEL, bsz, t).transpose(1, 0, 2)
            o, k_buf, v_buf = _sample_attn("band", r3(q), from_cache(c_k), from_cache(c_v), r3(k), r3(v),
                                           (cvec,), past=past, new_t=(per_stream(k_tail), per_stream(v_tail)))
            new = (to_cache(k_buf), to_cache(v_buf))
        x_out = _post([o.reshape(m, D_MODEL)], x, wo1, gvec(g_post_mix, 1), gvec(g_pre_ffn, 1),
                      wup1, wdn1, gvec(g_post_ffn, 1))
        return x_out, new

    xp, new_p_odd = odd_layer(xp, b_p, t_p, None)
    xs, new_s_odd = odd_layer(xs, b_s, t_s, (cache_c_k[0], cache_c_v[0]))

    return (xp.reshape(b_p, t_p, D_MODEL), xs.reshape(b_s, t_s, D_MODEL),
            *new_p_even, *new_p_odd, *new_s_even, *new_s_odd)
```

```python
import functools
import math

import numpy as np
import jax
import jax.numpy as jnp
from jax import lax
from jax.experimental import pallas as pl
from jax.experimental.pallas import tpu as pltpu

F32 = jnp.float32
BF16 = jnp.bfloat16

D_MODEL = 1024
HEAD_DIM = 64
CHUNK = 64
H_A = 4
H_B = 8
H_C = 16
A_W = 512
N_SEG = 6
BAND_PAST = 512
REL_CLIP = 256
D_FF = 4 * D_MODEL
NORM_EPS = 1e-6
NEG_INF = -1e30
QK_SCALE = HEAD_DIM ** -0.5

LANES = 128
ROW_TILE = 512
FF_CHUNK = 1024
POST_STREAMS = 2
VMEM_LIMIT = 56 * 1024 * 1024

_NT = (((1,), (1,)), ((), ()))


def _rms(x, g):
    ms = jnp.mean(x * x, axis=-1, keepdims=True)
    return x * lax.rsqrt(ms + NORM_EPS) * g


def _log_sigmoid(x):
    t = -x
    return -(jnp.maximum(t, 0.0) + jnp.log1p(jnp.exp(-jnp.abs(t))))


def _const_spec(shape):
    nd = len(shape)
    return pl.BlockSpec(shape, lambda *_: (0,) * nd, pipeline_mode=pl.Buffered(1))


def _params(n_axes):
    return pltpu.CompilerParams(dimension_semantics=("arbitrary",) * n_axes,
                                vmem_limit_bytes=VMEM_LIMIT)


def _proj_even_kernel(x_ref, g_ref, wt_ref, wft_ref, bft_ref,
                      qa_ref, ka_ref, kab_ref, va_ref, vab_ref,
                      qb_ref, kb_ref, kbb_ref, vb_ref, vbb_ref, lft_ref, *, feature_major):
    h = _rms(x_ref[...], g_ref[...]).astype(BF16)
    tm = h.shape[0]

    def seg(i):
        return lax.dot_general(h, wt_ref[i * A_W:(i + 1) * A_W, :], _NT, preferred_element_type=F32)

    def seg_t(i):
        return lax.dot_general(wt_ref[i * A_W:(i + 1) * A_W, :], h, _NT, preferred_element_type=F32)

    def store_token_head_rows(ref, z):
        for hd in range(H_A):
            ref[pl.ds(hd, tm, stride=H_A), :] = z[:, hd * LANES:(hd + 1) * LANES]

    qa_ref[...] = (seg(0) * QK_SCALE).astype(BF16)
    z = seg(1)
    store_token_head_rows(ka_ref, z)
    kab_ref[...] = z.astype(BF16)
    z = seg(2)
    store_token_head_rows(va_ref, z)
    if feature_major:
        vab_ref[0] = z.T.astype(BF16)
    else:
        vab_ref[...] = z.astype(BF16)
    qb_ref[...] = (seg(3) * QK_SCALE).astype(BF16)
    z = seg(4)
    kbb_ref[...] = z.astype(BF16)
    if feature_major:
        kb_ref[0] = z.T
        zt = seg_t(5)
        vb_ref[0] = zt
        vbb_ref[0] = zt.astype(BF16)
    else:
        kb_ref[...] = z
        z = seg(5)
        vb_ref[...] = z
        vbb_ref[...] = z.astype(BF16)
    fzt = lax.dot_general(wft_ref[...], h, _NT, preferred_element_type=F32)
    lft_ref[0] = _log_sigmoid(fzt[:H_B] + bft_ref[...])


def _proj_even(x, g, w_t, wft, bf_col, *, tiles_per_batch=None):
    m = x.shape[0]
    tm = min(ROW_TILE, m)
    n = m // tm
    row = lambda w: pl.BlockSpec((tm, w), lambda i: (i, 0))
    bfo = jax.ShapeDtypeStruct((m, A_W), BF16)
    tok_head = jax.ShapeDtypeStruct((m * H_A, LANES), F32)
    tok_head_spec = pl.BlockSpec((tm * H_A, LANES), lambda i: (i, 0))
    if tiles_per_batch:
        tpb = tiles_per_batch
        fm = lambda dt: jax.ShapeDtypeStruct((n // tpb, A_W, tpb * tm), dt)
        fm_spec = pl.BlockSpec((1, A_W, tm), lambda i: (i // tpb, 0, i % tpb))
        outs = [(bfo, row(A_W)), (tok_head, tok_head_spec), (bfo, row(A_W)), (tok_head, tok_head_spec),
                (fm(BF16), fm_spec), (bfo, row(A_W)), (fm(F32), fm_spec), (bfo, row(A_W)),
                (fm(F32), fm_spec), (fm(BF16), fm_spec)]
    else:
        f32o = jax.ShapeDtypeStruct((m, A_W), F32)
        outs = [(bfo, row(A_W)), (tok_head, tok_head_spec), (bfo, row(A_W)), (tok_head, tok_head_spec),
                (bfo, row(A_W)), (bfo, row(A_W)), (f32o, row(A_W)), (bfo, row(A_W)),
                (f32o, row(A_W)), (bfo, row(A_W))]
    outs.append((jax.ShapeDtypeStruct((n, H_B, tm), F32), pl.BlockSpec((1, H_B, tm), lambda i: (i, 0, 0))))
    return pl.pallas_call(
        functools.partial(_proj_even_kernel, feature_major=bool(tiles_per_batch)),
        grid=(n,),
        in_specs=[row(D_MODEL), _const_spec((1, D_MODEL)), _const_spec(w_t.shape),
                  _const_spec(wft.shape), _const_spec(bf_col.shape)],
        out_specs=[s for _, s in outs],
        out_shape=[o for o, _ in outs],
        compiler_params=_params(1),
        name="proj_even",
    )(x, g, w_t, wft, bf_col)


def _cumsum_kernel(x_ref, o_ref):
    x = x_ref[0]
    t = x.shape[1]
    col = lax.broadcasted_iota(jnp.int32, x.shape, 1)
    s = 1
    while s < t:
        x = x + jnp.where(col >= s, pltpu.roll(x, s, 1), 0.0)
        s *= 2
    o_ref[0] = x


def _cumsum_time(x):
    b, h, t = x.shape
    spec = pl.BlockSpec((1, h, t), lambda i: (i, 0, 0))
    return pl.pallas_call(
        _cumsum_kernel, grid=(b,), in_specs=[spec], out_specs=spec,
        out_shape=jax.ShapeDtypeStruct(x.shape, F32),
        compiler_params=_params(1), name="cumsum_time",
    )(x)


def _stack_pair(q):
    q32 = q.astype(F32)
    lo = lax.broadcasted_iota(jnp.int32, q32.shape, 1) < HEAD_DIM
    return jnp.concatenate([jnp.where(lo, q32, 0.0), jnp.where(lo, 0.0, q32)], axis=0).astype(BF16)


def _sample_attn_kernel(*refs, mode, t, past, lam_init):
    if mode == "diff":
        slopes_ref, lamv_ref, g_ref, q_ref, kc_ref, vc_ref, kn_ref, vn_ref, o_ref = refs
    elif mode == "fox":
        cum_ref, q_ref, kc_ref, vc_ref, kn_ref, vn_ref, o_ref = refs
    else:
        c_ref, q_ref, kc_ref, vc_ref, kn_ref, vn_ref, knt_ref, vnt_ref, o_ref, kco_ref, vco_ref, bias_sc = refs
        kco_ref[0] = jnp.concatenate([kc_ref[0, :, t:], knt_ref[0]], axis=1)
        vco_ref[0] = jnp.concatenate([vc_ref[0, :, t:], vnt_ref[0]], axis=1)

        @pl.when(pl.program_id(0) == 0)
        def _():
            for hd in range(bias_sc.shape[0]):
                c = jnp.broadcast_to(c_ref[hd // 2, hd % 2:hd % 2 + 1, :], (t, c_ref.shape[2]))
                bias_sc[hd] = pltpu.roll(c, 0, 1, stride=1, stride_axis=0)
    pairs = q_ref.shape[2] // LANES
    n_c = vc_ref.shape[1] // pairs if mode == "diff" else kc_ref.shape[2]
    row = lax.broadcasted_iota(jnp.int32, (t, n_c), 0)
    col = lax.broadcasted_iota(jnp.int32, (t, n_c), 1)
    row_n = lax.broadcasted_iota(jnp.int32, (t, t), 0)
    col_n = lax.broadcasted_iota(jnp.int32, (t, t), 1)
    lo = lax.broadcasted_iota(jnp.int32, (t, LANES), 1) < HEAD_DIM

    for p in range(pairs):
        lanes = slice(p * LANES, (p + 1) * LANES)
        q2 = _stack_pair(q_ref[0, :, lanes])
        kn = kn_ref[0, :, lanes]
        vn = vn_ref[0, :, lanes]
        if mode == "diff":
            kc = kc_ref[0, pl.ds(p, n_c, stride=pairs), :].astype(BF16)
            vc = vc_ref[0, pl.ds(p, n_c, stride=pairs), :].astype(BF16)
            s_c = lax.dot_general(q2, kc, _NT, preferred_element_type=F32)
        else:
            kc = kc_ref[0, lanes, :].astype(BF16)
            vc = vc_ref[0, lanes, :].astype(BF16)
            s_c = jnp.dot(q2, kc, preferred_element_type=F32)
        s_n = lax.dot_general(q2, kn, _NT, preferred_element_type=F32)

        outs = []
        for a in range(2):
            rows = slice(a * t, (a + 1) * t)
            if mode == "diff":
                slope = slopes_ref[p]
                t_c = s_c[rows] - slope * (past + row - col).astype(F32)
                t_n = s_n[rows] - slope * jnp.abs(row_n - col_n).astype(F32)
            elif mode == "fox":
                ck = cum_ref[0, 2 * p + a]
                t_c = s_c[rows] - ck[:, :n_c]
                t_n = jnp.where(col_n <= row_n, s_n[rows] - ck[:, n_c:n_c + t], NEG_INF)
            else:
                t_c = s_c[rows] + bias_sc[2 * p + a, :, :n_c]
                t_n = s_n[rows] + bias_sc[2 * p + a, :, n_c:n_c + t]
            m = jnp.maximum(jnp.max(t_c, axis=1, keepdims=True), jnp.max(t_n, axis=1, keepdims=True))
            p_c = jnp.exp(t_c - m)
            p_n = jnp.exp(t_n - m)
            l = jnp.sum(p_c, axis=1, keepdims=True) + jnp.sum(p_n, axis=1, keepdims=True)
            if mode == "diff":
                acc = jnp.dot(p_c.astype(BF16), vc, preferred_element_type=F32)
            else:
                acc = lax.dot_general(p_c.astype(BF16), vc, _NT, preferred_element_type=F32)
            acc = acc + jnp.dot(p_n.astype(BF16), vn, preferred_element_type=F32)
            outs.append(acc / l)
        if mode == "diff":
            lv = lamv_ref[...]
            lam = (jnp.exp(jnp.sum(lv[0:1] * lv[1:2], axis=1, keepdims=True))
                   - jnp.exp(jnp.sum(lv[2:3] * lv[3:4], axis=1, keepdims=True)) + lam_init)
            oa = outs[0] - lam * outs[1]
            o_ref[0, :, lanes] = (_rms(oa, g_ref[...]) * (1.0 - lam_init)).astype(BF16)
        else:
            o_ref[0, :, lanes] = jnp.where(lo, outs[0], outs[1]).astype(BF16)


def _sample_attn(mode, q, kc, vc, kn, vn, extras, *, past, lam_init=0.0, new_t=()):
    b, t, width = q.shape
    whole = lambda a: pl.BlockSpec((1,) + a.shape[1:], lambda i: (i,) + (0,) * (a.ndim - 1))
    const = lambda a: pl.BlockSpec(a.shape, lambda i: (0,) * a.ndim)
    if mode == "diff":
        slopes, lamv, g = extras
        extra_specs = [pl.BlockSpec(memory_space=pltpu.SMEM), const(lamv), const(g)]
    elif mode == "fox":
        extras = (extras[0][:, :, None, :],)
        extra_specs = [whole(extras[0])]
    else:
        extra_specs = [const(extras[0])]
    operands = [q, kc, vc, kn, vn]
    out_specs, out_shape = whole(q), jax.ShapeDtypeStruct(q.shape, BF16)
    scratch = []
    if mode == "band":
        scratch = [pltpu.VMEM((width // HEAD_DIM, t, extras[0].shape[2]), F32)]
        operands += list(new_t)
        out_specs = [out_specs, whole(kc), whole(vc)]
        out_shape = [out_shape, jax.ShapeDtypeStruct(kc.shape, F32), jax.ShapeDtypeStruct(vc.shape, F32)]
    return pl.pallas_call(
        functools.partial(_sample_attn_kernel, mode=mode, t=t, past=past, lam_init=lam_init),
        grid=(b,),
        in_specs=extra_specs + [whole(a) for a in operands],
        out_specs=out_specs,
        out_shape=out_shape,
        scratch_shapes=scratch,
        compiler_params=_params(1),
        name="sample_" + mode,
    )(*extras, *operands)


ONES_ROWS = 16


def _with_ones(vt):
    return jnp.concatenate([vt, jnp.ones((ONES_ROWS, vt.shape[1]), BF16)], axis=0)


AUG_TERMS = 3
FLASH_STREAMS = 4
BAND_STREAMS = 2


def _lane_terms(lane, first, terms):
    out = jnp.zeros(terms[0].shape, F32)
    for t, v in enumerate(terms):
        out = jnp.where(lane == first + t, v, out)
    return out


def _flash_t_kernel(*refs, mode, tq, lam_init):
    tk = tq
    n_extra = 3 if mode == "diff" else 1
    extra_refs, (q_ref, k_ref, vt_ref, o_ref, m_sc, acc_sc, q2_sc, mu_sc, kaug_sc, bdiag_sc) = (
        refs[:n_extra], refs[n_extra:n_extra + 10])
    s_bufs = refs[n_extra + 10:]
    if mode == "diff":
        slopes_ref, lamv_ref, g_ref = extra_refs
    else:
        (cum_ref,) = extra_refs
    pair = pl.program_id(0)
    qi = pl.program_id(2)
    t_all = kaug_sc.shape[1]
    streams = range(FLASH_STREAMS)

    @pl.when((pl.program_id(1) == 0) & (qi == 0))
    def _():
        row = lax.broadcasted_iota(jnp.int32, (tk, tq), 0)
        col = lax.broadcasted_iota(jnp.int32, (tk, tq), 1)
        lane_q = lax.broadcasted_iota(jnp.int32, (2 * tq, LANES), 1)
        first_q = jnp.where(lax.broadcasted_iota(jnp.int32, (2 * tq, LANES), 0) < tq, 0, AUG_TERMS)
        if mode == "diff":
            slope = slopes_ref[pair]
            visible = (row >> 6) <= (col >> 6)
            bdiag_sc[...] = jnp.where(visible, -2.0 * slope * jnp.maximum(row - col, 0).astype(F32), NEG_INF)
            for s in streams:
                q2_sc[s, :, LANES:] = jnp.where(lane_q < AUG_TERMS, 1.0, 0.0).astype(BF16)
            for c0 in range(0, t_all, tk):
                j = c0 + lax.broadcasted_iota(jnp.int32, (tk, LANES), 0)
                lane = lax.broadcasted_iota(jnp.int32, (tk, LANES), 1)
                terms = [((j >> 8) << 8).astype(F32), (((j >> 4) & 15) << 4).astype(F32), (j & 15).astype(F32)]
                kaug_sc[0, c0:c0 + tk, :] = (slope * _lane_terms(lane, 0, terms)).astype(BF16)
        else:
            bdiag_sc[...] = jnp.where(row <= col, 0.0, NEG_INF)
            mine = (lane_q >= first_q) & (lane_q < first_q + AUG_TERMS)
            for s in streams:
                q2_sc[s, :, LANES:] = jnp.where(mine, -1.0, 0.0).astype(BF16)

    if mode == "fox":
        @pl.when(qi == 0)
        def _():
            for s in streams:
                for c0 in range(0, t_all, tk):
                    lane = lax.broadcasted_iota(jnp.int32, (tk, LANES), 1)
                    feat = jnp.zeros((tk, LANES), F32)
                    for a in range(2):
                        r = cum_ref[s, 2 * pair + a, :, c0:c0 + tk]
                        ck = jnp.broadcast_to(r, (LANES, tk)).T
                        hi = ck.astype(BF16).astype(F32)
                        mid = (ck - hi).astype(BF16).astype(F32)
                        lo = ((ck - hi) - mid).astype(BF16).astype(F32)
                        feat = feat + _lane_terms(lane, AUG_TERMS * a, [hi, mid, lo])
                    kaug_sc[s, c0:c0 + tk, :] = feat.astype(BF16)

    for s in streams:
        q2_sc[s, :, :LANES] = _stack_pair(q_ref[s])
    m_sc[...] = jnp.full(m_sc.shape, NEG_INF, F32)
    acc_sc[...] = jnp.zeros(acc_sc.shape, F32)

    def stage_a(k0, slot, diag):
        for s in streams:
            kaug = kaug_sc[0 if mode == "diff" else s, pl.ds(k0, tk), :]
            kk = jnp.concatenate([k_ref[s, pl.ds(k0, tk), :], kaug], axis=1)
            st = lax.dot_general(kk, q2_sc[s], _NT, preferred_element_type=F32)
            for a in range(2):
                sa = st[:, a * tq:(a + 1) * tq]
                if diag:
                    sa = sa + bdiag_sc[...]
                s_bufs[2 * s + slot][:, a * tq:(a + 1) * tq] = sa
                mu_sc[s, slot, :, a * tq:(a + 1) * tq] = jnp.max(sa, axis=0, keepdims=True)

    def stage_b(k0, slot):
        for s in streams:
            vt = vt_ref[s, :, pl.ds(k0, tk)]
            for a in range(2):
                m_prev = m_sc[s, a]
                m_next = jnp.maximum(m_prev, mu_sc[s, slot, :, a * tq:(a + 1) * tq])
                p = jnp.exp(s_bufs[2 * s + slot][:, a * tq:(a + 1) * tq] - m_next)
                alpha = jnp.exp(m_prev - m_next)
                vta = _with_ones(vt if mode == "diff" else vt[a * HEAD_DIM:(a + 1) * HEAD_DIM])
                acc_sc[s, a] = alpha * acc_sc[s, a] + jnp.dot(vta, p.astype(BF16), preferred_element_type=F32)
                m_sc[s, a] = m_next

    blk = lambda j: pl.multiple_of(j * tk, tk)
    pl.when(qi == 0)(lambda: stage_a(0, 0, True))
    pl.when(qi > 0)(lambda: stage_a(0, 0, False))

    def two_blocks(i, carry):
        stage_a(blk(2 * i + 1), 1, False)
        stage_b(blk(2 * i), 0)
        stage_a(blk(2 * i + 2), 0, False)
        stage_b(blk(2 * i + 1), 1)
        return carry

    lax.fori_loop(0, (qi - 1) // 2, two_blocks, 0)

    @pl.when(qi == 0)
    def _():
        stage_b(0, 0)

    @pl.when(qi % 2 == 1)
    def _():
        stage_a(blk(qi), 1, True)
        stage_b(blk(qi - 1), 0)
        stage_b(blk(qi), 1)

    @pl.when((qi % 2 == 0) & (qi > 0))
    def _():
        stage_a(blk(qi - 1), 1, False)
        stage_b(blk(qi - 2), 0)
        stage_a(blk(qi), 0, True)
        stage_b(blk(qi - 1), 1)
        stage_b(blk(qi), 0)

    d = acc_sc.shape[2] - ONES_ROWS
    for s in streams:
        o0 = acc_sc[s, 0, :d, :] / acc_sc[s, 0, d:d + 1, :]
        o1 = acc_sc[s, 1, :d, :] / acc_sc[s, 1, d:d + 1, :]
        if mode == "diff":
            lv = lamv_ref[...]
            lam = (jnp.exp(jnp.sum(lv[0:1] * lv[1:2], axis=1, keepdims=True))
                   - jnp.exp(jnp.sum(lv[2:3] * lv[3:4], axis=1, keepdims=True)) + lam_init)
            oa = o0 - lam * o1
            ms = jnp.mean(oa * oa, axis=0, keepdims=True)
            y = (oa * lax.rsqrt(ms + NORM_EPS)).T * g_ref[...]
            o_ref[s] = (y * (1.0 - lam_init)).astype(BF16)
        else:
            o_ref[s] = jnp.concatenate([o0, o1], axis=0).T.astype(BF16)


def _flash_t(mode, q, k, vt, extras, *, tq, lam_init=0.0):
    b, t, width = q.shape
    pairs = width // LANES
    ns = FLASH_STREAMS
    assert b % ns == 0
    qspec = pl.BlockSpec((ns, tq, LANES), lambda p, bi, i: (bi, i, p))
    kspec = pl.BlockSpec((ns, t, LANES), lambda p, bi, i: (bi, 0, p))
    vspec = pl.BlockSpec((ns, LANES, t), lambda p, bi, i: (bi, p, 0))
    acc_rows = (LANES if mode == "diff" else HEAD_DIM) + ONES_ROWS
    scratch = [pltpu.VMEM((ns, 2, 1, tq), F32),
               pltpu.VMEM((ns, 2, acc_rows, tq), F32),
               pltpu.VMEM((ns, 2 * tq, 2 * LANES), BF16),
               pltpu.VMEM((ns, 2, 1, 2 * tq), F32),
               pltpu.VMEM((1 if mode == "diff" else ns, t, LANES), BF16),
               pltpu.VMEM((tq, tq), F32)]
    scratch += [pltpu.VMEM((tq, 2 * tq), F32)] * (2 * ns)
    if mode == "diff":
        slopes, lamv, g = extras
        extra_specs = [pl.BlockSpec(memory_space=pltpu.SMEM),
                       pl.BlockSpec(lamv.shape, lambda p, bi, i: (0, 0)),
                       pl.BlockSpec(g.shape, lambda p, bi, i: (0, 0))]
    else:
        extras = (extras[0][:, :, None, :],)
        extra_specs = [pl.BlockSpec((ns,) + extras[0].shape[1:], lambda p, bi, i: (bi, 0, 0, 0))]
    return pl.pallas_call(
        functools.partial(_flash_t_kernel, mode=mode, tq=tq, lam_init=lam_init),
        grid=(pairs, b // ns, t // tq),
        in_specs=extra_specs + [qspec, kspec, vspec],
        out_specs=qspec,
        out_shape=jax.ShapeDtypeStruct(q.shape, BF16),
        scratch_shapes=scratch,
        compiler_params=_params(3),
        name="flash_t_" + mode,
    )(*extras, q, k, vt)


def _band_t_kernel(c_ref, q_ref, k_ref, vt_ref, o_ref, mu_sc, bias_sc, *s_bufs, tq, t):
    window = BAND_PAST + tq
    n_q = t // tq
    streams = range(BAND_STREAMS)

    @pl.when(pl.program_id(1) == 0)
    def _():
        row = lax.broadcasted_iota(jnp.int32, (tq, window), 0)
        col = lax.broadcasted_iota(jnp.int32, (tq, window), 1)
        visible = ((col >> 6) >= (row >> 6)) & ((col >> 6) <= (row >> 6) + BAND_PAST // CHUNK)
        for a in range(2):
            c = jnp.broadcast_to(c_ref[0, a:a + 1, :], (tq, c_ref.shape[2]))
            bias = pltpu.roll(c, 0, 1, stride=1, stride_axis=0)[:, :window]
            bias_sc[a] = jnp.where(visible, bias, NEG_INF).T

    lane_q = lax.broadcasted_iota(jnp.int32, (2 * tq, LANES), 1)
    q_extra = jnp.where(lane_q == 0, 1.0, 0.0).astype(BF16)

    def stage_a(qb, slot):
        w0 = pl.multiple_of(qb * tq, tq)
        pad_row = (w0 + lax.broadcasted_iota(jnp.int32, (window, LANES), 0)) < BAND_PAST
        lane_k = lax.broadcasted_iota(jnp.int32, (window, LANES), 1)
        k_extra = jnp.where(pad_row & (lane_k == 0), NEG_INF, 0.0).astype(BF16)
        for s in streams:
            q2 = jnp.concatenate([_stack_pair(q_ref[s, pl.ds(w0, tq), :]), q_extra], axis=1)
            kk = jnp.concatenate([k_ref[s, pl.ds(w0, window), :], k_extra], axis=1)
            st = lax.dot_general(kk, q2, _NT, preferred_element_type=F32)
            for a in range(2):
                u = st[:, a * tq:(a + 1) * tq] + bias_sc[a]
                s_bufs[2 * s + slot][:, a * tq:(a + 1) * tq] = u
                mu_sc[s, slot, :, a * tq:(a + 1) * tq] = jnp.max(u, axis=0, keepdims=True)

    def stage_b(qb, slot):
        w0 = pl.multiple_of(qb * tq, tq)
        for s in streams:
            vt = vt_ref[s, :, pl.ds(w0, window)]
            outs = []
            for a in range(2):
                p = jnp.exp(s_bufs[2 * s + slot][:, a * tq:(a + 1) * tq]
                            - mu_sc[s, slot, :, a * tq:(a + 1) * tq])
                vta = _with_ones(vt[a * HEAD_DIM:(a + 1) * HEAD_DIM])
                r = jnp.dot(vta, p.astype(BF16), preferred_element_type=F32)
                outs.append(r[:HEAD_DIM] / r[HEAD_DIM:HEAD_DIM + 1])
            o_ref[s, pl.ds(w0, tq), :] = jnp.concatenate(outs, axis=0).T.astype(BF16)

    stage_a(0, 0)

    def two_blocks(i, carry):
        stage_a(2 * i + 1, 1)
        stage_b(2 * i, 0)
        stage_a(2 * i + 2, 0)
        stage_b(2 * i + 1, 1)
        return carry

    lax.fori_loop(0, n_q // 2 - 1, two_blocks, 0)
    stage_a(n_q - 1, 1)
    stage_b(n_q - 2, 0)
    stage_b(n_q - 1, 1)


def _band_t(q, k_pad, vt_pad, cvec, *, tq):
    b, t, width = q.shape
    pairs = width // LANES
    window = BAND_PAST + tq
    ns = BAND_STREAMS
    assert (t // tq) % 2 == 0 and t // tq >= 2 and b % ns == 0
    qspec = pl.BlockSpec((ns, t, LANES), lambda p, bi: (bi, 0, p))
    kspec = pl.BlockSpec((ns, k_pad.shape[1], LANES), lambda p, bi: (bi, 0, p))
    vspec = pl.BlockSpec((ns, LANES, vt_pad.shape[2]), lambda p, bi: (bi, p, 0))
    cspec = pl.BlockSpec((1, 2, cvec.shape[2]), lambda p, bi: (p, 0, 0))
    return pl.pallas_call(
        functools.partial(_band_t_kernel, tq=tq, t=t),
        grid=(pairs, b // ns),
        in_specs=[cspec, qspec, kspec, vspec],
        out_specs=qspec,
        out_shape=jax.ShapeDtypeStruct(q.shape, BF16),
        scratch_shapes=[pltpu.VMEM((ns, 2, 1, 2 * tq), F32),
                        pltpu.VMEM((2, window, tq), F32)]
                       + [pltpu.VMEM((window, 2 * tq), F32)] * (2 * ns),
        compiler_params=_params(2),
        name="band_t_attn",
    )(cvec, q, k_pad, vt_pad)


def _proj_odd_kernel(x_ref, g_ref, w_ref, q_ref, k_ref, v_ref, kt_ref, vt_ref, *,
                     tiles, pad_tiles, v_feature_major):
    r = pl.program_id(1)

    if pad_tiles:
        @pl.when(r < pad_tiles)
        def _():
            k_ref[...] = jnp.zeros(k_ref.shape, BF16)
            v_ref[...] = jnp.zeros(v_ref.shape, BF16)

    @pl.when(r >= pad_tiles)
    def _():
        h = _rms(x_ref[...], g_ref[...]).astype(BF16)

        def seg(i):
            return jnp.dot(h, w_ref[:, i * D_MODEL:(i + 1) * D_MODEL], preferred_element_type=F32)

        is_tail = r == pad_tiles + tiles - 1
        q_ref[...] = (seg(0) * QK_SCALE).astype(BF16)
        kz = seg(1)
        k_ref[...] = kz.astype(BF16)
        vz = seg(2)
        if v_feature_major:
            v_ref[0] = vz.T.astype(BF16)
        else:
            v_ref[...] = vz.astype(BF16)

        @pl.when(is_tail)
        def _():
            kt_ref[0] = kz.T
            vt_ref[0] = vz.T


def _proj_odd(x, g, w, *, tiles, pad_tiles, v_feature_major):
    m = x.shape[0]
    tm = min(ROW_TILE, m)
    nb = m // tm // tiles
    steps = pad_tiles + tiles
    src = lambda b, r: (b * tiles + jnp.maximum(r - pad_tiles, 0), 0)
    row = pl.BlockSpec((tm, D_MODEL), src)
    padded_row = pl.BlockSpec((tm, D_MODEL), lambda b, r: (b * steps + r, 0))
    tail = pl.BlockSpec((1, D_MODEL, tm), lambda b, r: (b, 0, 0))
    bfo = jax.ShapeDtypeStruct((m, D_MODEL), BF16)
    padded = jax.ShapeDtypeStruct((nb * steps * tm, D_MODEL), BF16)
    tailo = jax.ShapeDtypeStruct((nb, D_MODEL, tm), F32)
    if v_feature_major:
        vo = jax.ShapeDtypeStruct((nb, D_MODEL, steps * tm), BF16)
        vspec = pl.BlockSpec((1, D_MODEL, tm), lambda b, r: (b, 0, r))
    else:
        vo, vspec = padded, padded_row
    return pl.pallas_call(
        functools.partial(_proj_odd_kernel, tiles=tiles, pad_tiles=pad_tiles, v_feature_major=v_feature_major),
        grid=(nb, steps),
        in_specs=[row, _const_spec((1, D_MODEL)), _const_spec(w.shape)],
        out_specs=[row, padded_row, vspec, tail, tail],
        out_shape=[bfo, padded, vo, tailo, tailo],
        compiler_params=_params(2),
        name="proj_odd",
    )(x, g, w)


def _post_kernel(*refs, n_o):
    o_refs = refs[:n_o]
    x_ref, wo_ref, gpm_ref, gpre_ref, wup_ref, wdn_ref, gpf_ref, out_ref = refs[n_o:]
    sub = x_ref.shape[0] // POST_STREAMS
    rows = [slice(s * sub, (s + 1) * sub) for s in range(POST_STREAMS)]
    x1, h, acc = [], [], []
    for r in rows:
        o = o_refs[0][r] if n_o == 1 else jnp.concatenate([ref[r] for ref in o_refs], axis=1)
        mixed = jnp.dot(o, wo_ref[...], preferred_element_type=F32)
        x1.append(x_ref[r] + _rms(mixed, gpm_ref[...]))
        h.append(_rms(x1[-1], gpre_ref[...]).astype(BF16))
        acc.append(jnp.zeros(x1[-1].shape, F32))
    for c in range(D_FF // FF_CHUNK):
        for s in range(POST_STREAMS):
            u = jnp.dot(h[s], wup_ref[:, c * FF_CHUNK:(c + 1) * FF_CHUNK], preferred_element_type=F32)
            u = jnp.maximum(u, 0.0)
            acc[s] = acc[s] + jnp.dot((u * u).astype(BF16), wdn_ref[c * FF_CHUNK:(c + 1) * FF_CHUNK, :],
                                      preferred_element_type=F32)
    for s, r in enumerate(rows):
        out_ref[r] = x1[s] + _rms(acc[s], gpf_ref[...])


def _post(o_list, x, wo, gpm, gpre, wup, wdn, gpf):
    m = x.shape[0]
    tm = min(ROW_TILE * POST_STREAMS, m)
    row = lambda w: pl.BlockSpec((tm, w), lambda i: (i, 0))
    gspec = _const_spec((1, D_MODEL))
    return pl.pallas_call(
        functools.partial(_post_kernel, n_o=len(o_list)),
        grid=(m // tm,),
        in_specs=[row(o.shape[1]) for o in o_list]
                 + [row(D_MODEL), _const_spec(wo.shape), gspec, gspec,
                    _const_spec(wup.shape), _const_spec(wdn.shape), gspec],
        out_specs=row(D_MODEL),
        out_shape=jax.ShapeDtypeStruct((m, D_MODEL), F32),
        compiler_params=_params(1),
        name="post_mix_mlp",
    )(*o_list, x, wo, gpm, gpre, wup, wdn, gpf)


def _band_bias_vectors(table):
    n = 2 * BAND_PAST
    m = np.arange(n)
    d = np.where(m <= n - REL_CLIP, -m, n - m)
    idx = np.clip(d + BAND_PAST, -REL_CLIP, REL_CLIP) + REL_CLIP
    return jnp.take(table.astype(F32), jnp.asarray(idx, jnp.int32), axis=1).reshape(H_C // 2, 2, n)


def kernel(x_prompt, x_sample, cache_a_k, cache_a_v, cache_b_k, cache_b_v, cache_b_logf, cache_c_k, cache_c_v,
           w_in_even, b_forget, lam_q1, lam_k1, lam_q2, lam_k2, subln_g, w_out_even, w_in_odd, rel_bias,
           w_out_odd, g_pre_mix, g_post_mix, g_pre_ffn, g_post_ffn, w_ffn_up, w_ffn_down):
    b_p, t_p, _ = x_prompt.shape
    b_s, t_s, _ = x_sample.shape
    past = cache_b_logf.shape[2]
    m_p, m_s = b_p * t_p, b_s * t_s
    xp = x_prompt.reshape(m_p, D_MODEL)
    xs = x_sample.reshape(m_s, D_MODEL)
    gvec = lambda a, l: a[l].reshape(1, D_MODEL)

    lam_init = 0.8 - 0.6 * math.exp(-0.3 * 0)
    w_in_t = w_in_even[0].T
    w_t = w_in_t[:N_SEG * A_W].astype(BF16)
    wft = jnp.pad(w_in_t[N_SEG * A_W:], ((0, 16 - H_B), (0, 0))).astype(BF16)
    bf_col = b_forget[0].reshape(H_B, 1)
    slopes = 2.0 ** (-8.0 * jnp.arange(1, H_A + 1, dtype=F32) / H_A)
    lamv = jnp.stack([lam_q1[0], lam_k1[0], lam_q2[0], lam_k2[0]]).astype(F32)
    sub_g = subln_g[0].reshape(1, 2 * HEAD_DIM)
    wo0 = w_out_even[0].astype(BF16)
    wup0, wdn0 = w_ffn_up[0].astype(BF16), w_ffn_down[0].astype(BF16)

    def even_layer(x, bsz, t, caches):
        m = x.shape[0]
        tm = min(ROW_TILE, m)
        (qa, ka, kab, va, vab, qb, kb, kbb, vb, vbb, lft) = _proj_even(
            x, gvec(g_pre_mix, 0), w_t, wft, bf_col,
            tiles_per_batch=(t // tm) if caches is None else None)
        r3 = lambda a: a.reshape(bsz, t, A_W)
        lft = lft.transpose(1, 0, 2).reshape(H_B, bsz, t).transpose(1, 0, 2)
        tok_head = lambda a: a.reshape(1, bsz, t, H_A, 2 * HEAD_DIM)
        if caches is None:
            oa = _flash_t("diff", r3(qa), r3(kab), vab, (slopes, lamv, sub_g), tq=512, lam_init=lam_init)
            ob = _flash_t("fox", r3(qb), r3(kbb), vbb, (_cumsum_time(lft),), tq=512)
            time_minor = lambda a: a.reshape(bsz, H_B, HEAD_DIM, t).transpose(0, 3, 1, 2)[None]
            new = (tok_head(ka), tok_head(va), time_minor(kb), time_minor(vb), lft.transpose(0, 2, 1)[None])
        else:
            c_ka, c_va, c_kb, c_vb, c_lf = caches
            p_len = c_ka.shape[1]
            t_all = p_len + t
            t_pad = -(-t_all // LANES) * LANES
            assert p_len % CHUNK == 0 and t <= CHUNK
            lf_all = jnp.concatenate([c_lf.astype(F32).transpose(0, 2, 1), lft], axis=2)
            cum = _cumsum_time(jnp.pad(lf_all, ((0, 0), (0, 0), (0, t_pad - t_all))))
            tok_head_rows = lambda c: c.reshape(bsz, p_len * H_A, 2 * HEAD_DIM)
            feat_major = lambda c: c.transpose(0, 2, 3, 1).reshape(bsz, A_W, p_len)
            oa = _sample_attn("diff", r3(qa), tok_head_rows(c_ka), tok_head_rows(c_va), r3(kab), r3(vab),
                              (slopes, lamv, sub_g), past=p_len, lam_init=lam_init)
            ob = _sample_attn("fox", r3(qb), feat_major(c_kb), feat_major(c_vb), r3(kbb), r3(vbb),
                              (cum,), past=p_len)
            new = (tok_head(ka), tok_head(va), kb.reshape(1, bsz, t, H_B, HEAD_DIM),
                   vb.reshape(1, bsz, t, H_B, HEAD_DIM), lft.transpose(0, 2, 1)[None])
        x_out = _post([oa.reshape(m, A_W), ob.reshape(m, A_W)], x, wo0, gvec(g_post_mix, 0),
                      gvec(g_pre_ffn, 0), wup0, wdn0, gvec(g_post_ffn, 0))
        return x_out, new

    xp, new_p_even = even_layer(xp, b_p, t_p, None)
    xs, new_s_even = even_layer(xs, b_s, t_s, (cache_a_k[0], cache_a_v[0], cache_b_k[0], cache_b_v[0],
                                              cache_b_logf[0]))

    w_odd = w_in_odd[0].astype(BF16)
    cvec = _band_bias_vectors(rel_bias[0])
    to_cache = lambda a: a.reshape(a.shape[0], H_C, HEAD_DIM, a.shape[2]).transpose(0, 3, 1, 2)[None]
    from_cache = lambda c: c.transpose(0, 2, 3, 1).reshape(c.shape[0], D_MODEL, c.shape[1])
    wo1 = w_out_odd[0].astype(BF16)
    wup1, wdn1 = w_ffn_up[1].astype(BF16), w_ffn_down[1].astype(BF16)

    def odd_layer(x, bsz, t, caches):
        m = x.shape[0]
        tm = min(ROW_TILE, m)
        r3 = lambda a: a.reshape(bsz, t, D_MODEL)
        if caches is None:
            assert BAND_PAST % tm == 0 and tm == min(BAND_PAST, t)
            q, kp, vtp, k_tail, v_tail = _proj_odd(x, gvec(g_pre_mix, 1), w_odd, tiles=t // tm,
                                                   pad_tiles=BAND_PAST // tm, v_feature_major=True)
            o = _band_t(r3(q), kp.reshape(bsz, BAND_PAST + t, D_MODEL), vtp, cvec, tq=256)
            new = (to_cache(k_tail), to_cache(v_tail))
        else:
            c_k, c_v = caches
            assert c_k.shape[1] == BAND_PAST and past % CHUNK == 0 and past >= BAND_PAST and t <= CHUNK
            q, k, v, k_tail, v_tail = _proj_odd(x, gvec(g_pre_mix, 1), w_odd, tiles=m // tm,
                                                pad_tiles=0, v_feature_major=False)
            per_stream = lambda tail: tail.reshape(D_MODEL, bsz, t).transpose(1, 0, 2)
            o, k_buf, v_buf = _sample_attn("band", r3(q), from_cache(c_k), from_cache(c_v), r3(k), r3(v),
                                           (cvec,), past=past, new_t=(per_stream(k_tail), per_stream(v_tail)))
            new = (to_cache(k_buf), to_cache(v_buf))
        x_out = _post([o.reshape(m, D_MODEL)], x, wo1, gvec(g_post_mix, 1), gvec(g_pre_ffn, 1),
                      wup1, wdn1, gvec(g_post_ffn, 1))
        return x_out, new

    xp, new_p_odd = odd_layer(xp, b_p, t_p, None)
    xs, new_s_odd = odd_layer(xs, b_s, t_s, (cache_c_k[0], cache_c_v[0]))

    return (xp.reshape(b_p, t_p, D_MODEL), xs.reshape(b_s, t_s, D_MODEL),
            *new_p_even, *new_p_odd, *new_s_even, *new_s_odd)
```

```python
import functools
import math

import numpy as np
import jax
import jax.numpy as jnp
from jax import lax
from jax.experimental import pallas as pl
from jax.experimental.pallas import tpu as pltpu

F32 = jnp.float32
BF16 = jnp.bfloat16

D_MODEL = 1024
HEAD_DIM = 64
CHUNK = 64
H_A = 4
H_B = 8
H_C = 16
A_W = 512
N_SEG = 6
BAND_PAST = 512
REL_CLIP = 256
D_FF = 4 * D_MODEL
NORM_EPS = 1e-6
NEG_INF = -1e30
QK_SCALE = HEAD_DIM ** -0.5

LANES = 128
ROW_TILE = 512
FF_CHUNK = 1024
POST_STREAMS = 2
VMEM_LIMIT = 56 * 1024 * 1024

_NT = (((1,), (1,)), ((), ()))


def _rms(x, g):
    ms = jnp.mean(x * x, axis=-1, keepdims=True)
    return x * lax.rsqrt(ms + NORM_EPS) * g


def _log_sigmoid(x):
    t = -x
    return -(jnp.maximum(t, 0.0) + jnp.log1p(jnp.exp(-jnp.abs(t))))


def _const_spec(shape):
    nd = len(shape)
    return pl.BlockSpec(shape, lambda *_: (0,) * nd, pipeline_mode=pl.Buffered(1))


def _params(n_axes):
    return pltpu.CompilerParams(dimension_semantics=("arbitrary",) * n_axes,
                                vmem_limit_bytes=VMEM_LIMIT)


def _proj_even_kernel(x_ref, g_ref, wt_ref, wft_ref, bft_ref,
                      qa_ref, ka_ref, kab_ref, va_ref, vab_ref,
                      qb_ref, kb_ref, kbb_ref, vb_ref, vbb_ref, lft_ref, *, feature_major):
    h = _rms(x_ref[...], g_ref[...]).astype(BF16)
    tm = h.shape[0]

    def seg(i):
        return lax.dot_general(h, wt_ref[i * A_W:(i + 1) * A_W, :], _NT, preferred_element_type=F32)

    def seg_t(i):
        return lax.dot_general(wt_ref[i * A_W:(i + 1) * A_W, :], h, _NT, preferred_element_type=F32)

    def store_token_head_rows(ref, z):
        for hd in range(H_A):
            ref[pl.ds(hd, tm, stride=H_A), :] = z[:, hd * LANES:(hd + 1) * LANES]

    qa_ref[...] = (seg(0) * QK_SCALE).astype(BF16)
    z = seg(1)
    store_token_head_rows(ka_ref, z)
    kab_ref[...] = z.astype(BF16)
    z = seg(2)
    store_token_head_rows(va_ref, z)
    if feature_major:
        vab_ref[0] = z.T.astype(BF16)
    else:
        vab_ref[...] = z.astype(BF16)
    qb_ref[...] = (seg(3) * QK_SCALE).astype(BF16)
    z = seg(4)
    kbb_ref[...] = z.astype(BF16)
    if feature_major:
        kb_ref[0] = z.T
        zt = seg_t(5)
        vb_ref[0] = zt
        vbb_ref[0] = zt.astype(BF16)
    else:
        kb_ref[...] = z
        z = seg(5)
        vb_ref[...] = z
        vbb_ref[...] = z.astype(BF16)
    fzt = lax.dot_general(wft_ref[...], h, _NT, preferred_element_type=F32)
    lft_ref[0] = _log_sigmoid(fzt[:H_B] + bft_ref[...])


def _proj_even(x, g, w_t, wft, bf_col, *, tiles_per_batch=None):
    m = x.shape[0]
    tm = min(ROW_TILE, m)
    n = m // tm
    row = lambda w: pl.BlockSpec((tm, w), lambda i: (i, 0))
    bfo = jax.ShapeDtypeStruct((m, A_W), BF16)
    tok_head = jax.ShapeDtypeStruct((m * H_A, LANES), F32)
    tok_head_spec = pl.BlockSpec((tm * H_A, LANES), lambda i: (i, 0))
    if tiles_per_batch:
        tpb = tiles_per_batch
        fm = lambda dt: jax.ShapeDtypeStruct((n // tpb, A_W, tpb * tm), dt)
        fm_spec = pl.BlockSpec((1, A_W, tm), lambda i: (i // tpb, 0, i % tpb))
        outs = [(bfo, row(A_W)), (tok_head, tok_head_spec), (bfo, row(A_W)), (tok_head, tok_head_spec),
                (fm(BF16), fm_spec), (bfo, row(A_W)), (fm(F32), fm_spec), (bfo, row(A_W)),
                (fm(F32), fm_spec), (fm(BF16), fm_spec)]
    else:
        f32o = jax.ShapeDtypeStruct((m, A_W), F32)
        outs = [(bfo, row(A_W)), (tok_head, tok_head_spec), (bfo, row(A_W)), (tok_head, tok_head_spec),
                (bfo, row(A_W)), (bfo, row(A_W)), (f32o, row(A_W)), (bfo, row(A_W)),
                (f32o, row(A_W)), (bfo, row(A_W))]
    outs.append((jax.ShapeDtypeStruct((n, H_B, tm), F32), pl.BlockSpec((1, H_B, tm), lambda i: (i, 0, 0))))
    return pl.pallas_call(
        functools.partial(_proj_even_kernel, feature_major=bool(tiles_per_batch)),
        grid=(n,),
        in_specs=[row(D_MODEL), _const_spec((1, D_MODEL)), _const_spec(w_t.shape),
                  _const_spec(wft.shape), _const_spec(bf_col.shape)],
        out_specs=[s for _, s in outs],
        out_shape=[o for o, _ in outs],
        compiler_params=_params(1),
        name="proj_even",
    )(x, g, w_t, wft, bf_col)


def _cumsum_kernel(x_ref, o_ref):
    x = x_ref[0]
    t = x.shape[1]
    col = lax.broadcasted_iota(jnp.int32, x.shape, 1)
    s = 1
    while s < t:
        x = x + jnp.where(col >= s, pltpu.roll(x, s, 1), 0.0)
        s *= 2
    o_ref[0] = x


def _cumsum_time(x):
    b, h, t = x.shape
    spec = pl.BlockSpec((1, h, t), lambda i: (i, 0, 0))
    return pl.pallas_call(
        _cumsum_kernel, grid=(b,), in_specs=[spec], out_specs=spec,
        out_shape=jax.ShapeDtypeStruct(x.shape, F32),
        compiler_params=_params(1), name="cumsum_time",
    )(x)


def _stack_pair(q):
    q32 = q.astype(F32)
    lo = lax.broadcasted_iota(jnp.int32, q32.shape, 1) < HEAD_DIM
    return jnp.concatenate([jnp.where(lo, q32, 0.0), jnp.where(lo, 0.0, q32)], axis=0).astype(BF16)


def _sample_attn_kernel(*refs, mode, t, past, lam_init):
    if mode == "diff":
        slopes_ref, lamv_ref, g_ref, q_ref, kc_ref, vc_ref, kn_ref, vn_ref, o_ref = refs
    elif mode == "fox":
        cum_ref, q_ref, kc_ref, vc_ref, kn_ref, vn_ref, o_ref = refs
    else:
        c_ref, q_ref, kc_ref, vc_ref, kn_ref, vn_ref, knt_ref, vnt_ref, o_ref, kco_ref, vco_ref, bias_sc = refs
        kco_ref[0] = jnp.concatenate([kc_ref[0, :, t:], knt_ref[0]], axis=1)
        vco_ref[0] = jnp.concatenate([vc_ref[0, :, t:], vnt_ref[0]], axis=1)

        @pl.when(pl.program_id(0) == 0)
        def _():
            for hd in range(bias_sc.shape[0]):
                c = jnp.broadcast_to(c_ref[hd // 2, hd % 2:hd % 2 + 1, :], (t, c_ref.shape[2]))
                bias_sc[hd] = pltpu.roll(c, 0, 1, stride=1, stride_axis=0)
    pairs = q_ref.shape[2] // LANES
    n_c = vc_ref.shape[1] // pairs if mode == "diff" else kc_ref.shape[2]
    row = lax.broadcasted_iota(jnp.int32, (t, n_c), 0)
    col = lax.broadcasted_iota(jnp.int32, (t, n_c), 1)
    row_n = lax.broadcasted_iota(jnp.int32, (t, t), 0)
    col_n = lax.broadcasted_iota(jnp.int32, (t, t), 1)
    lo = lax.broadcasted_iota(jnp.int32, (t, LANES), 1) < HEAD_DIM

    for p in range(pairs):
        lanes = slice(p * LANES, (p + 1) * LANES)
        q2 = _stack_pair(q_ref[0, :, lanes])
        kn = kn_ref[0, :, lanes]
        vn = vn_ref[0, :, lanes]
        if mode == "diff":
            kc = kc_ref[0, pl.ds(p, n_c, stride=pairs), :].astype(BF16)
            vc = vc_ref[0, pl.ds(p, n_c, stride=pairs), :].astype(BF16)
            s_c = lax.dot_general(q2, kc, _NT, preferred_element_type=F32)
        else:
            kc = kc_ref[0, lanes, :].astype(BF16)
            vc = vc_ref[0, lanes, :].astype(BF16)
            s_c = jnp.dot(q2, kc, preferred_element_type=F32)
        s_n = lax.dot_general(q2, kn, _NT, preferred_element_type=F32)

        outs = []
        for a in range(2):
            rows = slice(a * t, (a + 1) * t)
            if mode == "diff":
                slope = slopes_ref[p]
                t_c = s_c[rows] - slope * (past + row - col).astype(F32)
                t_n = s_n[rows] - slope * jnp.abs(row_n - col_n).astype(F32)
            elif mode == "fox":
                ck = cum_ref[0, 2 * p + a]
                t_c = s_c[rows] - ck[:, :n_c]
                t_n = jnp.where(col_n <= row_n, s_n[rows] - ck[:, n_c:n_c + t], NEG_INF)
            else:
                t_c = s_c[rows] + bias_sc[2 * p + a, :, :n_c]
                t_n = s_n[rows] + bias_sc[2 * p + a, :, n_c:n_c + t]
            m = jnp.maximum(jnp.max(t_c, axis=1, keepdims=True), jnp.max(t_n, axis=1, keepdims=True))
            p_c = jnp.exp(t_c - m)
            p_n = jnp.exp(t_n - m)
            l = jnp.sum(p_c, axis=1, keepdims=True) + jnp.sum(p_n, axis=1, keepdims=True)
            if mode == "diff":
                acc = jnp.dot(p_c.astype(BF16), vc, preferred_element_type=F32)
            else:
                acc = lax.dot_general(p_c.astype(BF16), vc, _NT, preferred_element_type=F32)
            acc = acc + jnp.dot(p_n.astype(BF16), vn, preferred_element_type=F32)
            outs.append(acc / l)
        if mode == "diff":
            lv = lamv_ref[...]
            lam = (jnp.exp(jnp.sum(lv[0:1] * lv[1:2], axis=1, keepdims=True))
                   - jnp.exp(jnp.sum(lv[2:3] * lv[3:4], axis=1, keepdims=True)) + lam_init)
            oa = outs[0] - lam * outs[1]
            o_ref[0, :, lanes] = (_rms(oa, g_ref[...]) * (1.0 - lam_init)).astype(BF16)
        else:
            o_ref[0, :, lanes] = jnp.where(lo, outs[0], outs[1]).astype(BF16)


def _sample_attn(mode, q, kc, vc, kn, vn, extras, *, past, lam_init=0.0, new_t=()):
    b, t, width = q.shape
    whole = lambda a: pl.BlockSpec((1,) + a.shape[1:], lambda i: (i,) + (0,) * (a.ndim - 1))
    const = lambda a: pl.BlockSpec(a.shape, lambda i: (0,) * a.ndim)
    if mode == "diff":
        slopes, lamv, g = extras
        extra_specs = [pl.BlockSpec(memory_space=pltpu.SMEM), const(lamv), const(g)]
    elif mode == "fox":
        extras = (extras[0][:, :, None, :],)
        extra_specs = [whole(extras[0])]
    else:
        extra_specs = [const(extras[0])]
    operands = [q, kc, vc, kn, vn]
    out_specs, out_shape = whole(q), jax.ShapeDtypeStruct(q.shape, BF16)
    scratch = []
    if mode == "band":
        scratch = [pltpu.VMEM((width // HEAD_DIM, t, extras[0].shape[2]), F32)]
        operands += list(new_t)
        out_specs = [out_specs, whole(kc), whole(vc)]
        out_shape = [out_shape, jax.ShapeDtypeStruct(kc.shape, F32), jax.ShapeDtypeStruct(vc.shape, F32)]
    return pl.pallas_call(
        functools.partial(_sample_attn_kernel, mode=mode, t=t, past=past, lam_init=lam_init),
        grid=(b,),
        in_specs=extra_specs + [whole(a) for a in operands],
        out_specs=out_specs,
        out_shape=out_shape,
        scratch_shapes=scratch,
        compiler_params=_params(1),
        name="sample_" + mode,
    )(*extras, *operands)


ONES_ROWS = 16


def _with_ones(vt):
    return jnp.concatenate([vt, jnp.ones((ONES_ROWS, vt.shape[1]), BF16)], axis=0)


AUG_TERMS = 3
FLASH_STREAMS = 4
BAND_STREAMS = 4


def _lane_terms(lane, first, terms):
    out = jnp.zeros(terms[0].shape, F32)
    for t, v in enumerate(terms):
        out = jnp.where(lane == first + t, v, out)
    return out


def _flash_t_kernel(*refs, mode, tq, lam_init):
    tk = tq
    n_extra = 3 if mode == "diff" else 1
    extra_refs, (q_ref, k_ref, vt_ref, o_ref, m_sc, acc_sc, q2_sc, mu_sc, kaug_sc, bdiag_sc) = (
        refs[:n_extra], refs[n_extra:n_extra + 10])
    s_bufs = refs[n_extra + 10:]
    if mode == "diff":
        slopes_ref, lamv_ref, g_ref = extra_refs
    else:
        (cum_ref,) = extra_refs
    pair = pl.program_id(0)
    qi = pl.program_id(2)
    t_all = kaug_sc.shape[1]
    streams = range(FLASH_STREAMS)

    @pl.when((pl.program_id(1) == 0) & (qi == 0))
    def _():
        row = lax.broadcasted_iota(jnp.int32, (tk, tq), 0)
        col = lax.broadcasted_iota(jnp.int32, (tk, tq), 1)
        lane_q = lax.broadcasted_iota(jnp.int32, (2 * tq, LANES), 1)
        first_q = jnp.where(lax.broadcasted_iota(jnp.int32, (2 * tq, LANES), 0) < tq, 0, AUG_TERMS)
        if mode == "diff":
            slope = slopes_ref[pair]
            visible = (row >> 6) <= (col >> 6)
            bdiag_sc[...] = jnp.where(visible, -2.0 * slope * jnp.maximum(row - col, 0).astype(F32), NEG_INF)
            for s in streams:
                q2_sc[s, :, LANES:] = jnp.where(lane_q < AUG_TERMS, 1.0, 0.0).astype(BF16)
            for c0 in range(0, t_all, tk):
                j = c0 + lax.broadcasted_iota(jnp.int32, (tk, LANES), 0)
                lane = lax.broadcasted_iota(jnp.int32, (tk, LANES), 1)
                terms = [((j >> 8) << 8).astype(F32), (((j >> 4) & 15) << 4).astype(F32), (j & 15).astype(F32)]
                kaug_sc[0, c0:c0 + tk, :] = (slope * _lane_terms(lane, 0, terms)).astype(BF16)
        else:
            bdiag_sc[...] = jnp.where(row <= col, 0.0, NEG_INF)
            mine = (lane_q >= first_q) & (lane_q < first_q + AUG_TERMS)
            for s in streams:
                q2_sc[s, :, LANES:] = jnp.where(mine, -1.0, 0.0).astype(BF16)

    if mode == "fox":
        @pl.when(qi == 0)
        def _():
            for s in streams:
                for c0 in range(0, t_all, tk):
                    lane = lax.broadcasted_iota(jnp.int32, (tk, LANES), 1)
                    feat = jnp.zeros((tk, LANES), F32)
                    for a in range(2):
                        r = cum_ref[s, 2 * pair + a, :, c0:c0 + tk]
                        ck = jnp.broadcast_to(r, (LANES, tk)).T
                        hi = ck.astype(BF16).astype(F32)
                        mid = (ck - hi).astype(BF16).astype(F32)
                        lo = ((ck - hi) - mid).astype(BF16).astype(F32)
                        feat = feat + _lane_terms(lane, AUG_TERMS * a, [hi, mid, lo])
                    kaug_sc[s, c0:c0 + tk, :] = feat.astype(BF16)

    for s in streams:
        q2_sc[s, :, :LANES] = _stack_pair(q_ref[s])
    m_sc[...] = jnp.full(m_sc.shape, NEG_INF, F32)
    acc_sc[...] = jnp.zeros(acc_sc.shape, F32)

    def stage_a(k0, slot, diag):
        for s in streams:
            kaug = kaug_sc[0 if mode == "diff" else s, pl.ds(k0, tk), :]
            kk = jnp.concatenate([k_ref[s, pl.ds(k0, tk), :], kaug], axis=1)
            st = lax.dot_general(kk, q2_sc[s], _NT, preferred_element_type=F32)
            for a in range(2):
                sa = st[:, a * tq:(a + 1) * tq]
                if diag:
                    sa = sa + bdiag_sc[...]
                s_bufs[2 * s + slot][:, a * tq:(a + 1) * tq] = sa
                mu_sc[s, slot, :, a * tq:(a + 1) * tq] = jnp.max(sa, axis=0, keepdims=True)

    def stage_b(k0, slot):
        for s in streams:
            vt = vt_ref[s, :, pl.ds(k0, tk)]
            for a in range(2):
                m_prev = m_sc[s, a]
                m_next = jnp.maximum(m_prev, mu_sc[s, slot, :, a * tq:(a + 1) * tq])
                p = jnp.exp(s_bufs[2 * s + slot][:, a * tq:(a + 1) * tq] - m_next)
                alpha = jnp.exp(m_prev - m_next)
                vta = _with_ones(vt if mode == "diff" else vt[a * HEAD_DIM:(a + 1) * HEAD_DIM])
                acc_sc[s, a] = alpha * acc_sc[s, a] + jnp.dot(vta, p.astype(BF16), preferred_element_type=F32)
                m_sc[s, a] = m_next

    blk = lambda j: pl.multiple_of(j * tk, tk)
    pl.when(qi == 0)(lambda: stage_a(0, 0, True))
    pl.when(qi > 0)(lambda: stage_a(0, 0, False))

    def two_blocks(i, carry):
        stage_a(blk(2 * i + 1), 1, False)
        stage_b(blk(2 * i), 0)
        stage_a(blk(2 * i + 2), 0, False)
        stage_b(blk(2 * i + 1), 1)
        return carry

    lax.fori_loop(0, (qi - 1) // 2, two_blocks, 0)

    @pl.when(qi == 0)
    def _():
        stage_b(0, 0)

    @pl.when(qi % 2 == 1)
    def _():
        stage_a(blk(qi), 1, True)
        stage_b(blk(qi - 1), 0)
        stage_b(blk(qi), 1)

    @pl.when((qi % 2 == 0) & (qi > 0))
    def _():
        stage_a(blk(qi - 1), 1, False)
        stage_b(blk(qi - 2), 0)
        stage_a(blk(qi), 0, True)
        stage_b(blk(qi - 1), 1)
        stage_b(blk(qi), 0)

    d = acc_sc.shape[2] - ONES_ROWS
    for s in streams:
        o0 = acc_sc[s, 0, :d, :] / acc_sc[s, 0, d:d + 1, :]
        o1 = acc_sc[s, 1, :d, :] / acc_sc[s, 1, d:d + 1, :]
        if mode == "diff":
            lv = lamv_ref[...]
            lam = (jnp.exp(jnp.sum(lv[0:1] * lv[1:2], axis=1, keepdims=True))
                   - jnp.exp(jnp.sum(lv[2:3] * lv[3:4], axis=1, keepdims=True)) + lam_init)
            oa = o0 - lam * o1
            ms = jnp.mean(oa * oa, axis=0, keepdims=True)
            y = (oa * lax.rsqrt(ms + NORM_EPS)).T * g_ref[...]
            o_ref[s] = (y * (1.0 - lam_init)).astype(BF16)
        else:
            o_ref[s] = jnp.concatenate([o0, o1], axis=0).T.astype(BF16)


def _flash_t(mode, q, k, vt, extras, *, tq, lam_init=0.0):
    b, t, width = q.shape
    pairs = width // LANES
    ns = FLASH_STREAMS
    assert b % ns == 0
    qspec = pl.BlockSpec((ns, tq, LANES), lambda p, bi, i: (bi, i, p))
    kspec = pl.BlockSpec((ns, t, LANES), lambda p, bi, i: (bi, 0, p))
    vspec = pl.BlockSpec((ns, LANES, t), lambda p, bi, i: (bi, p, 0))
    acc_rows = (LANES if mode == "diff" else HEAD_DIM) + ONES_ROWS
    scratch = [pltpu.VMEM((ns, 2, 1, tq), F32),
               pltpu.VMEM((ns, 2, acc_rows, tq), F32),
               pltpu.VMEM((ns, 2 * tq, 2 * LANES), BF16),
               pltpu.VMEM((ns, 2, 1, 2 * tq), F32),
               pltpu.VMEM((1 if mode == "diff" else ns, t, LANES), BF16),
               pltpu.VMEM((tq, tq), F32)]
    scratch += [pltpu.VMEM((tq, 2 * tq), F32)] * (2 * ns)
    if mode == "diff":
        slopes, lamv, g = extras
        extra_specs = [pl.BlockSpec(memory_space=pltpu.SMEM),
                       pl.BlockSpec(lamv.shape, lambda p, bi, i: (0, 0)),
                       pl.BlockSpec(g.shape, lambda p, bi, i: (0, 0))]
    else:
        extras = (extras[0][:, :, None, :],)
        extra_specs = [pl.BlockSpec((ns,) + extras[0].shape[1:], lambda p, bi, i: (bi, 0, 0, 0))]
    return pl.pallas_call(
        functools.partial(_flash_t_kernel, mode=mode, tq=tq, lam_init=lam_init),
        grid=(pairs, b // ns, t // tq),
        in_specs=extra_specs + [qspec, kspec, vspec],
        out_specs=qspec,
        out_shape=jax.ShapeDtypeStruct(q.shape, BF16),
        scratch_shapes=scratch,
        compiler_params=_params(3),
        name="flash_t_" + mode,
    )(*extras, q, k, vt)


def _band_t_kernel(c_ref, q_ref, k_ref, vt_ref, o_ref, mu_sc, bias_sc, *s_bufs, tq, t):
    window = BAND_PAST + tq
    n_q = t // tq
    streams = range(BAND_STREAMS)

    @pl.when(pl.program_id(1) == 0)
    def _():
        row = lax.broadcasted_iota(jnp.int32, (tq, window), 0)
        col = lax.broadcasted_iota(jnp.int32, (tq, window), 1)
        visible = ((col >> 6) >= (row >> 6)) & ((col >> 6) <= (row >> 6) + BAND_PAST // CHUNK)
        for a in range(2):
            c = jnp.broadcast_to(c_ref[0, a:a + 1, :], (tq, c_ref.shape[2]))
            bias = pltpu.roll(c, 0, 1, stride=1, stride_axis=0)[:, :window]
            bias_sc[a] = jnp.where(visible, bias, NEG_INF).T

    lane_q = lax.broadcasted_iota(jnp.int32, (2 * tq, LANES), 1)
    q_extra = jnp.where(lane_q == 0, 1.0, 0.0).astype(BF16)

    def stage_a(qb, slot):
        w0 = pl.multiple_of(qb * tq, tq)
        pad_row = (w0 + lax.broadcasted_iota(jnp.int32, (window, LANES), 0)) < BAND_PAST
        lane_k = lax.broadcasted_iota(jnp.int32, (window, LANES), 1)
        k_extra = jnp.where(pad_row & (lane_k == 0), NEG_INF, 0.0).astype(BF16)
        for s in streams:
            q2 = jnp.concatenate([_stack_pair(q_ref[s, pl.ds(w0, tq), :]), q_extra], axis=1)
            kk = jnp.concatenate([k_ref[s, pl.ds(w0, window), :], k_extra], axis=1)
            st = lax.dot_general(kk, q2, _NT, preferred_element_type=F32)
            for a in range(2):
                u = st[:, a * tq:(a + 1) * tq] + bias_sc[a]
                s_bufs[2 * s + slot][:, a * tq:(a + 1) * tq] = u
                mu_sc[s, slot, :, a * tq:(a + 1) * tq] = jnp.max(u, axis=0, keepdims=True)

    def stage_b(qb, slot):
        w0 = pl.multiple_of(qb * tq, tq)
        for s in streams:
            vt = vt_ref[s, :, pl.ds(w0, window)]
            outs = []
            for a in range(2):
                p = jnp.exp(s_bufs[2 * s + slot][:, a * tq:(a + 1) * tq]
                            - mu_sc[s, slot, :, a * tq:(a + 1) * tq])
                vta = _with_ones(vt[a * HEAD_DIM:(a + 1) * HEAD_DIM])
                r = jnp.dot(vta, p.astype(BF16), preferred_element_type=F32)
                outs.append(r[:HEAD_DIM] / r[HEAD_DIM:HEAD_DIM + 1])
            o_ref[s, pl.ds(w0, tq), :] = jnp.concatenate(outs, axis=0).T.astype(BF16)

    stage_a(0, 0)

    def two_blocks(i, carry):
        stage_a(2 * i + 1, 1)
        stage_b(2 * i, 0)
        stage_a(2 * i + 2, 0)
        stage_b(2 * i + 1, 1)
        return carry

    lax.fori_loop(0, n_q // 2 - 1, two_blocks, 0)
    stage_a(n_q - 1, 1)
    stage_b(n_q - 2, 0)
    stage_b(n_q - 1, 1)


def _band_t(q, k_pad, vt_pad, cvec, *, tq):
    b, t, width = q.shape
    pairs = width // LANES
    window = BAND_PAST + tq
    ns = BAND_STREAMS
    assert (t // tq) % 2 == 0 and t // tq >= 2 and b % ns == 0
    qspec = pl.BlockSpec((ns, t, LANES), lambda p, bi: (bi, 0, p))
    kspec = pl.BlockSpec((ns, k_pad.shape[1], LANES), lambda p, bi: (bi, 0, p))
    vspec = pl.BlockSpec((ns, LANES, vt_pad.shape[2]), lambda p, bi: (bi, p, 0))
    cspec = pl.BlockSpec((1, 2, cvec.shape[2]), lambda p, bi: (p, 0, 0))
    return pl.pallas_call(
        functools.partial(_band_t_kernel, tq=tq, t=t),
        grid=(pairs, b // ns),
        in_specs=[cspec, qspec, kspec, vspec],
        out_specs=qspec,
        out_shape=jax.ShapeDtypeStruct(q.shape, BF16),
        scratch_shapes=[pltpu.VMEM((ns, 2, 1, 2 * tq), F32),
                        pltpu.VMEM((2, window, tq), F32)]
                       + [pltpu.VMEM((window, 2 * tq), F32)] * (2 * ns),
        compiler_params=_params(2),
        name="band_t_attn",
    )(cvec, q, k_pad, vt_pad)


def _proj_odd_kernel(x_ref, g_ref, w_ref, q_ref, k_ref, v_ref, kt_ref, vt_ref, *,
                     tiles, pad_tiles, v_feature_major):
    r = pl.program_id(1)

    if pad_tiles:
        @pl.when(r < pad_tiles)
        def _():
            k_ref[...] = jnp.zeros(k_ref.shape, BF16)
            v_ref[...] = jnp.zeros(v_ref.shape, BF16)

    @pl.when(r >= pad_tiles)
    def _():
        h = _rms(x_ref[...], g_ref[...]).astype(BF16)

        def seg(i):
            return jnp.dot(h, w_ref[:, i * D_MODEL:(i + 1) * D_MODEL], preferred_element_type=F32)

        is_tail = r == pad_tiles + tiles - 1
        q_ref[...] = (seg(0) * QK_SCALE).astype(BF16)
        kz = seg(1)
        k_ref[...] = kz.astype(BF16)
        vz = seg(2)
        if v_feature_major:
            v_ref[0] = vz.T.astype(BF16)
        else:
            v_ref[...] = vz.astype(BF16)

        @pl.when(is_tail)
        def _():
            kt_ref[0] = kz.T
            vt_ref[0] = vz.T


def _proj_odd(x, g, w, *, tiles, pad_tiles, v_feature_major):
    m = x.shape[0]
    tm = min(ROW_TILE, m)
    nb = m // tm // tiles
    steps = pad_tiles + tiles
    src = lambda b, r: (b * tiles + jnp.maximum(r - pad_tiles, 0), 0)
    row = pl.BlockSpec((tm, D_MODEL), src)
    padded_row = pl.BlockSpec((tm, D_MODEL), lambda b, r: (b * steps + r, 0))
    tail = pl.BlockSpec((1, D_MODEL, tm), lambda b, r: (b, 0, 0))
    bfo = jax.ShapeDtypeStruct((m, D_MODEL), BF16)
    padded = jax.ShapeDtypeStruct((nb * steps * tm, D_MODEL), BF16)
    tailo = jax.ShapeDtypeStruct((nb, D_MODEL, tm), F32)
    if v_feature_major:
        vo = jax.ShapeDtypeStruct((nb, D_MODEL, steps * tm), BF16)
        vspec = pl.BlockSpec((1, D_MODEL, tm), lambda b, r: (b, 0, r))
    else:
        vo, vspec = padded, padded_row
    return pl.pallas_call(
        functools.partial(_proj_odd_kernel, tiles=tiles, pad_tiles=pad_tiles, v_feature_major=v_feature_major),
        grid=(nb, steps),
        in_specs=[row, _const_spec((1, D_MODEL)), _const_spec(w.shape)],
        out_specs=[row, padded_row, vspec, tail, tail],
        out_shape=[bfo, padded, vo, tailo, tailo],
        compiler_params=_params(2),
        name="proj_odd",
    )(x, g, w)


def _post_kernel(*refs, n_o):
    o_refs = refs[:n_o]
    x_ref, wo_ref, gpm_ref, gpre_ref, wup_ref, wdn_ref, gpf_ref, out_ref = refs[n_o:]
    sub = x_ref.shape[0] // POST_STREAMS
    rows = [slice(s * sub, (s + 1) * sub) for s in range(POST_STREAMS)]
    x1, h, acc = [], [], []
    for r in rows:
        o = o_refs[0][r] if n_o == 1 else jnp.concatenate([ref[r] for ref in o_refs], axis=1)
        mixed = jnp.dot(o, wo_ref[...], preferred_element_type=F32)
        x1.append(x_ref[r] + _rms(mixed, gpm_ref[...]))
        h.append(_rms(x1[-1], gpre_ref[...]).astype(BF16))
        acc.append(jnp.zeros(x1[-1].shape, F32))
    for c in range(D_FF // FF_CHUNK):
        for s in range(POST_STREAMS):
            u = jnp.dot(h[s], wup_ref[:, c * FF_CHUNK:(c + 1) * FF_CHUNK], preferred_element_type=F32)
            u = jnp.maximum(u, 0.0)
            acc[s] = acc[s] + jnp.dot((u * u).astype(BF16), wdn_ref[c * FF_CHUNK:(c + 1) * FF_CHUNK, :],
                                      preferred_element_type=F32)
    for s, r in enumerate(rows):
        out_ref[r] = x1[s] + _rms(acc[s], gpf_ref[...])


def _post(o_list, x, wo, gpm, gpre, wup, wdn, gpf):
    m = x.shape[0]
    tm = min(ROW_TILE * POST_STREAMS, m)
    row = lambda w: pl.BlockSpec((tm, w), lambda i: (i, 0))
    gspec = _const_spec((1, D_MODEL))
    return pl.pallas_call(
        functools.partial(_post_kernel, n_o=len(o_list)),
        grid=(m // tm,),
        in_specs=[row(o.shape[1]) for o in o_list]
                 + [row(D_MODEL), _const_spec(wo.shape), gspec, gspec,
                    _const_spec(wup.shape), _const_spec(wdn.shape), gspec],
        out_specs=row(D_MODEL),
        out_shape=jax.ShapeDtypeStruct((m, D_MODEL), F32),
        compiler_params=_params(1),
        name="post_mix_mlp",
    )(*o_list, x, wo, gpm, gpre, wup, wdn, gpf)


def _band_bias_vectors(table):
    n = 2 * BAND_PAST
    m = np.arange(n)
    d = np.where(m <= n - REL_CLIP, -m, n - m)
    idx = np.clip(d + BAND_PAST, -REL_CLIP, REL_CLIP) + REL_CLIP
    return jnp.take(table.astype(F32), jnp.asarray(idx, jnp.int32), axis=1).reshape(H_C // 2, 2, n)


def kernel(x_prompt, x_sample, cache_a_k, cache_a_v, cache_b_k, cache_b_v, cache_b_logf, cache_c_k, cache_c_v,
           w_in_even, b_forget, lam_q1, lam_k1, lam_q2, lam_k2, subln_g, w_out_even, w_in_odd, rel_bias,
           w_out_odd, g_pre_mix, g_post_mix, g_pre_ffn, g_post_ffn, w_ffn_up, w_ffn_down):
    b_p, t_p, _ = x_prompt.shape
    b_s, t_s, _ = x_sample.shape
    past = cache_b_logf.shape[2]
    m_p, m_s = b_p * t_p, b_s * t_s
    xp = x_prompt.reshape(m_p, D_MODEL)
    xs = x_sample.reshape(m_s, D_MODEL)
    gvec = lambda a, l: a[l].reshape(1, D_MODEL)

    lam_init = 0.8 - 0.6 * math.exp(-0.3 * 0)
    w_in_t = w_in_even[0].T
    w_t = w_in_t[:N_SEG * A_W].astype(BF16)
    wft = jnp.pad(w_in_t[N_SEG * A_W:], ((0, 16 - H_B), (0, 0))).astype(BF16)
    bf_col = b_forget[0].reshape(H_B, 1)
    slopes = 2.0 ** (-8.0 * jnp.arange(1, H_A + 1, dtype=F32) / H_A)
    lamv = jnp.stack([lam_q1[0], lam_k1[0], lam_q2[0], lam_k2[0]]).astype(F32)
    sub_g = subln_g[0].reshape(1, 2 * HEAD_DIM)
    wo0 = w_out_even[0].astype(BF16)
    wup0, wdn0 = w_ffn_up[0].astype(BF16), w_ffn_down[0].astype(BF16)

    def even_layer(x, bsz, t, caches):
        m = x.shape[0]
        tm = min(ROW_TILE, m)
        (qa, ka, kab, va, vab, qb, kb, kbb, vb, vbb, lft) = _proj_even(
            x, gvec(g_pre_mix, 0), w_t, wft, bf_col,
            tiles_per_batch=(t // tm) if caches is None else None)
        r3 = lambda a: a.reshape(bsz, t, A_W)
        lft = lft.transpose(1, 0, 2).reshape(H_B, bsz, t).transpose(1, 0, 2)
        tok_head = lambda a: a.reshape(1, bsz, t, H_A, 2 * HEAD_DIM)
        if caches is None:
            oa = _flash_t("diff", r3(qa), r3(kab), vab, (slopes, lamv, sub_g), tq=512, lam_init=lam_init)
            ob = _flash_t("fox", r3(qb), r3(kbb), vbb, (_cumsum_time(lft),), tq=512)
            time_minor = lambda a: a.reshape(bsz, H_B, HEAD_DIM, t).transpose(0, 3, 1, 2)[None]
            new = (tok_head(ka), tok_head(va), time_minor(kb), time_minor(vb), lft.transpose(0, 2, 1)[None])
        else:
            c_ka, c_va, c_kb, c_vb, c_lf = caches
            p_len = c_ka.shape[1]
            t_all = p_len + t
            t_pad = -(-t_all // LANES) * LANES
            assert p_len % CHUNK == 0 and t <= CHUNK
            lf_all = jnp.concatenate([c_lf.astype(F32).transpose(0, 2, 1), lft], axis=2)
            cum = _cumsum_time(jnp.pad(lf_all, ((0, 0), (0, 0), (0, t_pad - t_all))))
            tok_head_rows = lambda c: c.reshape(bsz, p_len * H_A, 2 * HEAD_DIM)
            feat_major = lambda c: c.transpose(0, 2, 3, 1).reshape(bsz, A_W, p_len)
            oa = _sample_attn("diff", r3(qa), tok_head_rows(c_ka), tok_head_rows(c_va), r3(kab), r3(vab),
                              (slopes, lamv, sub_g), past=p_len, lam_init=lam_init)
            ob = _sample_attn("fox", r3(qb), feat_major(c_kb), feat_major(c_vb), r3(kbb), r3(vbb),
                              (cum,), past=p_len)
            new = (tok_head(ka), tok_head(va), kb.reshape(1, bsz, t, H_B, HEAD_DIM),
                   vb.reshape(1, bsz, t, H_B, HEAD_DIM), lft.transpose(0, 2, 1)[None])
        x_out = _post([oa.reshape(m, A_W), ob.reshape(m, A_W)], x, wo0, gvec(g_post_mix, 0),
                      gvec(g_pre_ffn, 0), wup0, wdn0, gvec(g_post_ffn, 0))
        return x_out, new

    xp, new_p_even = even_layer(xp, b_p, t_p, None)
    xs, new_s_even = even_layer(xs, b_s, t_s, (cache_a_k[0], cache_a_v[0], cache_b_k[0], cache_b_v[0],
                                              cache_b_logf[0]))

    w_odd = w_in_odd[0].astype(BF16)
    cvec = _band_bias_vectors(rel_bias[0])
    to_cache = lambda a: a.reshape(a.shape[0], H_C, HEAD_DIM, a.shape[2]).transpose(0, 3, 1, 2)[None]
    from_cache = lambda c: c.transpose(0, 2, 3, 1).reshape(c.shape[0], D_MODEL, c.shape[1])
    wo1 = w_out_odd[0].astype(BF16)
    wup1, wdn1 = w_ffn_up[1].astype(BF16), w_ffn_down[1].astype(BF16)

    def odd_layer(x, bsz, t, caches):
        m = x.shape[0]
        tm = min(ROW_TILE, m)
        r3 = lambda a: a.reshape(bsz, t, D_MODEL)
        if caches is None:
            assert BAND_PAST % tm == 0 and tm == min(BAND_PAST, t)
            q, kp, vtp, k_tail, v_tail = _proj_odd(x, gvec(g_pre_mix, 1), w_odd, tiles=t // tm,
                                                   pad_tiles=BAND_PAST // tm, v_feature_major=True)
            o = _band_t(r3(q), kp.reshape(bsz, BAND_PAST + t, D_MODEL), vtp, cvec, tq=256)
            new = (to_cache(k_tail), to_cache(v_tail))
        else:
            c_k, c_v = caches
            assert c_k.shape[1] == BAND_PAST and past % CHUNK == 0 and past >= BAND_PAST and t <= CHUNK
            q, k, v, k_tail, v_tail = _proj_odd(x, gvec(g_pre_mix, 1), w_odd, tiles=m // tm,
                                                pad_tiles=0, v_feature_major=False)
            per_stream = lambda tail: tail.reshape(D_MODEL, bsz, t).transpose(1, 0, 2)
            o, k_buf, v_buf = _sample_attn("band", r3(q), from_cache(c_k), from_cache(c_v), r3(k), r3(v),
                                           (cvec,), past=past, new_t=(per_stream(k_tail), per_stream(v_tail)))
            new = (to_cache(k_buf), to_cache(v_buf))
        x_out = _post([o.reshape(m, D_MODEL)], x, wo1, gvec(g_post_mix, 1), gvec(g_pre_ffn, 1),
                      wup1, wdn1, gvec(g_post_ffn, 1))
        return x_out, new

    xp, new_p_odd = odd_layer(xp, b_p, t_p, None)
    xs, new_s_odd = odd_layer(xs, b_s, t_s, (cache_c_k[0], cache_c_v[0]))

    return (xp.reshape(b_p, t_p, D_MODEL), xs.reshape(b_s, t_s, D_MODEL),
            *new_p_even, *new_p_odd, *new_s_even, *new_s_odd)
```

```python
import functools
import math

import numpy as np
import jax
import jax.numpy as jnp
from jax import lax
from jax.experimental import pallas as pl
from jax.experimental.pallas import tpu as pltpu

F32 = jnp.float32
BF16 = jnp.bfloat16

D_MODEL = 1024
HEAD_DIM = 64
CHUNK = 64
H_A = 4
H_B = 8
H_C = 16
A_W = 512
N_SEG = 6
BAND_PAST = 512
REL_CLIP = 256
D_FF = 4 * D_MODEL
NORM_EPS = 1e-6
NEG_INF = -1e30
QK_SCALE = HEAD_DIM ** -0.5

CHUNK_SHIFT = CHUNK.bit_length() - 1
LANES = 128
BF16_ROWS = 16
ROW_TILE = 512
FF_CHUNK = 1024
POST_STREAMS = 2
VMEM_LIMIT = 56 * 1024 * 1024

_NT = (((1,), (1,)), ((), ()))


def _rms(x, g):
    ms = jnp.mean(x * x, axis=-1, keepdims=True)
    return x * lax.rsqrt(ms + NORM_EPS) * g


def _log_sigmoid(x):
    t = -x
    return -(jnp.maximum(t, 0.0) + jnp.log1p(jnp.exp(-jnp.abs(t))))


def _const_spec(shape):
    nd = len(shape)
    return pl.BlockSpec(shape, lambda *_: (0,) * nd, pipeline_mode=pl.Buffered(1))


def _params(n_axes):
    return pltpu.CompilerParams(dimension_semantics=("arbitrary",) * n_axes,
                                vmem_limit_bytes=VMEM_LIMIT)


def _proj_even_kernel(x_ref, g_ref, wt_ref, wft_ref, bft_ref,
                      qa_ref, ka_ref, kab_ref, va_ref, vab_ref,
                      qb_ref, kb_ref, kbb_ref, vb_ref, vbb_ref, lft_ref, *, feature_major):
    h = _rms(x_ref[...], g_ref[...]).astype(BF16)
    tm = h.shape[0]

    def seg(i):
        return lax.dot_general(h, wt_ref[i * A_W:(i + 1) * A_W, :], _NT, preferred_element_type=F32)

    def seg_t(i):
        return lax.dot_general(wt_ref[i * A_W:(i + 1) * A_W, :], h, _NT, preferred_element_type=F32)

    def store_token_head_rows(ref, z):
        for hd in range(H_A):
            ref[pl.ds(hd, tm, stride=H_A), :] = z[:, hd * LANES:(hd + 1) * LANES]

    qa_ref[...] = (seg(0) * QK_SCALE).astype(BF16)
    z = seg(1)
    store_token_head_rows(ka_ref, z)
    kab_ref[...] = z.astype(BF16)
    z = seg(2)
    store_token_head_rows(va_ref, z)
    if feature_major:
        vab_ref[0] = z.T.astype(BF16)
    else:
        vab_ref[...] = z.astype(BF16)
    qb_ref[...] = (seg(3) * QK_SCALE).astype(BF16)
    z = seg(4)
    kbb_ref[...] = z.astype(BF16)
    if feature_major:
        kb_ref[0] = z.T
        zt = seg_t(5)
        vb_ref[0] = zt
        vbb_ref[0] = zt.astype(BF16)
    else:
        kb_ref[...] = z
        z = seg(5)
        vb_ref[...] = z
        vbb_ref[...] = z.astype(BF16)
    fzt = lax.dot_general(wft_ref[...], h, _NT, preferred_element_type=F32)
    lft_ref[0] = _log_sigmoid(fzt[:H_B] + bft_ref[...])


def _proj_even(x, g, w_t, wft, bf_col, *, tiles_per_batch=None):
    m = x.shape[0]
    tm = min(ROW_TILE, m)
    n = m // tm
    row = lambda w: pl.BlockSpec((tm, w), lambda i: (i, 0))
    bfo = jax.ShapeDtypeStruct((m, A_W), BF16)
    tok_head = jax.ShapeDtypeStruct((m * H_A, LANES), F32)
    tok_head_spec = pl.BlockSpec((tm * H_A, LANES), lambda i: (i, 0))
    if tiles_per_batch:
        tpb = tiles_per_batch
        fm = lambda dt: jax.ShapeDtypeStruct((n // tpb, A_W, tpb * tm), dt)
        fm_spec = pl.BlockSpec((1, A_W, tm), lambda i: (i // tpb, 0, i % tpb))
        outs = [(bfo, row(A_W)), (tok_head, tok_head_spec), (bfo, row(A_W)), (tok_head, tok_head_spec),
                (fm(BF16), fm_spec), (bfo, row(A_W)), (fm(F32), fm_spec), (bfo, row(A_W)),
                (fm(F32), fm_spec), (fm(BF16), fm_spec)]
    else:
        f32o = jax.ShapeDtypeStruct((m, A_W), F32)
        outs = [(bfo, row(A_W)), (tok_head, tok_head_spec), (bfo, row(A_W)), (tok_head, tok_head_spec),
                (bfo, row(A_W)), (bfo, row(A_W)), (f32o, row(A_W)), (bfo, row(A_W)),
                (f32o, row(A_W)), (bfo, row(A_W))]
    outs.append((jax.ShapeDtypeStruct((n, H_B, tm), F32), pl.BlockSpec((1, H_B, tm), lambda i: (i, 0, 0))))
    return pl.pallas_call(
        functools.partial(_proj_even_kernel, feature_major=bool(tiles_per_batch)),
        grid=(n,),
        in_specs=[row(D_MODEL), _const_spec((1, D_MODEL)), _const_spec(w_t.shape),
                  _const_spec(wft.shape), _const_spec(bf_col.shape)],
        out_specs=[s for _, s in outs],
        out_shape=[o for o, _ in outs],
        compiler_params=_params(1),
        name="proj_even",
    )(x, g, w_t, wft, bf_col)


def _cumsum_kernel(x_ref, o_ref):
    x = x_ref[0]
    t = x.shape[1]
    col = lax.broadcasted_iota(jnp.int32, x.shape, 1)
    s = 1
    while s < t:
        x = x + jnp.where(col >= s, pltpu.roll(x, s, 1), 0.0)
        s *= 2
    o_ref[0] = x


def _cumsum_time(x):
    b, h, t = x.shape
    spec = pl.BlockSpec((1, h, t), lambda i: (i, 0, 0))
    return pl.pallas_call(
        _cumsum_kernel, grid=(b,), in_specs=[spec], out_specs=spec,
        out_shape=jax.ShapeDtypeStruct(x.shape, F32),
        compiler_params=_params(1), name="cumsum_time",
    )(x)


def _stack_pair(q):
    q32 = q.astype(F32)
    lo = lax.broadcasted_iota(jnp.int32, q32.shape, 1) < HEAD_DIM
    return jnp.concatenate([jnp.where(lo, q32, 0.0), jnp.where(lo, 0.0, q32)], axis=0).astype(BF16)


def _sample_attn_kernel(*refs, mode, t, past, lam_init):
    if mode == "diff":
        slopes_ref, lamv_ref, g_ref, q_ref, kc_ref, vc_ref, kn_ref, vn_ref, o_ref = refs
    elif mode == "fox":
        cum_ref, q_ref, kc_ref, vc_ref, kn_ref, vn_ref, o_ref = refs
    else:
        c_ref, q_ref, kc_ref, vc_ref, kn_ref, vn_ref, knt_ref, vnt_ref, o_ref, kco_ref, vco_ref, bias_sc = refs
        kco_ref[0] = jnp.concatenate([kc_ref[0, :, t:], knt_ref[0]], axis=1)
        vco_ref[0] = jnp.concatenate([vc_ref[0, :, t:], vnt_ref[0]], axis=1)

        @pl.when(pl.program_id(0) == 0)
        def _():
            for hd in range(bias_sc.shape[0]):
                c = jnp.broadcast_to(c_ref[hd // 2, hd % 2:hd % 2 + 1, :], (t, c_ref.shape[2]))
                bias_sc[hd] = pltpu.roll(c, 0, 1, stride=1, stride_axis=0)
    pairs = q_ref.shape[2] // LANES
    n_c = vc_ref.shape[1] // pairs if mode == "diff" else kc_ref.shape[2]
    row = lax.broadcasted_iota(jnp.int32, (t, n_c), 0)
    col = lax.broadcasted_iota(jnp.int32, (t, n_c), 1)
    row_n = lax.broadcasted_iota(jnp.int32, (t, t), 0)
    col_n = lax.broadcasted_iota(jnp.int32, (t, t), 1)
    lo = lax.broadcasted_iota(jnp.int32, (t, LANES), 1) < HEAD_DIM

    for p in range(pairs):
        lanes = slice(p * LANES, (p + 1) * LANES)
        q2 = _stack_pair(q_ref[0, :, lanes])
        kn = kn_ref[0, :, lanes]
        vn = vn_ref[0, :, lanes]
        if mode == "diff":
            kc = kc_ref[0, pl.ds(p, n_c, stride=pairs), :].astype(BF16)
            vc = vc_ref[0, pl.ds(p, n_c, stride=pairs), :].astype(BF16)
            s_c = lax.dot_general(q2, kc, _NT, preferred_element_type=F32)
        else:
            kc = kc_ref[0, lanes, :].astype(BF16)
            vc = vc_ref[0, lanes, :].astype(BF16)
            s_c = jnp.dot(q2, kc, preferred_element_type=F32)
        s_n = lax.dot_general(q2, kn, _NT, preferred_element_type=F32)

        outs = []
        for a in range(2):
            rows = slice(a * t, (a + 1) * t)
            if mode == "diff":
                slope = slopes_ref[p]
                t_c = s_c[rows] - slope * (past + row - col).astype(F32)
                t_n = s_n[rows] - slope * jnp.abs(row_n - col_n).astype(F32)
            elif mode == "fox":
                ck = cum_ref[0, 2 * p + a]
                t_c = s_c[rows] - ck[:, :n_c]
                t_n = jnp.where(col_n <= row_n, s_n[rows] - ck[:, n_c:n_c + t], NEG_INF)
            else:
                t_c = s_c[rows] + bias_sc[2 * p + a, :, :n_c]
                t_n = s_n[rows] + bias_sc[2 * p + a, :, n_c:n_c + t]
            m = jnp.maximum(jnp.max(t_c, axis=1, keepdims=True), jnp.max(t_n, axis=1, keepdims=True))
            p_c = jnp.exp(t_c - m)
            p_n = jnp.exp(t_n - m)
            l = jnp.sum(p_c, axis=1, keepdims=True) + jnp.sum(p_n, axis=1, keepdims=True)
            if mode == "diff":
                acc = jnp.dot(p_c.astype(BF16), vc, preferred_element_type=F32)
            else:
                acc = lax.dot_general(p_c.astype(BF16), vc, _NT, preferred_element_type=F32)
            acc = acc + jnp.dot(p_n.astype(BF16), vn, preferred_element_type=F32)
            outs.append(acc / l)
        if mode == "diff":
            lv = lamv_ref[...]
            lam = (jnp.exp(jnp.sum(lv[0:1] * lv[1:2], axis=1, keepdims=True))
                   - jnp.exp(jnp.sum(lv[2:3] * lv[3:4], axis=1, keepdims=True)) + lam_init)
            oa = outs[0] - lam * outs[1]
            o_ref[0, :, lanes] = (_rms(oa, g_ref[...]) * (1.0 - lam_init)).astype(BF16)
        else:
            o_ref[0, :, lanes] = jnp.where(lo, outs[0], outs[1]).astype(BF16)


def _sample_attn(mode, q, kc, vc, kn, vn, extras, *, past, lam_init=0.0, new_t=()):
    b, t, width = q.shape
    whole = lambda a: pl.BlockSpec((1,) + a.shape[1:], lambda i: (i,) + (0,) * (a.ndim - 1))
    const = lambda a: pl.BlockSpec(a.shape, lambda i: (0,) * a.ndim)
    if mode == "diff":
        slopes, lamv, g = extras
        extra_specs = [pl.BlockSpec(memory_space=pltpu.SMEM), const(lamv), const(g)]
    elif mode == "fox":
        extras = (extras[0][:, :, None, :],)
        extra_specs = [whole(extras[0])]
    else:
        extra_specs = [const(extras[0])]
    operands = [q, kc, vc, kn, vn]
    out_specs, out_shape = whole(q), jax.ShapeDtypeStruct(q.shape, BF16)
    scratch = []
    if mode == "band":
        scratch = [pltpu.VMEM((width // HEAD_DIM, t, extras[0].shape[2]), F32)]
        operands += list(new_t)
        out_specs = [out_specs, whole(kc), whole(vc)]
        out_shape = [out_shape, jax.ShapeDtypeStruct(kc.shape, F32), jax.ShapeDtypeStruct(vc.shape, F32)]
    return pl.pallas_call(
        functools.partial(_sample_attn_kernel, mode=mode, t=t, past=past, lam_init=lam_init),
        grid=(b,),
        in_specs=extra_specs + [whole(a) for a in operands],
        out_specs=out_specs,
        out_shape=out_shape,
        scratch_shapes=scratch,
        compiler_params=_params(1),
        name="sample_" + mode,
    )(*extras, *operands)


ONES_ROWS = BF16_ROWS


def _with_ones(vt):
    return jnp.concatenate([vt, jnp.ones((ONES_ROWS, vt.shape[1]), BF16)], axis=0)


AUG_TERMS = 3
FLASH_STREAMS = 4
BAND_STREAMS = 4


def _lane_terms(lane, first, terms):
    out = jnp.zeros(terms[0].shape, F32)
    for t, v in enumerate(terms):
        out = jnp.where(lane == first + t, v, out)
    return out


def _flash_t_kernel(*refs, mode, tq, lam_init):
    tk = tq
    n_extra = 3 if mode == "diff" else 1
    extra_refs, (q_ref, k_ref, vt_ref, o_ref, m_sc, acc_sc, q2_sc, mu_sc, kaug_sc, bdiag_sc) = (
        refs[:n_extra], refs[n_extra:n_extra + 10])
    s_bufs = refs[n_extra + 10:]
    if mode == "diff":
        slopes_ref, lamv_ref, g_ref = extra_refs
    else:
        (cum_ref,) = extra_refs
    pair = pl.program_id(0)
    qi = pl.program_id(2)
    t_all = kaug_sc.shape[1]
    streams = range(FLASH_STREAMS)

    @pl.when((pl.program_id(1) == 0) & (qi == 0))
    def _():
        row = lax.broadcasted_iota(jnp.int32, (tk, tq), 0)
        col = lax.broadcasted_iota(jnp.int32, (tk, tq), 1)
        lane_q = lax.broadcasted_iota(jnp.int32, (2 * tq, LANES), 1)
        first_q = jnp.where(lax.broadcasted_iota(jnp.int32, (2 * tq, LANES), 0) < tq, 0, AUG_TERMS)
        if mode == "diff":
            slope = slopes_ref[pair]
            visible = (row >> CHUNK_SHIFT) <= (col >> CHUNK_SHIFT)
            bdiag_sc[...] = jnp.where(visible, -2.0 * slope * jnp.maximum(row - col, 0).astype(F32), NEG_INF)
            for s in streams:
                q2_sc[s, :, LANES:] = jnp.where(lane_q < AUG_TERMS, 1.0, 0.0).astype(BF16)
            for c0 in range(0, t_all, tk):
                j = c0 + lax.broadcasted_iota(jnp.int32, (tk, LANES), 0)
                lane = lax.broadcasted_iota(jnp.int32, (tk, LANES), 1)
                terms = [((j >> 8) << 8).astype(F32), (((j >> 4) & 15) << 4).astype(F32), (j & 15).astype(F32)]
                kaug_sc[0, c0:c0 + tk, :] = (slope * _lane_terms(lane, 0, terms)).astype(BF16)
        else:
            bdiag_sc[...] = jnp.where(row <= col, 0.0, NEG_INF)
            mine = (lane_q >= first_q) & (lane_q < first_q + AUG_TERMS)
            for s in streams:
                q2_sc[s, :, LANES:] = jnp.where(mine, -1.0, 0.0).astype(BF16)

    if mode == "fox":
        @pl.when(qi == 0)
        def _():
            for s in streams:
                for c0 in range(0, t_all, tk):
                    lane = lax.broadcasted_iota(jnp.int32, (tk, LANES), 1)
                    feat = jnp.zeros((tk, LANES), F32)
                    for a in range(2):
                        r = cum_ref[s, 2 * pair + a, :, c0:c0 + tk]
                        ck = jnp.broadcast_to(r, (LANES, tk)).T
                        hi = ck.astype(BF16).astype(F32)
                        mid = (ck - hi).astype(BF16).astype(F32)
                        lo = ((ck - hi) - mid).astype(BF16).astype(F32)
                        feat = feat + _lane_terms(lane, AUG_TERMS * a, [hi, mid, lo])
                    kaug_sc[s, c0:c0 + tk, :] = feat.astype(BF16)

    for s in streams:
        q2_sc[s, :, :LANES] = _stack_pair(q_ref[s])
    m_sc[...] = jnp.full(m_sc.shape, NEG_INF, F32)
    acc_sc[...] = jnp.zeros(acc_sc.shape, F32)

    def stage_a(k0, slot, diag):
        for s in streams:
            kaug = kaug_sc[0 if mode == "diff" else s, pl.ds(k0, tk), :]
            kk = jnp.concatenate([k_ref[s, pl.ds(k0, tk), :], kaug], axis=1)
            st = lax.dot_general(kk, q2_sc[s], _NT, preferred_element_type=F32)
            for a in range(2):
                sa = st[:, a * tq:(a + 1) * tq]
                if diag:
                    sa = sa + bdiag_sc[...]
                s_bufs[2 * s + slot][:, a * tq:(a + 1) * tq] = sa
                mu_sc[s, slot, :, a * tq:(a + 1) * tq] = jnp.max(sa, axis=0, keepdims=True)

    def stage_b(k0, slot):
        for s in streams:
            vt = vt_ref[s, :, pl.ds(k0, tk)]
            for a in range(2):
                m_prev = m_sc[s, a]
                m_next = jnp.maximum(m_prev, mu_sc[s, slot, :, a * tq:(a + 1) * tq])
                p = jnp.exp(s_bufs[2 * s + slot][:, a * tq:(a + 1) * tq] - m_next)
                alpha = jnp.exp(m_prev - m_next)
                vta = _with_ones(vt if mode == "diff" else vt[a * HEAD_DIM:(a + 1) * HEAD_DIM])
                acc_sc[s, a] = alpha * acc_sc[s, a] + jnp.dot(vta, p.astype(BF16), preferred_element_type=F32)
                m_sc[s, a] = m_next

    blk = lambda j: pl.multiple_of(j * tk, tk)
    pl.when(qi == 0)(lambda: stage_a(0, 0, True))
    pl.when(qi > 0)(lambda: stage_a(0, 0, False))

    def two_blocks(i, carry):
        stage_a(blk(2 * i + 1), 1, False)
        stage_b(blk(2 * i), 0)
        stage_a(blk(2 * i + 2), 0, False)
        stage_b(blk(2 * i + 1), 1)
        return carry

    lax.fori_loop(0, (qi - 1) // 2, two_blocks, 0)

    @pl.when(qi == 0)
    def _():
        stage_b(0, 0)

    @pl.when(qi % 2 == 1)
    def _():
        stage_a(blk(qi), 1, True)
        stage_b(blk(qi - 1), 0)
        stage_b(blk(qi), 1)

    @pl.when((qi % 2 == 0) & (qi > 0))
    def _():
        stage_a(blk(qi - 1), 1, False)
        stage_b(blk(qi - 2), 0)
        stage_a(blk(qi), 0, True)
        stage_b(blk(qi - 1), 1)
        stage_b(blk(qi), 0)

    d = acc_sc.shape[2] - ONES_ROWS
    for s in streams:
        o0 = acc_sc[s, 0, :d, :] / acc_sc[s, 0, d:d + 1, :]
        o1 = acc_sc[s, 1, :d, :] / acc_sc[s, 1, d:d + 1, :]
        if mode == "diff":
            lv = lamv_ref[...]
            lam = (jnp.exp(jnp.sum(lv[0:1] * lv[1:2], axis=1, keepdims=True))
                   - jnp.exp(jnp.sum(lv[2:3] * lv[3:4], axis=1, keepdims=True)) + lam_init)
            oa = o0 - lam * o1
            ms = jnp.mean(oa * oa, axis=0, keepdims=True)
            y = (oa * lax.rsqrt(ms + NORM_EPS)).T * g_ref[...]
            o_ref[s] = (y * (1.0 - lam_init)).astype(BF16)
        else:
            o_ref[s] = jnp.concatenate([o0, o1], axis=0).T.astype(BF16)


def _flash_t(mode, q, k, vt, extras, *, tq, lam_init=0.0):
    b, t, width = q.shape
    pairs = width // LANES
    ns = FLASH_STREAMS
    assert b % ns == 0
    qspec = pl.BlockSpec((ns, tq, LANES), lambda p, bi, i: (bi, i, p))
    kspec = pl.BlockSpec((ns, t, LANES), lambda p, bi, i: (bi, 0, p))
    vspec = pl.BlockSpec((ns, LANES, t), lambda p, bi, i: (bi, p, 0))
    acc_rows = (LANES if mode == "diff" else HEAD_DIM) + ONES_ROWS
    scratch = [pltpu.VMEM((ns, 2, 1, tq), F32),
               pltpu.VMEM((ns, 2, acc_rows, tq), F32),
               pltpu.VMEM((ns, 2 * tq, 2 * LANES), BF16),
               pltpu.VMEM((ns, 2, 1, 2 * tq), F32),
               pltpu.VMEM((1 if mode == "diff" else ns, t, LANES), BF16),
               pltpu.VMEM((tq, tq), F32)]
    scratch += [pltpu.VMEM((tq, 2 * tq), F32)] * (2 * ns)
    if mode == "diff":
        slopes, lamv, g = extras
        extra_specs = [pl.BlockSpec(memory_space=pltpu.SMEM),
                       pl.BlockSpec(lamv.shape, lambda p, bi, i: (0, 0)),
                       pl.BlockSpec(g.shape, lambda p, bi, i: (0, 0))]
    else:
        extras = (extras[0][:, :, None, :],)
        extra_specs = [pl.BlockSpec((ns,) + extras[0].shape[1:], lambda p, bi, i: (bi, 0, 0, 0))]
    return pl.pallas_call(
        functools.partial(_flash_t_kernel, mode=mode, tq=tq, lam_init=lam_init),
        grid=(pairs, b // ns, t // tq),
        in_specs=extra_specs + [qspec, kspec, vspec],
        out_specs=qspec,
        out_shape=jax.ShapeDtypeStruct(q.shape, BF16),
        scratch_shapes=scratch,
        compiler_params=_params(3),
        name="flash_t_" + mode,
    )(*extras, q, k, vt)


def _band_t_kernel(c_ref, q_ref, k_ref, vt_ref, o_ref, mu_sc, bias_sc, *s_bufs, tq, t):
    window = BAND_PAST + tq
    n_q = t // tq
    streams = range(BAND_STREAMS)

    @pl.when(pl.program_id(1) == 0)
    def _():
        row = lax.broadcasted_iota(jnp.int32, (tq, window), 0)
        col = lax.broadcasted_iota(jnp.int32, (tq, window), 1)
        kc, qc = col >> CHUNK_SHIFT, row >> CHUNK_SHIFT
        visible = (kc >= qc) & (kc <= qc + BAND_PAST // CHUNK)
        for a in range(2):
            c = jnp.broadcast_to(c_ref[0, a:a + 1, :], (tq, c_ref.shape[2]))
            bias = pltpu.roll(c, 0, 1, stride=1, stride_axis=0)[:, :window]
            bias_sc[a] = jnp.where(visible, bias, NEG_INF).T

    lane_q = lax.broadcasted_iota(jnp.int32, (2 * tq, LANES), 1)
    q_extra = jnp.where(lane_q == 0, 1.0, 0.0).astype(BF16)

    def stage_a(qb, slot):
        w0 = pl.multiple_of(qb * tq, tq)
        pad_row = (w0 + lax.broadcasted_iota(jnp.int32, (window, LANES), 0)) < BAND_PAST
        lane_k = lax.broadcasted_iota(jnp.int32, (window, LANES), 1)
        k_extra = jnp.where(pad_row & (lane_k == 0), NEG_INF, 0.0).astype(BF16)
        for s in streams:
            q2 = jnp.concatenate([_stack_pair(q_ref[s, pl.ds(w0, tq), :]), q_extra], axis=1)
            kk = jnp.concatenate([k_ref[s, pl.ds(w0, window), :], k_extra], axis=1)
            st = lax.dot_general(kk, q2, _NT, preferred_element_type=F32)
            for a in range(2):
                u = st[:, a * tq:(a + 1) * tq] + bias_sc[a]
                s_bufs[2 * s + slot][:, a * tq:(a + 1) * tq] = u
                mu_sc[s, slot, :, a * tq:(a + 1) * tq] = jnp.max(u, axis=0, keepdims=True)

    def stage_b(qb, slot):
        w0 = pl.multiple_of(qb * tq, tq)
        for s in streams:
            vt = vt_ref[s, :, pl.ds(w0, window)]
            outs = []
            for a in range(2):
                p = jnp.exp(s_bufs[2 * s + slot][:, a * tq:(a + 1) * tq]
                            - mu_sc[s, slot, :, a * tq:(a + 1) * tq])
                vta = _with_ones(vt[a * HEAD_DIM:(a + 1) * HEAD_DIM])
                r = jnp.dot(vta, p.astype(BF16), preferred_element_type=F32)
                outs.append(r[:HEAD_DIM] / r[HEAD_DIM:HEAD_DIM + 1])
            o_ref[s, pl.ds(w0, tq), :] = jnp.concatenate(outs, axis=0).T.astype(BF16)

    stage_a(0, 0)

    def two_blocks(i, carry):
        stage_a(2 * i + 1, 1)
        stage_b(2 * i, 0)
        stage_a(2 * i + 2, 0)
        stage_b(2 * i + 1, 1)
        return carry

    lax.fori_loop(0, n_q // 2 - 1, two_blocks, 0)
    stage_a(n_q - 1, 1)
    stage_b(n_q - 2, 0)
    stage_b(n_q - 1, 1)


def _band_t(q, k_pad, vt_pad, cvec, *, tq):
    b, t, width = q.shape
    pairs = width // LANES
    window = BAND_PAST + tq
    ns = BAND_STREAMS
    assert (t // tq) % 2 == 0 and t // tq >= 2 and b % ns == 0
    qspec = pl.BlockSpec((ns, t, LANES), lambda p, bi: (bi, 0, p))
    kspec = pl.BlockSpec((ns, k_pad.shape[1], LANES), lambda p, bi: (bi, 0, p))
    vspec = pl.BlockSpec((ns, LANES, vt_pad.shape[2]), lambda p, bi: (bi, p, 0))
    cspec = pl.BlockSpec((1, 2, cvec.shape[2]), lambda p, bi: (p, 0, 0))
    return pl.pallas_call(
        functools.partial(_band_t_kernel, tq=tq, t=t),
        grid=(pairs, b // ns),
        in_specs=[cspec, qspec, kspec, vspec],
        out_specs=qspec,
        out_shape=jax.ShapeDtypeStruct(q.shape, BF16),
        scratch_shapes=[pltpu.VMEM((ns, 2, 1, 2 * tq), F32),
                        pltpu.VMEM((2, window, tq), F32)]
                       + [pltpu.VMEM((window, 2 * tq), F32)] * (2 * ns),
        compiler_params=_params(2),
        name="band_t_attn",
    )(cvec, q, k_pad, vt_pad)


def _proj_odd_kernel(x_ref, g_ref, w_ref, q_ref, k_ref, v_ref, kt_ref, vt_ref, *,
                     tiles, pad_tiles, v_feature_major):
    r = pl.program_id(1)

    if pad_tiles:
        @pl.when(r < pad_tiles)
        def _():
            k_ref[...] = jnp.zeros(k_ref.shape, BF16)
            v_ref[...] = jnp.zeros(v_ref.shape, BF16)

    @pl.when(r >= pad_tiles)
    def _():
        h = _rms(x_ref[...], g_ref[...]).astype(BF16)

        def seg(i):
            return jnp.dot(h, w_ref[:, i * D_MODEL:(i + 1) * D_MODEL], preferred_element_type=F32)

        is_tail = r == pad_tiles + tiles - 1
        q_ref[...] = (seg(0) * QK_SCALE).astype(BF16)
        kz = seg(1)
        k_ref[...] = kz.astype(BF16)
        vz = seg(2)
        if v_feature_major:
            v_ref[0] = vz.T.astype(BF16)
        else:
            v_ref[...] = vz.astype(BF16)

        @pl.when(is_tail)
        def _():
            kt_ref[0] = kz.T
            vt_ref[0] = vz.T


def _proj_odd(x, g, w, *, tiles, pad_tiles, v_feature_major):
    m = x.shape[0]
    tm = min(ROW_TILE, m)
    nb = m // tm // tiles
    steps = pad_tiles + tiles
    src = lambda b, r: (b * tiles + jnp.maximum(r - pad_tiles, 0), 0)
    row = pl.BlockSpec((tm, D_MODEL), src)
    padded_row = pl.BlockSpec((tm, D_MODEL), lambda b, r: (b * steps + r, 0))
    tail = pl.BlockSpec((1, D_MODEL, tm), lambda b, r: (b, 0, 0))
    bfo = jax.ShapeDtypeStruct((m, D_MODEL), BF16)
    padded = jax.ShapeDtypeStruct((nb * steps * tm, D_MODEL), BF16)
    tailo = jax.ShapeDtypeStruct((nb, D_MODEL, tm), F32)
    if v_feature_major:
        vo = jax.ShapeDtypeStruct((nb, D_MODEL, steps * tm), BF16)
        vspec = pl.BlockSpec((1, D_MODEL, tm), lambda b, r: (b, 0, r))
    else:
        vo, vspec = padded, padded_row
    return pl.pallas_call(
        functools.partial(_proj_odd_kernel, tiles=tiles, pad_tiles=pad_tiles, v_feature_major=v_feature_major),
        grid=(nb, steps),
        in_specs=[row, _const_spec((1, D_MODEL)), _const_spec(w.shape)],
        out_specs=[row, padded_row, vspec, tail, tail],
        out_shape=[bfo, padded, vo, tailo, tailo],
        compiler_params=_params(2),
        name="proj_odd",
    )(x, g, w)


def _post_kernel(*refs, n_o):
    o_refs = refs[:n_o]
    x_ref, wo_ref, gpm_ref, gpre_ref, wup_ref, wdn_ref, gpf_ref, out_ref = refs[n_o:]
    sub = x_ref.shape[0] // POST_STREAMS
    rows = [slice(s * sub, (s + 1) * sub) for s in range(POST_STREAMS)]
    x1, h, acc = [], [], []
    for r in rows:
        o = o_refs[0][r] if n_o == 1 else jnp.concatenate([ref[r] for ref in o_refs], axis=1)
        mixed = jnp.dot(o, wo_ref[...], preferred_element_type=F32)
        x1.append(x_ref[r] + _rms(mixed, gpm_ref[...]))
        h.append(_rms(x1[-1], gpre_ref[...]).astype(BF16))
        acc.append(jnp.zeros(x1[-1].shape, F32))
    for c in range(D_FF // FF_CHUNK):
        for s in range(POST_STREAMS):
            u = jnp.dot(h[s], wup_ref[:, c * FF_CHUNK:(c + 1) * FF_CHUNK], preferred_element_type=F32)
            u = jnp.maximum(u, 0.0)
            acc[s] = acc[s] + jnp.dot((u * u).astype(BF16), wdn_ref[c * FF_CHUNK:(c + 1) * FF_CHUNK, :],
                                      preferred_element_type=F32)
    for s, r in enumerate(rows):
        out_ref[r] = x1[s] + _rms(acc[s], gpf_ref[...])


def _post(o_list, x, wo, gpm, gpre, wup, wdn, gpf):
    m = x.shape[0]
    tm = min(ROW_TILE * POST_STREAMS, m)
    row = lambda w: pl.BlockSpec((tm, w), lambda i: (i, 0))
    gspec = _const_spec((1, D_MODEL))
    return pl.pallas_call(
        functools.partial(_post_kernel, n_o=len(o_list)),
        grid=(m // tm,),
        in_specs=[row(o.shape[1]) for o in o_list]
                 + [row(D_MODEL), _const_spec(wo.shape), gspec, gspec,
                    _const_spec(wup.shape), _const_spec(wdn.shape), gspec],
        out_specs=row(D_MODEL),
        out_shape=jax.ShapeDtypeStruct((m, D_MODEL), F32),
        compiler_params=_params(1),
        name="post_mix_mlp",
    )(*o_list, x, wo, gpm, gpre, wup, wdn, gpf)


def _band_bias_vectors(table):
    n = 2 * BAND_PAST
    m = np.arange(n)
    d = np.where(m <= n - REL_CLIP, -m, n - m)
    idx = np.clip(d + BAND_PAST, -REL_CLIP, REL_CLIP) + REL_CLIP
    return jnp.take(table.astype(F32), jnp.asarray(idx, jnp.int32), axis=1).reshape(H_C // 2, 2, n)


def kernel(x_prompt, x_sample, cache_a_k, cache_a_v, cache_b_k, cache_b_v, cache_b_logf, cache_c_k, cache_c_v,
           w_in_even, b_forget, lam_q1, lam_k1, lam_q2, lam_k2, subln_g, w_out_even, w_in_odd, rel_bias,
           w_out_odd, g_pre_mix, g_post_mix, g_pre_ffn, g_post_ffn, w_ffn_up, w_ffn_down):
    b_p, t_p, _ = x_prompt.shape
    b_s, t_s, _ = x_sample.shape
    past = cache_b_logf.shape[2]
    m_p, m_s = b_p * t_p, b_s * t_s
    xp = x_prompt.reshape(m_p, D_MODEL)
    xs = x_sample.reshape(m_s, D_MODEL)
    gvec = lambda a, l: a[l].reshape(1, D_MODEL)

    lam_init = 0.8 - 0.6 * math.exp(-0.3 * 0)
    w_in_t = w_in_even[0].T
    w_t = w_in_t[:N_SEG * A_W].astype(BF16)
    wft = jnp.pad(w_in_t[N_SEG * A_W:], ((0, BF16_ROWS - H_B), (0, 0))).astype(BF16)
    bf_col = b_forget[0].reshape(H_B, 1)
    slopes = 2.0 ** (-8.0 * jnp.arange(1, H_A + 1, dtype=F32) / H_A)
    lamv = jnp.stack([lam_q1[0], lam_k1[0], lam_q2[0], lam_k2[0]]).astype(F32)
    sub_g = subln_g[0].reshape(1, 2 * HEAD_DIM)
    wo0 = w_out_even[0].astype(BF16)
    wup0, wdn0 = w_ffn_up[0].astype(BF16), w_ffn_down[0].astype(BF16)

    def even_layer(x, bsz, t, caches):
        m = x.shape[0]
        tm = min(ROW_TILE, m)
        (qa, ka, kab, va, vab, qb, kb, kbb, vb, vbb, lft) = _proj_even(
            x, gvec(g_pre_mix, 0), w_t, wft, bf_col,
            tiles_per_batch=(t // tm) if caches is None else None)
        r3 = lambda a: a.reshape(bsz, t, A_W)
        lft = lft.transpose(1, 0, 2).reshape(H_B, bsz, t).transpose(1, 0, 2)
        tok_head = lambda a: a.reshape(1, bsz, t, H_A, 2 * HEAD_DIM)
        if caches is None:
            oa = _flash_t("diff", r3(qa), r3(kab), vab, (slopes, lamv, sub_g), tq=512, lam_init=lam_init)
            ob = _flash_t("fox", r3(qb), r3(kbb), vbb, (_cumsum_time(lft),), tq=512)
            time_minor = lambda a: a.reshape(bsz, H_B, HEAD_DIM, t).transpose(0, 3, 1, 2)[None]
            new = (tok_head(ka), tok_head(va), time_minor(kb), time_minor(vb), lft.transpose(0, 2, 1)[None])
        else:
            c_ka, c_va, c_kb, c_vb, c_lf = caches
            p_len = c_ka.shape[1]
            t_all = p_len + t
            t_pad = -(-t_all // LANES) * LANES
            assert p_len % CHUNK == 0 and t <= CHUNK
            lf_all = jnp.concatenate([c_lf.astype(F32).transpose(0, 2, 1), lft], axis=2)
            cum = _cumsum_time(jnp.pad(lf_all, ((0, 0), (0, 0), (0, t_pad - t_all))))
            tok_head_rows = lambda c: c.reshape(bsz, p_len * H_A, 2 * HEAD_DIM)
            feat_major = lambda c: c.transpose(0, 2, 3, 1).reshape(bsz, A_W, p_len)
            oa = _sample_attn("diff", r3(qa), tok_head_rows(c_ka), tok_head_rows(c_va), r3(kab), r3(vab),
                              (slopes, lamv, sub_g), past=p_len, lam_init=lam_init)
            ob = _sample_attn("fox", r3(qb), feat_major(c_kb), feat_major(c_vb), r3(kbb), r3(vbb),
                              (cum,), past=p_len)
            new = (tok_head(ka), tok_head(va), kb.reshape(1, bsz, t, H_B, HEAD_DIM),
                   vb.reshape(1, bsz, t, H_B, HEAD_DIM), lft.transpose(0, 2, 1)[None])
        x_out = _post([oa.reshape(m, A_W), ob.reshape(m, A_W)], x, wo0, gvec(g_post_mix, 0),
                      gvec(g_pre_ffn, 0), wup0, wdn0, gvec(g_post_ffn, 0))
        return x_out, new

    xp, new_p_even = even_layer(xp, b_p, t_p, None)
    xs, new_s_even = even_layer(xs, b_s, t_s, (cache_a_k[0], cache_a_v[0], cache_b_k[0], cache_b_v[0],
                                              cache_b_logf[0]))

    w_odd = w_in_odd[0].astype(BF16)
    cvec = _band_bias_vectors(rel_bias[0])
    to_cache = lambda a: a.reshape(a.shape[0], H_C, HEAD_DIM, a.shape[2]).transpose(0, 3, 1, 2)[None]
    from_cache = lambda c: c.transpose(0, 2, 3, 1).reshape(c.shape[0], D_MODEL, c.shape[1])
    wo1 = w_out_odd[0].astype(BF16)
    wup1, wdn1 = w_ffn_up[1].astype(BF16), w_ffn_down[1].astype(BF16)

    def odd_layer(x, bsz, t, caches):
        m = x.shape[0]
        tm = min(ROW_TILE, m)
        r3 = lambda a: a.reshape(bsz, t, D_MODEL)
        if caches is None:
            assert BAND_PAST % tm == 0 and tm == min(BAND_PAST, t)
            q, kp, vtp, k_tail, v_tail = _proj_odd(x, gvec(g_pre_mix, 1), w_odd, tiles=t // tm,
                                                   pad_tiles=BAND_PAST // tm, v_feature_major=True)
            o = _band_t(r3(q), kp.reshape(bsz, BAND_PAST + t, D_MODEL), vtp, cvec, tq=256)
            new = (to_cache(k_tail), to_cache(v_tail))
        else:
            c_k, c_v = caches
            assert c_k.shape[1] == BAND_PAST and past % CHUNK == 0 and past >= BAND_PAST and t <= CHUNK
            q, k, v, k_tail, v_tail = _proj_odd(x, gvec(g_pre_mix, 1), w_odd, tiles=m // tm,
                                                pad_tiles=0, v_feature_major=False)
            per_stream = lambda tail: tail.reshape(D_MODEL, bsz, t).transpose(1, 0, 2)
            o, k_buf, v_buf = _sample_attn("band", r3(q), from_cache(c_k), from_cache(c_v), r3(k), r3(v),
                                           (cvec,), past=past, new_t=(per_stream(k_tail), per_stream(v_tail)))
            new = (to_cache(k_buf), to_cache(v_buf))
        x_out = _post([o.reshape(m, D_MODEL)], x, wo1, gvec(g_post_mix, 1), gvec(g_pre_ffn, 1),
                      wup1, wdn1, gvec(g_post_ffn, 1))
        return x_out, new

    xp, new_p_odd = odd_layer(xp, b_p, t_p, None)
    xs, new_s_odd = odd_layer(xs, b_s, t_s, (cache_c_k[0], cache_c_v[0]))

    return (xp.reshape(b_p, t_p, D_MODEL), xs.reshape(b_s, t_s, D_MODEL),
            *new_p_even, *new_p_odd, *new_s_even, *new_s_odd)
```
